```python
import math
import jax, jax.numpy as jnp
from jax import lax
import numpy as np

D_MODEL = 1024
BATCH = 32
SEQ = 2048
DEPTH = 2
DEC_BATCH = 2
DEC_SEQ = 8192
PAST_LEN = 128

HEAD_DIM = 64
GRID_W = 64
RMS_EPS = 1e-6
NEG_INF = -1e30

A_HEADS = 4
A_PATTERNS = ((128, 1), (512, 4), (2048, 16))
ROPE_THETA = 500000.0
ROPE_DIMS = HEAD_DIM // 4
B_HEADS = 4
B_CONV = 5
B_CHUNK = 64
C_HEADS = 4
C_KV_HEADS = 2
C_BLOCK = 128
AXIAL_THETA = 10000.0
D_HEADS = 4
NA_ROWS = 8
NA_COLS = 16

A_W = A_HEADS * HEAD_DIM
B_W = B_HEADS * HEAD_DIM
C_QW = C_HEADS * HEAD_DIM
C_KVW = C_KV_HEADS * HEAD_DIM
D_W = D_HEADS * HEAD_DIM
N_BRANCH = 4
BRANCH_W = 256
IN_WIDTHS = (3 * A_W, 3 * B_W, B_W, 4 * B_HEADS, C_QW, 2 * C_KVW, 3 * D_W)
IN_COLS = 3 * A_W + 4 * B_W + 4 * B_HEADS + C_QW + 2 * C_KVW + 3 * D_W

N_GROUPS = 4
EXPERTS_PER_GROUP = 4
TOP_K = 2
D_EXPERT = 256

kernel_name = "hybrid_bidir_encoder_griffin_merge"


def rms_norm(x, gain):
    xf = x.astype(jnp.float32)
    y = xf * lax.rsqrt(jnp.mean(xf * xf, axis=-1, keepdims=True) + RMS_EPS)
    return (y * gain.astype(jnp.float32)).astype(x.dtype)


def rotary(x, pos, theta):
    half = x.shape[-1] // 2
    inv_freq = jnp.float32(theta) ** (-jnp.arange(half, dtype=jnp.float32) / half)
    ang = pos.astype(jnp.float32)[:, None] * inv_freq[None, :]
    cos = jnp.cos(ang)[None, :, None, :]
    sin = jnp.sin(ang)[None, :, None, :]
    xf = x.astype(jnp.float32)
    x1, x2 = xf[..., :half], xf[..., half:]
    return jnp.concatenate([x1 * cos - x2 * sin, x2 * cos + x1 * sin], axis=-1).astype(x.dtype)


def partial_rotary(x, pos):
    return jnp.concatenate([rotary(x[..., :ROPE_DIMS], pos, ROPE_THETA), x[..., ROPE_DIMS:]], axis=-1)


def split_cols(u, widths):
    out, off = [], 0
    for w in widths:
        out.append(u[..., off:off + w])
        off += w
    return out


def to_strided(t, d):
    b, s = t.shape[:2]
    rest = t.shape[2:]
    return t.reshape((b, s // d, d) + rest).swapaxes(1, 2).reshape((b * d, s // d) + rest)


def from_strided(t, d, b):
    l = t.shape[1]
    rest = t.shape[2:]
    return t.reshape((b, d, l) + rest).swapaxes(1, 2).reshape((b, d * l) + rest)


def banded_attention(q, k, v, half):
    n, l, h, dh = q.shape
    blk = half
    nb = -(-l // blk)
    lp = nb * blk
    qb = jnp.pad(q, ((0, 0), (0, lp - l), (0, 0), (0, 0))).reshape(n, nb, blk, h, dh)
    pad_kv = ((0, 0), (blk, lp - l + blk), (0, 0), (0, 0))
    kp, vp = jnp.pad(k, pad_kv), jnp.pad(v, pad_kv)

    def band(t):
        return jnp.concatenate([t[:, i * blk:i * blk + lp].reshape(n, nb, blk, h, dh) for i in range(3)], axis=2)

    kb, vb = band(kp), band(vp)
    qpos = np.arange(lp).reshape(nb, blk)
    kpos = np.arange(nb)[:, None] * blk + np.arange(3 * blk)[None, :] - blk
    valid = ((np.abs(kpos[:, None, :] - qpos[:, :, None]) <= half)
             & (kpos[:, None, :] >= 0) & (kpos[:, None, :] < l))
    s = jnp.einsum('nbqhd,nbkhd->nbhqk', qb, kb).astype(jnp.float32) * (dh ** -0.5)
    s = jnp.where(valid[None, :, None], s, NEG_INF)
    m = jnp.max(s, axis=-1, keepdims=True)
    p = jnp.exp(s - m)
    den = jnp.sum(p, axis=-1)
    o = jnp.einsum('nbhqk,nbkhd->nbqhd', p, vb.astype(jnp.float32)) / jnp.swapaxes(den, 2, 3)[..., None]
    lse = jnp.swapaxes(m[..., 0] + jnp.log(den), 2, 3)
    return o.reshape(n, lp, h, dh)[:, :l], lse.reshape(n, lp, h)[:, :l]


def dilated_attention(q, k, v):
    b = q.shape[0]
    outs, lses = [], []
    for window, dil in A_PATTERNS:
        half = window // (2 * dil)
        o, lse = banded_attention(to_strided(q, dil), to_strided(k, dil), to_strided(v, dil), half)
        outs.append(from_strided(o, dil, b))
        lses.append(from_strided(lse, dil, b))
    w = jax.nn.softmax(jnp.stack(lses, axis=0), axis=0)
    o = jnp.sum(jnp.stack(outs, axis=0) * w[..., None], axis=0)
    return o.astype(q.dtype)


def short_conv(x, w):
    c = x.shape[-1]
    return lax.conv_general_dilated(x, w[:, None, :], window_strides=(1,),
                                    padding=((B_CONV // 2, B_CONV // 2),),
                                    dimension_numbers=('NWC', 'WIO', 'NWC'),
                                    feature_group_count=c)


def l2norm(x):
    xf = x.astype(jnp.float32)
    return xf * lax.rsqrt(jnp.sum(xf * xf, axis=-1, keepdims=True) + 1e-6)


def gated_delta_chunked(q, k, v, g, beta):
    b, s, h, dk = q.shape
    dv = v.shape[-1]
    c = B_CHUNK
    n = s // c

    def chunk(t):
        t = t.reshape((b, n, c, h) + t.shape[3:])
        return jnp.moveaxis(t, 3, 1)

    qc, kc, vc = chunk(q * (dk ** -0.5)), chunk(k), chunk(v)
    gc = jnp.cumsum(chunk(g), axis=-1)
    bc = chunk(beta)
    tri_incl = np.tril(np.ones((c, c), dtype=bool))
    tri_strict = np.tril(np.ones((c, c), dtype=bool), -1)
    decay = jnp.exp(jnp.where(tri_incl, gc[..., :, None] - gc[..., None, :], NEG_INF))
    kbeta = kc * bc[..., None]
    lower = jnp.where(tri_strict, jnp.einsum('bhncd,bhnmd->bhncm', kbeta, kc) * decay, 0.0)
    a_mat = lower + jnp.eye(c, dtype=jnp.float32)
    rhs = jnp.concatenate([vc * bc[..., None], kbeta * jnp.exp(gc)[..., None]], axis=-1)
    sol = lax.linalg.triangular_solve(a_mat, rhs, left_side=True, lower=True, unit_diagonal=True)
    u, w = sol[..., :dv], sol[..., dv:]
    intra = jnp.where(tri_incl, jnp.einsum('bhncd,bhnmd->bhncm', qc, kc) * decay, 0.0)
    g_last = gc[..., -1]
    k_tail = kc * jnp.exp(g_last[..., None] - gc)[..., None]
    q_head = qc * jnp.exp(gc)[..., None]

    def step(state, xs):
        qh, kt, ui, wi, ai, gl = xs
        v_new = ui - jnp.einsum('bhck,bhkv->bhcv', wi, state)
        o = jnp.einsum('bhck,bhkv->bhcv', qh, state) + jnp.einsum('bhcm,bhmv->bhcv', ai, v_new)
        state = state * jnp.exp(gl)[..., None, None] + jnp.einsum('bhck,bhcv->bhkv', kt, v_new)
        return state, o

    xs = tuple(jnp.moveaxis(t, 2, 0) for t in (q_head, k_tail, u, w, intra, g_last))
    state0 = jnp.zeros((b, h, dk, dv), jnp.float32)
    _, o = lax.scan(step, state0, xs)
    return o.transpose(1, 0, 3, 2, 4).reshape(b, s, h, dv)


def gated_deltanet(qkv, z, ab, conv_w, a_log, dt_bias, out_gain):
    b, s, _ = qkv.shape
    qkv = jax.nn.silu(short_conv(qkv, conv_w))
    q, k, v = (t.reshape(b, s, B_HEADS, HEAD_DIM) for t in jnp.split(qkv, 3, axis=-1))
    q, k, v = l2norm(q), l2norm(k), v.astype(jnp.float32)
    ab = ab.astype(jnp.float32).reshape(b, s, 2, 2, B_HEADS)
    a_log = a_log.astype(jnp.float32)
    dt_bias = dt_bias.astype(jnp.float32)
    o = jnp.zeros((b, s, B_HEADS, HEAD_DIM), jnp.float32)
    for d in range(2):
        beta = jax.nn.sigmoid(ab[:, :, 0, d])
        g = -jnp.exp(a_log[d]) * jax.nn.softplus(ab[:, :, 1, d] + dt_bias[d])
        if d == 0:
            o = o + gated_delta_chunked(q, k, v, g, beta)
        else:
            fl = lambda t: jnp.flip(t, axis=1)
            o = o + fl(gated_delta_chunked(fl(q), fl(k), fl(v), fl(g), fl(beta)))
    zf = z.astype(jnp.float32).reshape(b, s, B_HEADS, HEAD_DIM)
    o = o * lax.rsqrt(jnp.mean(o * o, axis=-1, keepdims=True) + RMS_EPS) * out_gain.astype(jnp.float32) * jax.nn.silu(zf)
    return o.reshape(b, s, B_W).astype(qkv.dtype)


def axial_gqa(q, k, v, q_gain, k_gain):
    b, s = q.shape[:2]
    q = rms_norm(q.reshape(b, s, C_HEADS, HEAD_DIM), q_gain)
    k = rms_norm(k.reshape(b, s, C_KV_HEADS, HEAD_DIM), k_gain)
    v = v.reshape(b, s, C_KV_HEADS, HEAD_DIM)
    t = jnp.arange(s)
    rows, cols = t // GRID_W, t % GRID_W
    half = HEAD_DIM // 2

    def axial(x):
        return jnp.concatenate([rotary(x[..., :half], rows, AXIAL_THETA), rotary(x[..., half:], cols, AXIAL_THETA)], axis=-1)

    q, k = axial(q), axial(k)
    grp = C_HEADS // C_KV_HEADS
    nblk = s // C_BLOCK
    qb = jnp.moveaxis(q.reshape(b, nblk, C_BLOCK, C_KV_HEADS, grp, HEAD_DIM), 1, 0)
    scale = HEAD_DIM ** -0.5

    def block(qi):
        sc = jnp.einsum('bqhgd,bkhd->bhgqk', qi, k).astype(jnp.float32) * scale
        p = jax.nn.softmax(sc, axis=-1)
        return jnp.einsum('bhgqk,bkhd->bqhgd', p.astype(v.dtype), v)

    o = lax.map(block, qb)
    return jnp.moveaxis(o, 0, 1).reshape(b, s, C_QW)


def neighbourhood_attention(q, k, v, rpb):
    b, s = q.shape[:2]
    rows = s // GRID_W
    kr = min(NA_ROWS, rows)
    qg = q.reshape(b, rows, GRID_W, D_HEADS, HEAD_DIM)
    kgrid = k.reshape(b, rows, GRID_W, D_HEADS, HEAD_DIM)
    vgrid = v.reshape(b, rows, GRID_W, D_HEADS, HEAD_DIM)
    r = np.arange(rows)
    rs = np.clip(r - kr // 2, 0, rows - kr)
    key_rows = rs[:, None] + np.arange(kr)[None, :]
    c = np.arange(GRID_W)
    cs = np.clip(c - NA_COLS // 2, 0, GRID_W - NA_COLS)
    col_valid = (c[None, :] >= cs[:, None]) & (c[None, :] < cs[:, None] + NA_COLS)
    dr = key_rows - r[:, None] + NA_ROWS - 1
    dc = np.clip(c[None, :] - c[:, None], -(NA_COLS - 1), NA_COLS - 1) + NA_COLS - 1
    bias = rpb[:, dr[:, None, :, None], dc[None, :, None, :]]
    bias = bias.reshape(D_HEADS, rows, GRID_W, kr * GRID_W).transpose(1, 0, 2, 3)
    mask = np.broadcast_to(col_valid[:, None, :], (GRID_W, kr, GRID_W)).reshape(GRID_W, kr * GRID_W)
    kg = kgrid[:, key_rows].reshape(b, rows, kr * GRID_W, D_HEADS, HEAD_DIM)
    vg = vgrid[:, key_rows].reshape(b, rows, kr * GRID_W, D_HEADS, HEAD_DIM)
    sc = jnp.einsum('brqhd,brkhd->brhqk', qg, kg).astype(jnp.float32) * (HEAD_DIM ** -0.5) + bias.astype(jnp.float32)[None]
    sc = jnp.where(mask, sc, NEG_INF)
    p = jax.nn.softmax(sc, axis=-1)
    o = jnp.einsum('brhqk,brkhd->brqhd', p.astype(v.dtype), vg)
    return o.reshape(b, s, D_W)


def hier_moe(h, w_rg, b_rg, w_re, b_re, w_gt, w_up, w_dn):
    shp = h.shape
    t = h.reshape(-1, shp[-1])
    gl = (t @ w_rg).astype(jnp.float32) + b_rg.astype(jnp.float32)
    gp, gi = lax.top_k(jax.nn.softmax(gl, axis=-1), 1)
    gw = jax.nn.one_hot(gi[:, 0], N_GROUPS, dtype=jnp.float32)
    el = jnp.einsum('td,dge->tge', t, w_re).astype(jnp.float32) + b_re.astype(jnp.float32)
    el = jnp.einsum('tge,tg->te', el, gw)
    tv, ti = lax.top_k(el, TOP_K)
    tw = jax.nn.softmax(tv, axis=-1) * gp
    ew = jnp.sum(jax.nn.one_hot(ti, EXPERTS_PER_GROUP, dtype=jnp.float32) * tw[..., None], axis=1)
    out = jnp.zeros(t.shape, jnp.float32)
    for g in range(N_GROUPS):
        wg = (gw[:, g:g + 1] * ew).astype(t.dtype)
        hid = jax.nn.silu(jnp.einsum('td,edf->tef', t, w_gt[g])) * jnp.einsum('td,edf->tef', t, w_up[g])
        out = out + jnp.einsum('tef,efd->td', hid * wg[..., None], w_dn[g])
    return out.astype(h.dtype).reshape(shp)


def encoder(x, p):
    b, s, _ = x.shape
    pos = jnp.arange(s)
    for l in range(DEPTH):
        h = rms_norm(x, p["norm_mix"][l])
        u = h @ p["w_in"][l]
        a_qkv, b_qkv, b_z, b_ab, c_q, c_kv, d_qkv = split_cols(u, IN_WIDTHS)
        qa, ka, va = (t.reshape(b, s, A_HEADS, HEAD_DIM) for t in jnp.split(a_qkv, 3, axis=-1))
        o_a = dilated_attention(partial_rotary(qa, pos), partial_rotary(ka, pos), va).reshape(b, s, A_W)
        o_b = gated_deltanet(b_qkv, b_z, b_ab, p["conv_w"][l], p["a_log"][l], p["dt_bias"][l], p["b_norm"][l])
        ck, cv = jnp.split(c_kv, 2, axis=-1)
        o_c = axial_gqa(c_q, ck, cv, p["c_q_norm"][l], p["c_k_norm"][l])
        qd, kd, vd = (t.reshape(b, s, D_HEADS, HEAD_DIM) for t in jnp.split(d_qkv, 3, axis=-1))
        o_d = neighbourhood_attention(qd, kd, vd, p["d_rpb"][l])
        y = jnp.zeros_like(x)
        for i, o in enumerate((o_a, o_b, o_c, o_d)):
            gate = jax.nn.sigmoid(h @ p["w_gate"][l, i] + p["b_gate"][l, i])
            y = y + gate * (o @ p["w_branch"][l, i])
        x = x + y @ p["w_out"][l]
        h2 = rms_norm(x, p["norm_ffn"][l])
        x = x + hier_moe(h2, p["w_route_group"][l], p["b_route_group"][l], p["w_route_expert"][l],
                         p["b_route_expert"][l], p["w_ff_gate"][l], p["w_ff_up"][l], p["w_ff_down"][l])
    return rms_norm(x, p["norm_final"])


def setup_inputs(seed: int = 0) -> dict:
    key = jax.random.key(seed)
    ks = jax.random.split(key, 24)
    f32 = jnp.float32
    nrm = lambda k, shape, scale: jax.random.normal(k, shape, f32) * scale
    dt = jnp.exp(jax.random.uniform(ks[5], (DEPTH, 2, B_HEADS), f32, math.log(1e-3), math.log(1e-1)))
    return {
        "x_prompt": nrm(ks[0], (BATCH, SEQ, D_MODEL), 1.0),
        "x_sample": nrm(ks[1], (DEC_BATCH, DEC_SEQ, D_MODEL), 1.0),
        "norm_mix": 1.0 + nrm(ks[2], (DEPTH, D_MODEL), 0.02),
        "w_in": nrm(ks[3], (DEPTH, D_MODEL, IN_COLS), D_MODEL ** -0.5),
        "conv_w": nrm(ks[4], (DEPTH, B_CONV, 3 * B_W), B_CONV ** -0.5),
        "a_log": jnp.log(jax.random.uniform(ks[6], (DEPTH, 2, B_HEADS), f32, 1.0, 16.0)),
        "dt_bias": dt + jnp.log(-jnp.expm1(-dt)),
        "b_norm": 1.0 + nrm(ks[7], (DEPTH, HEAD_DIM), 0.02),
        "c_q_norm": 1.0 + nrm(ks[8], (DEPTH, HEAD_DIM), 0.02),
        "c_k_norm": 1.0 + nrm(ks[9], (DEPTH, HEAD_DIM), 0.02),
        "d_rpb": nrm(ks[10], (DEPTH, D_HEADS, 2 * NA_ROWS - 1, 2 * NA_COLS - 1), 0.02),
        "w_gate": nrm(ks[11], (DEPTH, N_BRANCH, D_MODEL, D_MODEL), D_MODEL ** -0.5),
        "b_gate": nrm(ks[12], (DEPTH, N_BRANCH, D_MODEL), 0.02),
        "w_branch": nrm(ks[13], (DEPTH, N_BRANCH, BRANCH_W, D_MODEL), BRANCH_W ** -0.5),
        "w_out": nrm(ks[14], (DEPTH, D_MODEL, D_MODEL), D_MODEL ** -0.5),
        "norm_ffn": 1.0 + nrm(ks[15], (DEPTH, D_MODEL), 0.02),
        "w_route_group": nrm(ks[16], (DEPTH, D_MODEL, N_GROUPS), D_MODEL ** -0.5),
        "b_route_group": nrm(ks[17], (DEPTH, N_GROUPS), 0.01),
        "w_route_expert": nrm(ks[18], (DEPTH, D_MODEL, N_GROUPS, EXPERTS_PER_GROUP), D_MODEL ** -0.5),
        "b_route_expert": nrm(ks[19], (DEPTH, N_GROUPS, EXPERTS_PER_GROUP), 0.01),
        "w_ff_gate": nrm(ks[20], (DEPTH, N_GROUPS, EXPERTS_PER_GROUP, D_MODEL, D_EXPERT), D_MODEL ** -0.5),
        "w_ff_up": nrm(ks[21], (DEPTH, N_GROUPS, EXPERTS_PER_GROUP, D_MODEL, D_EXPERT), D_MODEL ** -0.5),
        "w_ff_down": nrm(ks[22], (DEPTH, N_GROUPS, EXPERTS_PER_GROUP, D_EXPERT, D_MODEL), D_EXPERT ** -0.5),
        "norm_final": 1.0 + nrm(ks[23], (D_MODEL,), 0.02),
    }


def reference(x_prompt, x_sample, norm_mix, w_in, conv_w, a_log, dt_bias, b_norm, c_q_norm, c_k_norm,
              d_rpb, w_gate, b_gate, w_branch, w_out, norm_ffn, w_route_group, b_route_group,
              w_route_expert, b_route_expert, w_ff_gate, w_ff_up, w_ff_down, norm_final):
    p = dict(norm_mix=norm_mix, w_in=w_in, conv_w=conv_w, a_log=a_log, dt_bias=dt_bias, b_norm=b_norm,
             c_q_norm=c_q_norm, c_k_norm=c_k_norm, d_rpb=d_rpb, w_gate=w_gate, b_gate=b_gate,
             w_branch=w_branch, w_out=w_out, norm_ffn=norm_ffn, w_route_group=w_route_group,
             b_route_group=b_route_group, w_route_expert=w_route_expert, b_route_expert=b_route_expert,
             w_ff_gate=w_ff_gate, w_ff_up=w_ff_up, w_ff_down=w_ff_down, norm_final=norm_final)
    y_prompt = encoder(x_prompt, p)
    y_sample = encoder(x_sample, p)
    return (y_prompt, y_sample)
```

```python
import functools
import math

import jax
import jax.numpy as jnp
import numpy as np
from jax import lax
from jax.experimental import pallas as pl
from jax.experimental.pallas import tpu as pltpu

D_MODEL = 1024
HEAD_DIM = 64
N_HEADS = 4
MIX_W = N_HEADS * HEAD_DIM
GRID_W = 64
RMS_EPS = 1e-6
NEG_INF = -1e30

A_PATTERNS = ((128, 1), (512, 4), (2048, 16))
ROPE_THETA = 500000.0
ROPE_DIMS = HEAD_DIM // 4
AXIAL_THETA = 10000.0
C_KV_HEADS = 2
B_CONV = 5
B_CHUNK = 64
NA_ROWS = 8
NA_COLS = 16
N_GROUPS = 4
EXPERTS_PER_GROUP = 4
N_EXPERTS = N_GROUPS * EXPERTS_PER_GROUP
D_EXPERT = 256

LANES = 128
BF16_SUBLANES = 16
VMEM_LIMIT = 56 * 1024 * 1024

F32 = jnp.float32
BF16 = jnp.bfloat16


def _params(*sem):
    return pltpu.CompilerParams(dimension_semantics=sem, vmem_limit_bytes=VMEM_LIMIT)


def _const_spec(shape):
    n = len(shape)
    return pl.BlockSpec(shape, lambda *_: (0,) * n)


def _rms_rows(x, gain):
    return x * lax.rsqrt(jnp.mean(x * x, axis=-1, keepdims=True) + RMS_EPS) * gain


def _head_ids(width):
    return lax.broadcasted_iota(jnp.int32, (1, width), 1) // HEAD_DIM


def _block_ones(width):
    r = lax.broadcasted_iota(jnp.int32, (width, width), 0) // HEAD_DIM
    c = lax.broadcasted_iota(jnp.int32, (width, width), 1) // HEAD_DIM
    return jnp.where(r == c, 1.0, 0.0).astype(BF16)


def _head_sums(x):
    return jnp.dot(x.astype(BF16), _block_ones(x.shape[-1]), preferred_element_type=F32)


def _rotate(x, cos, sin_lo, sin_hi, shift):
    w = x.shape[-1]
    return x * cos + pltpu.roll(x, shift, 1) * sin_hi + pltpu.roll(x, w - shift, 1) * sin_lo


def _in_proj_kernel(x_ref, gain_ref, wa_ref, wb_ref, wz_ref, wab_ref, wcq_ref, wckv_ref, wd_ref,
                    cqg_ref, ckg_ref, rot_a_ref, rot_c_ref,
                    a_ref, b_ref, z_ref, ab_ref, cq_ref, ckv_ref, d_ref):
    h = _rms_rows(x_ref[...], gain_ref[...]).astype(BF16)
    dot = lambda w_ref: jnp.dot(h, w_ref[...], preferred_element_type=F32)
    scale = HEAD_DIM ** -0.5

    ua = dot(wa_ref)
    cos, s_lo, s_hi = rot_a_ref[0], rot_a_ref[1], rot_a_ref[2]
    half = ROPE_DIMS // 2
    a_ref[:, 0:MIX_W] = (_rotate(ua[:, 0:MIX_W], cos, s_lo, s_hi, half) * scale).astype(BF16)
    a_ref[:, MIX_W:2 * MIX_W] = _rotate(ua[:, MIX_W:2 * MIX_W], cos, s_lo, s_hi, half).astype(BF16)
    a_ref[:, 2 * MIX_W:] = ua[:, 2 * MIX_W:].astype(BF16)

    b_ref[...] = dot(wb_ref).astype(BF16)
    z_ref[...] = dot(wz_ref).astype(BF16)
    ab_ref[...] = dot(wab_ref)[:, :ab_ref.shape[-1]]

    cos, s_lo, s_hi = rot_c_ref[0], rot_c_ref[1], rot_c_ref[2]
    quarter = HEAD_DIM // 4
    uq = dot(wcq_ref)
    uq = uq * lax.rsqrt(_head_sums(uq * uq) * (1.0 / HEAD_DIM) + RMS_EPS) * cqg_ref[...]
    cq_ref[...] = (_rotate(uq, cos, s_lo, s_hi, quarter) * scale).astype(BF16)
    ukv = dot(wckv_ref)
    kw = C_KV_HEADS * HEAD_DIM
    uk = ukv[:, :kw]
    uk = uk * lax.rsqrt(_head_sums(uk * uk) * (1.0 / HEAD_DIM) + RMS_EPS) * ckg_ref[...]
    ckv_ref[:, :kw] = _rotate(uk, cos[:, :kw], s_lo[:, :kw], s_hi[:, :kw], quarter).astype(BF16)
    ckv_ref[:, kw:] = ukv[:, kw:].astype(BF16)

    ud = dot(wd_ref)
    d_ref[:, 0:MIX_W] = (ud[:, 0:MIX_W] * scale).astype(BF16)
    d_ref[:, MIX_W:] = ud[:, MIX_W:].astype(BF16)


def _rotary_tables(seq):
    pos = np.arange(seq, dtype=np.float64)

    def table(pos_of_lane, theta, half, lane_in_group, active):
        freq_idx = lane_in_group % half
        inv = theta ** (-(freq_idx.astype(np.float64)) / half)
        ang = pos_of_lane * inv[None, :]
        cos = np.where(active[None, :], np.cos(ang), 1.0)
        sin = np.where(active[None, :], np.sin(ang), 0.0)
        low = (lane_in_group < half)[None, :]
        return np.stack([cos, np.where(low, -sin, 0.0), np.where(low, 0.0, sin)]).astype(np.float32)

    lane = np.arange(MIX_W) % HEAD_DIM
    rot_a = table(np.broadcast_to(pos[:, None], (seq, MIX_W)), ROPE_THETA, ROPE_DIMS // 2, lane % ROPE_DIMS,
                  lane < ROPE_DIMS)
    rows, cols = pos // GRID_W, pos % GRID_W
    half = HEAD_DIM // 2
    pos_c = np.where((lane < half)[None, :], rows[:, None], cols[:, None])
    rot_c = table(pos_c, AXIAL_THETA, half // 2, lane % half, np.ones(MIX_W, bool))
    return jnp.asarray(rot_a), jnp.asarray(rot_c)


def _in_proj(x2d, seq, lw, tables, tm):
    t = x2d.shape[0]
    rot_a, rot_c = tables
    nblk = seq // tm
    tok = lambda w: pl.BlockSpec((tm, w), lambda i: (i, 0))
    rot = pl.BlockSpec((3, tm, MIX_W), lambda i: (0, i % nblk, 0))
    weights = (lw["norm_mix"], lw["wa"], lw["wb"], lw["wz"], lw["wab"], lw["wcq"], lw["wckv"], lw["wd"],
               lw["cq_gain"], lw["ck_gain"])
    out_widths = (3 * MIX_W, 3 * MIX_W, MIX_W, 16, MIX_W, 2 * C_KV_HEADS * HEAD_DIM, 3 * MIX_W)
    out_dtypes = (BF16, BF16, BF16, F32, BF16, BF16, BF16)
    return pl.pallas_call(
        _in_proj_kernel,
        grid=(t // tm,),
        in_specs=[tok(D_MODEL)] + [_const_spec(w.shape) for w in weights] + [rot, rot],
        out_specs=[tok(w) for w in out_widths],
        out_shape=[jax.ShapeDtypeStruct((t, w), d) for w, d in zip(out_widths, out_dtypes)],
        compiler_params=_params("parallel"),
    )(x2d, *weights, rot_a, rot_c)


def _gqa_kernel(q_ref, kv_ref, o_ref, m_ref, l_ref, acc_ref, *, tk):
    tq = q_ref.shape[0]
    seq = kv_ref.shape[0]
    kw = C_KV_HEADS * HEAD_DIM
    q = q_ref[...]
    lane_g = lax.broadcasted_iota(jnp.int32, (1, kw), 1) // HEAD_DIM
    zero = jnp.zeros((), BF16)
    qs = jnp.concatenate([jnp.where(lane_g == g, q[:, blk * kw:(blk + 1) * kw], zero)
                          for g in range(C_KV_HEADS) for blk in range(2)], axis=0)
    m_ref[...] = jnp.full(m_ref.shape, NEG_INF, F32)
    l_ref[...] = jnp.zeros(l_ref.shape, F32)
    acc_ref[...] = jnp.zeros(acc_ref.shape, F32)

    def body(j, carry):
        start = pl.multiple_of(j * tk, tk)
        k = kv_ref[pl.ds(start, tk), 0:kw]
        v = kv_ref[pl.ds(start, tk), kw:2 * kw]
        s = lax.dot_general(qs, k, (((1,), (1,)), ((), ())), preferred_element_type=F32)
        m_old = m_ref[...]
        m_new = jnp.maximum(m_old, jnp.max(s, axis=-1, keepdims=True))
        alpha = jnp.exp(m_old - m_new)
        p = jnp.exp(s - m_new)
        l_ref[...] = l_ref[...] * alpha + jnp.sum(p, axis=-1, keepdims=True)
        acc_ref[...] = acc_ref[...] * alpha + jnp.dot(p.astype(BF16), v, preferred_element_type=F32)
        m_ref[...] = m_new
        return carry

    lax.fori_loop(0, seq // tk, body, 0)
    o = acc_ref[...] / l_ref[...]
    lane_lo = lax.broadcasted_iota(jnp.int32, (1, kw), 1) < HEAD_DIM
    for blk in range(2):
        lo = o[blk * tq:(blk + 1) * tq]
        hi = o[(2 + blk) * tq:(3 + blk) * tq]
        o_ref[:, blk * kw:(blk + 1) * kw] = jnp.where(lane_lo, lo, hi).astype(o_ref.dtype)


def _gqa(cq, ckv, batch, seq, tq, tk):
    nq = seq // tq
    return pl.pallas_call(
        functools.partial(_gqa_kernel, tk=tk),
        grid=(batch, nq),
        in_specs=[pl.BlockSpec((tq, MIX_W), lambda b, i: (b * nq + i, 0)),
                  pl.BlockSpec((seq, 2 * C_KV_HEADS * HEAD_DIM), lambda b, i: (b, 0))],
        out_specs=pl.BlockSpec((tq, MIX_W), lambda b, i: (b * nq + i, 0)),
        out_shape=jax.ShapeDtypeStruct((batch * seq, MIX_W), BF16),
        scratch_shapes=[pltpu.VMEM((4 * tq, 1), F32), pltpu.VMEM((4 * tq, 1), F32),
                        pltpu.VMEM((4 * tq, C_KV_HEADS * HEAD_DIM), F32)],
        compiler_params=_params("parallel", "arbitrary"),
    )(cq, ckv)


def _stack_heads(q):
    ids = _head_ids(MIX_W)
    zero = jnp.zeros((), q.dtype)
    return jnp.concatenate([jnp.where(ids == h, q, zero) for h in range(N_HEADS)], axis=0)


def _unstack_heads(o, n):
    ids = _head_ids(MIX_W)
    out = o[0:n]
    for h in range(1, N_HEADS):
        out = jnp.where(ids == h, o[h * n:(h + 1) * n], out)
    return out


def _nbr_kernel(q_ref, k_ref, v_ref, bias_ref, o_ref, *, rows_per_step, grid_rows, kr):
    j = pl.program_id(1)
    win = kr * GRID_W

    def body(i, carry):
        r = j * rows_per_step + i
        rs = jnp.clip(r - kr // 2, 0, grid_rows - kr)
        kstart = pl.multiple_of(rs * GRID_W, GRID_W)
        qstart = pl.multiple_of(i * GRID_W, GRID_W)
        qs = _stack_heads(q_ref[pl.ds(qstart, GRID_W), :])
        s = lax.dot_general(qs, k_ref[pl.ds(kstart, win), :], (((1,), (1,)), ((), ())),
                            preferred_element_type=F32)
        s = s + bias_ref[r - rs]
        m = jnp.max(s, axis=-1, keepdims=True)
        p = jnp.exp(s - m)
        l = jnp.sum(p, axis=-1, keepdims=True)
        o = jnp.dot(p.astype(BF16), v_ref[pl.ds(kstart, win), :], preferred_element_type=F32) / l
        o_ref[pl.ds(qstart, GRID_W), :] = _unstack_heads(o, GRID_W).astype(o_ref.dtype)
        return carry

    lax.fori_loop(0, rows_per_step, body, 0)


def _nbr_bias(rpb, kr):
    c = np.arange(GRID_W)
    cs = np.clip(c - NA_COLS // 2, 0, GRID_W - NA_COLS)
    col_valid = (c[None, :] >= cs[:, None]) & (c[None, :] < cs[:, None] + NA_COLS)
    dc = np.clip(c[None, :] - c[:, None], -(NA_COLS - 1), NA_COLS - 1) + NA_COLS - 1
    dr = np.arange(kr)[None, :] - np.arange(kr)[:, None] + NA_ROWS - 1
    bias = rpb.astype(F32)[:, dr[:, None, :, None], dc[None, :, None, :]]
    bias = jnp.where(col_valid[None, None, :, None, :], bias, NEG_INF)
    return bias.transpose(1, 0, 2, 3, 4).reshape(kr, N_HEADS * GRID_W, kr * GRID_W)


def _nbr(dqkv, bias, batch, seq, rows_per_step):
    grid_rows = seq // GRID_W
    kr = min(NA_ROWS, grid_rows)
    nblk = grid_rows // rows_per_step
    tq = rows_per_step * GRID_W
    return pl.pallas_call(
        functools.partial(_nbr_kernel, rows_per_step=rows_per_step, grid_rows=grid_rows, kr=kr),
        grid=(batch, nblk),
        in_specs=[pl.BlockSpec((tq, MIX_W), lambda b, i: (b * nblk + i, 0)),
                  pl.BlockSpec((seq, MIX_W), lambda b, i: (b, 1)),
                  pl.BlockSpec((seq, MIX_W), lambda b, i: (b, 2)),
                  _const_spec(bias.shape)],
        out_specs=pl.BlockSpec((tq, MIX_W), lambda b, i: (b * nblk + i, 0)),
        out_shape=jax.ShapeDtypeStruct((batch * seq, MIX_W), BF16),
        compiler_params=_params("parallel", "arbitrary"),
    )(dqkv, dqkv, dqkv, bias)


def _band_kernel(q_ref, k_ref, v_ref, o_ref, lse_ref, *, half, kwin):
    tq = q_ref.shape[0]
    length = k_ref.shape[0]
    i0 = pl.program_id(2) * tq
    kstart = pl.multiple_of(jnp.clip(i0 - half, 0, length - kwin), BF16_SUBLANES)
    qs = _stack_heads(q_ref[...])
    s = lax.dot_general(qs, k_ref[pl.ds(kstart, kwin), :], (((1,), (1,)), ((), ())),
                        preferred_element_type=F32)
    qpos = i0 + lax.broadcasted_iota(jnp.int32, (N_HEADS * tq, kwin), 0) % tq
    kpos = kstart + lax.broadcasted_iota(jnp.int32, (N_HEADS * tq, kwin), 1)
    s = jnp.where(jnp.abs(qpos - kpos) <= half, s, NEG_INF)
    m = jnp.max(s, axis=-1, keepdims=True)
    p = jnp.exp(s - m)
    l = jnp.sum(p, axis=-1, keepdims=True)
    o = jnp.dot(p.astype(BF16), v_ref[pl.ds(kstart, kwin), :], preferred_element_type=F32) / l
    o_ref[...] = _unstack_heads(o, tq).astype(o_ref.dtype)
    lse = jnp.broadcast_to(m + jnp.log(l), (N_HEADS * tq, MIX_W))
    lse_ref[...] = _unstack_heads(lse, tq)


def _band(a, batch, seq, window, dil, tq):
    half = window // (2 * dil)
    length = seq // dil
    tq = min(tq, length)
    kwin = min(tq + 2 * half, length)
    nq = length // tq
    view = a.reshape(batch, length, dil * 3 * MIX_W)
    qkv = lambda part: (lambda b, r, i: (b, 0, 3 * r + part))
    out = pl.BlockSpec((None, tq, MIX_W), lambda b, r, i: (b, i, r))
    o, lse = pl.pallas_call(
        functools.partial(_band_kernel, half=half, kwin=kwin),
        grid=(batch, dil, nq),
        in_specs=[pl.BlockSpec((None, tq, MIX_W), lambda b, r, i: (b, i, 3 * r)),
                  pl.BlockSpec((None, length, MIX_W), qkv(1)),
                  pl.BlockSpec((None, length, MIX_W), qkv(2))],
        out_specs=[out, out],
        out_shape=[jax.ShapeDtypeStruct((batch, length, dil * MIX_W), BF16),
                   jax.ShapeDtypeStruct((batch, length, dil * MIX_W), F32)],
        compiler_params=_params("parallel", "parallel", "arbitrary"),
    )(view, view, view)
    return o.reshape(batch * seq, MIX_W), lse.reshape(batch * seq, MIX_W)


CUM_ROWS = 256


def _chunk_tri(n, upper):
    r = lax.broadcasted_iota(jnp.int32, (n, n), 0)
    c = lax.broadcasted_iota(jnp.int32, (n, n), 1)
    same = (r // B_CHUNK) == (c // B_CHUNK)
    keep = (r <= c) if upper else (r >= c)
    return jnp.where(same & keep, 1.0, 0.0).astype(F32)


def _delta_prep_kernel(x_ref, prev_ref, next_ref, ab_ref, conv_ref, alog_ref, dtb_ref,
                       qkv_ref, gate_ref, xe_ref):
    j = pl.program_id(1)
    rows = x_ref.shape[0]
    pad = BF16_SUBLANES
    keep_prev = jnp.where(j > 0, 1.0, 0.0)
    keep_next = jnp.where(j < pl.num_programs(1) - 1, 1.0, 0.0)
    xe_ref[0:pad, :] = prev_ref[...].astype(F32) * keep_prev
    xe_ref[pad:pad + rows, :] = x_ref[...].astype(F32)
    xe_ref[pad + rows:, :] = next_ref[...].astype(F32) * keep_next
    centre = B_CONV // 2
    y = jnp.zeros((rows, 3 * MIX_W), F32)
    for tap in range(B_CONV):
        y = y + xe_ref[pad + tap - centre:pad + tap - centre + rows, :] * conv_ref[tap:tap + 1, :]
    y = y * jax.nn.sigmoid(y)
    q, k = y[:, 0:MIX_W], y[:, MIX_W:2 * MIX_W]
    q = q * lax.rsqrt(_head_sums(q * q) + 1e-6) * (HEAD_DIM ** -0.5)
    k = k * lax.rsqrt(_head_sums(k * k) + 1e-6)
    qkv_ref[:, 0:MIX_W] = q.astype(BF16)
    qkv_ref[:, MIX_W:2 * MIX_W] = k.astype(BF16)
    qkv_ref[:, 2 * MIX_W:] = y[:, 2 * MIX_W:].astype(BF16)

    ab = ab_ref[...]
    col = lax.broadcasted_iota(jnp.int32, (1, ab.shape[-1]), 1)
    xs = ab + dtb_ref[...]
    softplus = jnp.maximum(xs, 0.0) + jnp.log(1.0 + jnp.exp(-jnp.abs(xs)))
    g = -jnp.exp(alog_ref[...]) * softplus
    beta = jax.nn.sigmoid(ab)
    tri_f, tri_b = _chunk_tri(CUM_ROWS, False), _chunk_tri(CUM_ROWS, True)
    for r0 in range(0, rows, CUM_ROWS):
        gp = g[r0:r0 + CUM_ROWS]
        cum_f = jnp.dot(tri_f, gp, preferred_element_type=F32, precision=lax.Precision.HIGHEST)
        cum_b = jnp.dot(tri_b, gp, preferred_element_type=F32, precision=lax.Precision.HIGHEST)
        gate_ref[r0:r0 + CUM_ROWS, :] = jnp.where(col < 8, beta[r0:r0 + CUM_ROWS],
                                                   jnp.where(col < 12, cum_f, cum_b))


def _delta_prep(bqkv, ab, lw, batch, seq, rows):
    t = batch * seq
    nblk = seq // rows
    per = rows // BF16_SUBLANES
    last = t // BF16_SUBLANES - 1
    w3 = 3 * MIX_W
    return pl.pallas_call(
        _delta_prep_kernel,
        grid=(batch, nblk),
        in_specs=[pl.BlockSpec((rows, w3), lambda b, j: (b * nblk + j, 0)),
                  pl.BlockSpec((BF16_SUBLANES, w3), lambda b, j: (jnp.maximum((b * nblk + j) * per - 1, 0), 0)),
                  pl.BlockSpec((BF16_SUBLANES, w3), lambda b, j: (jnp.minimum((b * nblk + j + 1) * per, last), 0)),
                  pl.BlockSpec((rows, 16), lambda b, j: (b * nblk + j, 0)),
                  _const_spec(lw["conv_w"].shape), _const_spec((1, 16)), _const_spec((1, 16))],
        out_specs=[pl.BlockSpec((rows, w3), lambda b, j: (b * nblk + j, 0)),
                   pl.BlockSpec((rows, 16), lambda b, j: (b * nblk + j, 0))],
        out_shape=[jax.ShapeDtypeStruct((t, w3), BF16), jax.ShapeDtypeStruct((t, 16), F32)],
        scratch_shapes=[pltpu.VMEM((rows + 2 * BF16_SUBLANES, w3), F32)],
        compiler_params=_params("parallel", "arbitrary"),
    )(bqkv, bqkv, bqkv, ab, lw["conv_w"], lw["a_log16"], lw["dt_bias16"])


def _expand_heads(cols):
    ids = _head_ids(MIX_W)
    out = jnp.broadcast_to(cols[:, 0:1], (cols.shape[0], MIX_W))
    for h in range(1, N_HEADS):
        out = jnp.where(ids == h, jnp.broadcast_to(cols[:, h:h + 1], out.shape), out)
    return out


def _block_diag(packed, mask):
    return jnp.where(mask, jnp.concatenate([packed] * N_HEADS, axis=0), 0.0).astype(BF16)


def _mm(a, b):
    return jnp.dot(a.astype(BF16), b.astype(BF16), preferred_element_type=F32)


def _delta_chunk(backward, qkv_ref, gate_ref, grow_ref, o_ref, s_ref, c):
    n = B_CHUNK
    r0 = pl.multiple_of(c * n, n)
    q = qkv_ref[pl.ds(r0, n), 0:MIX_W]
    k = qkv_ref[pl.ds(r0, n), MIX_W:2 * MIX_W]
    v = qkv_ref[pl.ds(r0, n), 2 * MIX_W:].astype(F32)
    gates = gate_ref[pl.ds(r0, n), :]
    d0 = 4 if backward else 0
    beta = _expand_heads(gates[:, d0:d0 + 4])
    gc = _expand_heads(gates[:, 8 + d0:12 + d0])
    gc_row = grow_ref[pl.ds(c, 1), d0 * n:(d0 + 4) * n]

    row = lax.broadcasted_iota(jnp.int32, (n, MIX_W), 0)
    col = lax.broadcasted_iota(jnp.int32, (n, MIX_W), 1) % n
    incl = (row <= col) if backward else (row >= col)
    strict = (row < col) if backward else (row > col)
    bd_mask = (lax.broadcasted_iota(jnp.int32, (MIX_W, MIX_W), 0) // n
               == lax.broadcasted_iota(jnp.int32, (MIX_W, MIX_W), 1) // n)
    decay = jnp.exp(jnp.where(incl, gc - gc_row, NEG_INF))

    kf = k.astype(F32)
    kq = lax.dot_general(jnp.concatenate([q, k], axis=0), _stack_heads(k), (((1,), (1,)), ((), ())),
                         preferred_element_type=F32)
    intra = kq[0:n] * decay
    low = jnp.where(strict, kq[n:] * beta * decay, 0.0)

    x = jnp.where(row == col, 1.0, 0.0) - low
    p = _mm(low, _block_diag(low, bd_mask))
    steps = int(math.log2(n)) - 1
    for step in range(steps):
        last = step == steps - 1
        lhs = x if last else jnp.concatenate([x, p], axis=0)
        r = _mm(lhs, _block_diag(p, bd_mask))
        x = x + r[0:n]
        if not last:
            p = r[n:]

    kbeta = kf * beta
    u = _mm(x, _block_diag(v * beta, bd_mask))
    w = _mm(x, _block_diag(kbeta * jnp.exp(gc), bd_mask))
    edge = gc[0:1] if backward else gc[n - 1:n]
    q_head = q.astype(F32) * jnp.exp(gc)
    k_tail = kf * jnp.exp(edge - gc)

    state = s_ref[...]
    ws_qs = _mm(jnp.concatenate([w, q_head], axis=0), state)
    v_new = u - ws_qs[0:n]
    o_ref[pl.ds(r0, n), :] = ws_qs[n:] + _mm(intra, _block_diag(v_new, bd_mask))
    kv = lax.dot_general(k_tail.astype(BF16), v_new.astype(BF16), (((0,), (0,)), ((), ())),
                         preferred_element_type=F32)
    s_ref[...] = state * jnp.exp(edge) + jnp.where(bd_mask, kv, 0.0)


def _delta_kernel(qkv_f, gate_f, grow_f, qkv_b, gate_b, grow_b, of_ref, ob_ref, sf_ref, sb_ref):
    @pl.when(pl.program_id(1) == 0)
    def _():
        sf_ref[...] = jnp.zeros(sf_ref.shape, F32)
        sb_ref[...] = jnp.zeros(sb_ref.shape, F32)

    nchunk = qkv_f.shape[0] // B_CHUNK

    def body(i, carry):
        _delta_chunk(False, qkv_f, gate_f, grow_f, of_ref, sf_ref, i)
        _delta_chunk(True, qkv_b, gate_b, grow_b, ob_ref, sb_ref, nchunk - 1 - i)
        return carry

    lax.fori_loop(0, nchunk, body, 0)


def _delta(qkvn, gates, batch, seq, rows):
    t = batch * seq
    nblk = seq // rows
    nchunk = rows // B_CHUNK
    grow = gates[:, 8:16].reshape(t // B_CHUNK, B_CHUNK, 8).transpose(0, 2, 1).reshape(t // B_CHUNK, 8 * B_CHUNK)
    fwd = lambda b, j: (b * nblk + j, 0)
    bwd = lambda b, j: (b * nblk + nblk - 1 - j, 0)
    specs = lambda m: [pl.BlockSpec((rows, 3 * MIX_W), m), pl.BlockSpec((rows, 16), m),
                       pl.BlockSpec((nchunk, 8 * B_CHUNK), m)]
    return pl.pallas_call(
        _delta_kernel,
        grid=(batch, nblk),
        in_specs=specs(fwd) + specs(bwd),
        out_specs=[pl.BlockSpec((rows, MIX_W), fwd), pl.BlockSpec((rows, MIX_W), bwd)],
        out_shape=[jax.ShapeDtypeStruct((t, MIX_W), F32)] * 2,
        scratch_shapes=[pltpu.VMEM((MIX_W, MIX_W), F32)] * 2,
        compiler_params=_params("parallel", "arbitrary"),
    )(qkvn, gates, grow, qkvn, gates, grow)


def _merge_kernel(x_ref, gain_ref, oa0, oa1, oa2, la0, la1, la2, bf_ref, bb_ref, z_ref, bgain_ref,
                  oc_ref, od_ref, wg_ref, bg_ref, wbr_ref, wout_ref, out_ref):
    x = x_ref[...]
    h = _rms_rows(x, gain_ref[...]).astype(BF16)

    lses = [r[...] for r in (la0, la1, la2)]
    top = jnp.maximum(jnp.maximum(lses[0], lses[1]), lses[2])
    ws = [jnp.exp(l - top) for l in lses]
    o_a = sum(w * r[...].astype(F32) for w, r in zip(ws, (oa0, oa1, oa2))) / (ws[0] + ws[1] + ws[2])

    o_b = bf_ref[...] + bb_ref[...]
    z = z_ref[...].astype(F32)
    o_b = (o_b * lax.rsqrt(_head_sums(o_b * o_b) * (1.0 / HEAD_DIM) + RMS_EPS) * bgain_ref[...]
           * (z * jax.nn.sigmoid(z)))

    branches = (o_a, o_b, oc_ref[...], od_ref[...])
    y = jnp.zeros(x.shape, F32)
    for i, o in enumerate(branches):
        gate = jax.nn.sigmoid(jnp.dot(h, wg_ref[i], preferred_element_type=F32) + bg_ref[i])
        y = y + gate * jnp.dot(o.astype(BF16), wbr_ref[i], preferred_element_type=F32)
    out_ref[...] = x + jnp.dot(y.astype(BF16), wout_ref[...], preferred_element_type=F32)


def _merge(x2d, a_outs, b_outs, z, oc, od, lw, tm):
    t = x2d.shape[0]
    tok = lambda w: pl.BlockSpec((tm, w), lambda i: (i, 0))
    (o0, l0), (o1, l1), (o2, l2) = a_outs
    acts = (o0, o1, o2, l0, l1, l2, b_outs[0], b_outs[1], z)
    weights = (lw["wg"], lw["bg"], lw["wbr"], lw["wout"])
    return pl.pallas_call(
        _merge_kernel,
        grid=(t // tm,),
        in_specs=([tok(D_MODEL), _const_spec(lw["norm_mix"].shape)] + [tok(MIX_W)] * len(acts)
                  + [_const_spec(lw["b_gain"].shape), tok(MIX_W), tok(MIX_W)]
                  + [_const_spec(w.shape) for w in weights]),
        out_specs=tok(D_MODEL),
        out_shape=jax.ShapeDtypeStruct((t, D_MODEL), F32),
        compiler_params=_params("parallel"),
    )(x2d, lw["norm_mix"], *acts, lw["b_gain"], oc, od, *weights)


def _moe_kernel(x_ref, gain_ref, wr_ref, br_ref, wgt_ref, wup_ref, wdn_ref, fgain_ref, out_ref, *, final):
    x = x_ref[...]
    h = _rms_rows(x, gain_ref[...])
    logits = jnp.dot(h, wr_ref[...], preferred_element_type=F32, precision=lax.Precision.HIGHEST) + br_ref[...]
    lane = lax.broadcasted_iota(jnp.int32, (1, LANES), 1)
    none = LANES

    def first_max(vals):
        top = jnp.max(vals, axis=-1, keepdims=True)
        return top, jnp.min(jnp.where(vals == top, lane, none), axis=-1, keepdims=True)

    gl = jnp.where(lane < N_GROUPS, logits, NEG_INF)
    g_top, g_idx = first_max(gl)
    g_prob = 1.0 / jnp.sum(jnp.exp(gl - g_top), axis=-1, keepdims=True)
    in_group = (lane >= N_GROUPS) & (lane < N_GROUPS + N_EXPERTS) & ((lane - N_GROUPS) // EXPERTS_PER_GROUP == g_idx)
    el = jnp.where(in_group, logits, NEG_INF)
    top1, idx1 = first_max(el)
    top2, idx2 = first_max(jnp.where(lane == idx1, NEG_INF, el))
    e2 = jnp.exp(top2 - top1)
    w1 = g_prob / (1.0 + e2)
    weights = jnp.where(lane == idx1, w1, 0.0) + jnp.where(lane == idx2, w1 * e2, 0.0)

    hb = h.astype(BF16)

    def expert(e, acc):
        w_e = jnp.sum(jnp.where(lane == e + N_GROUPS, weights, 0.0), axis=-1, keepdims=True)
        gate = jnp.dot(hb, wgt_ref[e], preferred_element_type=F32)
        up = jnp.dot(hb, wup_ref[e], preferred_element_type=F32)
        hid = gate * jax.nn.sigmoid(gate) * up * w_e
        return acc + jnp.dot(hid.astype(BF16), wdn_ref[e], preferred_element_type=F32)

    y = x + lax.fori_loop(0, N_EXPERTS, expert, jnp.zeros(x.shape, F32))
    if final:
        y = _rms_rows(y, fgain_ref[...])
    out_ref[...] = y


def _moe(x2d, lw, final_gain, final, tm):
    t = x2d.shape[0]
    tok = pl.BlockSpec((tm, D_MODEL), lambda i: (i, 0))
    once = lambda a: pl.BlockSpec(a.shape, lambda i: (0,) * a.ndim, pipeline_mode=pl.Buffered(1))
    weights = (lw["norm_ffn"], lw["w_route"], lw["b_route"], lw["w_ff_gate"], lw["w_ff_up"], lw["w_ff_down"],
               final_gain)
    return pl.pallas_call(
        functools.partial(_moe_kernel, final=final),
        grid=(t // tm,),
        in_specs=[tok] + [once(w) for w in weights],
        out_specs=tok,
        out_shape=jax.ShapeDtypeStruct((t, D_MODEL), F32),
        compiler_params=_params("parallel"),
    )(x2d, *weights)


def _layer_weights(p, l):
    w_in = p["w_in"][l]
    offs = np.cumsum([0, 3 * MIX_W, 3 * MIX_W, MIX_W, 16, MIX_W, 2 * C_KV_HEADS * HEAD_DIM, 3 * MIX_W])
    sec = [w_in[:, offs[i]:offs[i + 1]] for i in range(7)]
    c_order = np.concatenate([np.arange(h * HEAD_DIM, (h + 1) * HEAD_DIM) for h in (0, 2, 1, 3)])
    row = lambda v: v.reshape(1, -1).astype(F32)
    gate16 = lambda v: jnp.concatenate([jnp.zeros((8,), F32), v.reshape(-1).astype(F32)]).reshape(1, 16)
    wbr = p["w_branch"][l]
    wbr = jnp.stack([wbr[0], wbr[1], wbr[2][c_order], wbr[3]])
    w_route = jnp.concatenate([p["w_route_group"][l], p["w_route_expert"][l].reshape(D_MODEL, N_EXPERTS)], axis=1)
    b_route = jnp.concatenate([p["b_route_group"][l], p["b_route_expert"][l].reshape(N_EXPERTS)])
    pad_lanes = lambda a: jnp.pad(a, [(0, 0)] * (a.ndim - 1) + [(0, LANES - a.shape[-1])])
    return dict(
        norm_mix=row(p["norm_mix"][l]),
        wa=sec[0].astype(BF16), wb=sec[1].astype(BF16), wz=sec[2].astype(BF16),
        wab=pad_lanes(sec[3]).astype(BF16), wcq=sec[4][:, c_order].astype(BF16), wckv=sec[5].astype(BF16),
        wd=sec[6].astype(BF16),
        cq_gain=row(jnp.tile(p["c_q_norm"][l], N_HEADS)), ck_gain=row(jnp.tile(p["c_k_norm"][l], C_KV_HEADS)),
        conv_w=p["conv_w"][l].astype(F32), a_log16=gate16(p["a_log"][l]), dt_bias16=gate16(p["dt_bias"][l]),
        b_gain=row(jnp.tile(p["b_norm"][l], N_HEADS)),
        wg=p["w_gate"][l].astype(BF16), bg=p["b_gate"][l].reshape(N_HEADS, 1, D_MODEL).astype(F32),
        wbr=wbr.astype(BF16), wout=p["w_out"][l].astype(BF16),
        norm_ffn=row(p["norm_ffn"][l]),
        w_route=pad_lanes(w_route).astype(F32), b_route=pad_lanes(b_route.reshape(1, -1)).astype(F32),
        w_ff_gate=p["w_ff_gate"][l].reshape(N_EXPERTS, D_MODEL, D_EXPERT).astype(BF16),
        w_ff_up=p["w_ff_up"][l].reshape(N_EXPERTS, D_MODEL, D_EXPERT).astype(BF16),
        w_ff_down=p["w_ff_down"][l].reshape(N_EXPERTS, D_EXPERT, D_MODEL).astype(BF16),
        d_rpb=p["d_rpb"][l],
    )


def _tiles(seq):
    return dict(tm=min(512, seq), tq_c=min(256, seq), tk_c=min(512, seq), rows_d=min(8, seq // GRID_W),
                tq_a=128, rows_prep=min(512, seq), rows_delta=min(2048, seq))


def _encoder(x, layers, final_gain):
    batch, seq, _ = x.shape
    ts = _tiles(seq)
    tables = _rotary_tables(seq)
    depth = len(layers)
    x2d = x.reshape(batch * seq, D_MODEL)
    for l, lw in enumerate(layers):
        a, bqkv, z, ab, cq, ckv, dqkv = _in_proj(x2d, seq, lw, tables, ts["tm"])
        a_outs = [_band(a, batch, seq, window, dil, ts["tq_a"]) for window, dil in A_PATTERNS]
        qkvn, gates = _delta_prep(bqkv, ab, lw, batch, seq, ts["rows_prep"])
        b_outs = _delta(qkvn, gates, batch, seq, ts["rows_delta"])
        oc = _gqa(cq, ckv, batch, seq, ts["tq_c"], ts["tk_c"])
        od = _nbr(dqkv, _nbr_bias(lw["d_rpb"], min(NA_ROWS, seq // GRID_W)), batch, seq, ts["rows_d"])
        x2d = _merge(x2d, a_outs, b_outs, z, oc, od, lw, ts["tm"])
        x2d = _moe(x2d, lw, final_gain, l == depth - 1, ts["tm"])
    return x2d.reshape(batch, seq, D_MODEL)


def kernel(x_prompt, x_sample, norm_mix, w_in, conv_w, a_log, dt_bias, b_norm, c_q_norm, c_k_norm, d_rpb,
           w_gate, b_gate, w_branch, w_out, norm_ffn, w_route_group, b_route_group, w_route_expert,
           b_route_expert, w_ff_gate, w_ff_up, w_ff_down, norm_final):
    p = dict(norm_mix=norm_mix, w_in=w_in, conv_w=conv_w, a_log=a_log, dt_bias=dt_bias, b_norm=b_norm,
             c_q_norm=c_q_norm, c_k_norm=c_k_norm, d_rpb=d_rpb, w_gate=w_gate, b_gate=b_gate,
             w_branch=w_branch, w_out=w_out, norm_ffn=norm_ffn, w_route_group=w_route_group,
             b_route_group=b_route_group, w_route_expert=w_route_expert, b_route_expert=b_route_expert,
             w_ff_gate=w_ff_gate, w_ff_up=w_ff_up, w_ff_down=w_ff_down)
    layers = [_layer_weights(p, l) for l in range(norm_mix.shape[0])]
    final_gain = norm_final.reshape(1, -1).astype(F32)
    return _encoder(x_prompt, layers, final_gain), _encoder(x_sample, layers, final_gain)
```

```python
import functools
import math

import jax
import jax.numpy as jnp
import numpy as np
from jax import lax
from jax.experimental import pallas as pl
from jax.experimental.pallas import tpu as pltpu

D_MODEL = 1024
HEAD_DIM = 64
N_HEADS = 4
MIX_W = N_HEADS * HEAD_DIM
GRID_W = 64
RMS_EPS = 1e-6
NEG_INF = -1e30
LOG2_E = math.log2(math.e)

A_PATTERNS = ((128, 1), (512, 4), (2048, 16))
ROPE_THETA = 500000.0
ROPE_DIMS = HEAD_DIM // 4
AXIAL_THETA = 10000.0
C_KV_HEADS = 2
B_CONV = 5
B_CHUNK = 64
NA_ROWS = 8
NA_COLS = 16
N_GROUPS = 4
EXPERTS_PER_GROUP = 4
N_EXPERTS = N_GROUPS * EXPERTS_PER_GROUP
D_EXPERT = 256

LANES = 128
BF16_SUBLANES = 16
VMEM_LIMIT = 56 * 1024 * 1024

F32 = jnp.float32
BF16 = jnp.bfloat16


def _params(*sem):
    return pltpu.CompilerParams(dimension_semantics=sem, vmem_limit_bytes=VMEM_LIMIT)


def _const_spec(shape):
    n = len(shape)
    return pl.BlockSpec(shape, lambda *_: (0,) * n)


def _rms_rows(x, gain):
    return x * lax.rsqrt(jnp.mean(x * x, axis=-1, keepdims=True) + RMS_EPS) * gain


def _head_ids(width):
    return lax.broadcasted_iota(jnp.int32, (1, width), 1) // HEAD_DIM


def _block_ones(width):
    r = lax.broadcasted_iota(jnp.int32, (width, width), 0) // HEAD_DIM
    c = lax.broadcasted_iota(jnp.int32, (width, width), 1) // HEAD_DIM
    return jnp.where(r == c, 1.0, 0.0).astype(BF16)


def _head_sums(x):
    return jnp.dot(x.astype(BF16), _block_ones(x.shape[-1]), preferred_element_type=F32)


def _rotate(x, cos, sin_lo, sin_hi, shift):
    w = x.shape[-1]
    return x * cos + pltpu.roll(x, shift, 1) * sin_hi + pltpu.roll(x, w - shift, 1) * sin_lo


def _in_proj_kernel(x_ref, gain_ref, wa_ref, wb_ref, wz_ref, wab_ref, wcq_ref, wck_ref, wcvt_ref, wd_ref,
                    cqg_ref, ckg_ref, rot_a_ref, rot_c_ref,
                    a_ref, b_ref, z_ref, ab_ref, cq_ref, ck_ref, cvt_ref, d_ref):
    h = _rms_rows(x_ref[...], gain_ref[...]).astype(BF16)
    dot = lambda w_ref: jnp.dot(h, w_ref[...], preferred_element_type=F32)
    scale = HEAD_DIM ** -0.5

    ua = dot(wa_ref)
    cos, s_lo, s_hi = rot_a_ref[0], rot_a_ref[1], rot_a_ref[2]
    half = ROPE_DIMS // 2
    a_ref[:, 0:MIX_W] = (_rotate(ua[:, 0:MIX_W], cos, s_lo, s_hi, half) * scale).astype(BF16)
    a_ref[:, MIX_W:2 * MIX_W] = _rotate(ua[:, MIX_W:2 * MIX_W], cos, s_lo, s_hi, half).astype(BF16)
    a_ref[:, 2 * MIX_W:] = ua[:, 2 * MIX_W:].astype(BF16)

    b_ref[...] = dot(wb_ref).astype(BF16)
    z_ref[...] = dot(wz_ref).astype(BF16)
    ab_ref[...] = dot(wab_ref)[:, :ab_ref.shape[-1]]

    cos, s_lo, s_hi = rot_c_ref[0], rot_c_ref[1], rot_c_ref[2]
    quarter = HEAD_DIM // 4
    uq = dot(wcq_ref)
    uq = uq * lax.rsqrt(_head_sums(uq * uq) * (1.0 / HEAD_DIM) + RMS_EPS) * cqg_ref[...]
    cq_ref[...] = (_rotate(uq, cos, s_lo, s_hi, quarter) * (scale * LOG2_E)).astype(BF16)
    kw = C_KV_HEADS * HEAD_DIM
    uk = dot(wck_ref)
    uk = uk * lax.rsqrt(_head_sums(uk * uk) * (1.0 / HEAD_DIM) + RMS_EPS) * ckg_ref[...]
    ck_ref[...] = _rotate(uk, cos[:, :kw], s_lo[:, :kw], s_hi[:, :kw], quarter).astype(BF16)
    cvt_ref[...] = lax.dot_general(wcvt_ref[...], h, (((1,), (1,)), ((), ())),
                                   preferred_element_type=F32).astype(BF16)

    ud = dot(wd_ref)
    d_ref[:, 0:MIX_W] = (ud[:, 0:MIX_W] * scale).astype(BF16)
    d_ref[:, MIX_W:] = ud[:, MIX_W:].astype(BF16)


def _rotary_tables(seq):
    pos = np.arange(seq, dtype=np.float64)

    def table(pos_of_lane, theta, half, lane_in_group, active):
        freq_idx = lane_in_group % half
        inv = theta ** (-(freq_idx.astype(np.float64)) / half)
        ang = pos_of_lane * inv[None, :]
        cos = np.where(active[None, :], np.cos(ang), 1.0)
        sin = np.where(active[None, :], np.sin(ang), 0.0)
        low = (lane_in_group < half)[None, :]
        return np.stack([cos, np.where(low, -sin, 0.0), np.where(low, 0.0, sin)]).astype(np.float32)

    lane = np.arange(MIX_W) % HEAD_DIM
    rot_a = table(np.broadcast_to(pos[:, None], (seq, MIX_W)), ROPE_THETA, ROPE_DIMS // 2, lane % ROPE_DIMS,
                  lane < ROPE_DIMS)
    rows, cols = pos // GRID_W, pos % GRID_W
    half = HEAD_DIM // 2
    pos_c = np.where((lane < half)[None, :], rows[:, None], cols[:, None])
    rot_c = table(pos_c, AXIAL_THETA, half // 2, lane % half, np.ones(MIX_W, bool))
    return jnp.asarray(rot_a), jnp.asarray(rot_c)


def _in_proj(x2d, seq, lw, tables, tm):
    t = x2d.shape[0]
    rot_a, rot_c = tables
    nblk = seq // tm
    tok = lambda w: pl.BlockSpec((tm, w), lambda i: (i, 0))
    rot = pl.BlockSpec((3, tm, MIX_W), lambda i: (0, i % nblk, 0))
    weights = (lw["norm_mix"], lw["wa"], lw["wb"], lw["wz"], lw["wab"], lw["wcq"], lw["wck"], lw["wcvt"], lw["wd"],
               lw["cq_gain"], lw["ck_gain"])
    kw = C_KV_HEADS * HEAD_DIM
    outs = ((3 * MIX_W, BF16), (3 * MIX_W, BF16), (MIX_W, BF16), (16, F32), (MIX_W, BF16), (kw, BF16),
            None, (3 * MIX_W, BF16))
    out_specs = [pl.BlockSpec((kw, tm), lambda i: (0, i)) if o is None else tok(o[0]) for o in outs]
    out_shape = [jax.ShapeDtypeStruct((kw, t), BF16) if o is None else jax.ShapeDtypeStruct((t, o[0]), o[1])
                 for o in outs]
    return pl.pallas_call(
        _in_proj_kernel,
        grid=(t // tm,),
        in_specs=[tok(D_MODEL)] + [_const_spec(w.shape) for w in weights] + [rot, rot],
        out_specs=out_specs,
        out_shape=out_shape,
        compiler_params=_params("parallel"),
    )(x2d, *weights, rot_a, rot_c)


def _gqa_kernel(q_ref, k_ref, vt_ref, o_ref, acc_ref, *, tk):
    tq = q_ref.shape[0]
    seq = k_ref.shape[0]
    kw = C_KV_HEADS * HEAD_DIM
    qt = q_ref[...].astype(F32).T
    row_g = lax.broadcasted_iota(jnp.int32, (kw, 1), 0) // HEAD_DIM
    qs = jnp.concatenate([jnp.where(row_g == g, qt[blk * kw:(blk + 1) * kw], 0.0)
                          for g in range(C_KV_HEADS) for blk in range(2)], axis=1).astype(BF16)
    acc_ref[...] = jnp.zeros(acc_ref.shape, F32)

    def body(j, carry):
        m_old, l_old = carry
        start = pl.multiple_of(j * tk, tk)
        s = jnp.dot(k_ref[pl.ds(start, tk), :], qs, preferred_element_type=F32)
        m_new = jnp.maximum(m_old, jnp.max(s, axis=0, keepdims=True))
        alpha = jnp.exp2(m_old - m_new)
        p = jnp.exp2(s - m_new)
        acc_ref[...] = acc_ref[...] * alpha + jnp.dot(vt_ref[:, pl.ds(start, tk)], p.astype(BF16),
                                                      preferred_element_type=F32)
        return m_new, l_old * alpha + jnp.sum(p, axis=0, keepdims=True)

    init = (jnp.full((1, 4 * tq), NEG_INF, F32), jnp.zeros((1, 4 * tq), F32))
    _, l = lax.fori_loop(0, seq // tk, body, init)
    o = acc_ref[...] / l
    for blk in range(2):
        lo = o[:, blk * tq:(blk + 1) * tq]
        hi = o[:, (2 + blk) * tq:(3 + blk) * tq]
        o_ref[:, blk * kw:(blk + 1) * kw] = jnp.where(row_g == 0, lo, hi).T.astype(o_ref.dtype)


def _gqa(cq, ck, cvt, batch, seq, tq, tk):
    nq = seq // tq
    kw = C_KV_HEADS * HEAD_DIM
    return pl.pallas_call(
        functools.partial(_gqa_kernel, tk=tk),
        grid=(batch, nq),
        in_specs=[pl.BlockSpec((tq, MIX_W), lambda b, i: (b * nq + i, 0)),
                  pl.BlockSpec((seq, kw), lambda b, i: (b, 0)),
                  pl.BlockSpec((kw, seq), lambda b, i: (0, b))],
        out_specs=pl.BlockSpec((tq, MIX_W), lambda b, i: (b * nq + i, 0)),
        out_shape=jax.ShapeDtypeStruct((batch * seq, MIX_W), BF16),
        scratch_shapes=[pltpu.VMEM((kw, 4 * tq), F32)],
        compiler_params=_params("parallel", "arbitrary"),
    )(cq, ck, cvt)


def _stack_heads(q):
    ids = _head_ids(MIX_W)
    zero = jnp.zeros((), q.dtype)
    return jnp.concatenate([jnp.where(ids == h, q, zero) for h in range(N_HEADS)], axis=0)


def _unstack_heads(o, n):
    ids = _head_ids(MIX_W)
    out = o[0:n]
    for h in range(1, N_HEADS):
        out = jnp.where(ids == h, o[h * n:(h + 1) * n], out)
    return out


def _nbr_kernel(q_ref, k_ref, v_ref, bias_ref, o_ref, *, rows_per_step, grid_rows, kr):
    j = pl.program_id(1)
    win = kr * GRID_W

    def body(i, carry):
        r = j * rows_per_step + i
        rs = jnp.clip(r - kr // 2, 0, grid_rows - kr)
        kstart = pl.multiple_of(rs * GRID_W, GRID_W)
        qstart = pl.multiple_of(i * GRID_W, GRID_W)
        qs = _stack_heads(q_ref[pl.ds(qstart, GRID_W), :])
        s = lax.dot_general(qs, k_ref[pl.ds(kstart, win), :], (((1,), (1,)), ((), ())),
                            preferred_element_type=F32)
        s = s + bias_ref[r - rs]
        m = jnp.max(s, axis=-1, keepdims=True)
        p = jnp.exp(s - m)
        l = jnp.sum(p, axis=-1, keepdims=True)
        o = jnp.dot(p.astype(BF16), v_ref[pl.ds(kstart, win), :], preferred_element_type=F32) / l
        o_ref[pl.ds(qstart, GRID_W), :] = _unstack_heads(o, GRID_W).astype(o_ref.dtype)
        return carry

    lax.fori_loop(0, rows_per_step, body, 0)


def _nbr_bias(rpb, kr):
    c = np.arange(GRID_W)
    cs = np.clip(c - NA_COLS // 2, 0, GRID_W - NA_COLS)
    col_valid = (c[None, :] >= cs[:, None]) & (c[None, :] < cs[:, None] + NA_COLS)
    dc = np.clip(c[None, :] - c[:, None], -(NA_COLS - 1), NA_COLS - 1) + NA_COLS - 1
    dr = np.arange(kr)[None, :] - np.arange(kr)[:, None] + NA_ROWS - 1
    pick_r = (dr[..., None] == np.arange(2 * NA_ROWS - 1)).astype(np.float32)
    pick_c = (dc[..., None] == np.arange(2 * NA_COLS - 1)).astype(np.float32)
    bias = jnp.einsum('hab,vja,qkb->vhqjk', rpb.astype(F32), pick_r, pick_c, precision=lax.Precision.HIGHEST)
    bias = jnp.where(col_valid[None, None, :, None, :], bias, NEG_INF)
    return bias.reshape(kr, N_HEADS * GRID_W, kr * GRID_W)


def _nbr(dqkv, bias, batch, seq, rows_per_step):
    grid_rows = seq // GRID_W
    kr = min(NA_ROWS, grid_rows)
    nblk = grid_rows // rows_per_step
    tq = rows_per_step * GRID_W
    return pl.pallas_call(
        functools.partial(_nbr_kernel, rows_per_step=rows_per_step, grid_rows=grid_rows, kr=kr),
        grid=(batch, nblk),
        in_specs=[pl.BlockSpec((tq, MIX_W), lambda b, i: (b * nblk + i, 0)),
                  pl.BlockSpec((seq, MIX_W), lambda b, i: (b, 1)),
                  pl.BlockSpec((seq, MIX_W), lambda b, i: (b, 2)),
                  _const_spec(bias.shape)],
        out_specs=pl.BlockSpec((tq, MIX_W), lambda b, i: (b * nblk + i, 0)),
        out_shape=jax.ShapeDtypeStruct((batch * seq, MIX_W), BF16),
        compiler_params=_params("parallel", "arbitrary"),
    )(dqkv, dqkv, dqkv, bias)


def _band_kernel(q_ref, k_ref, v_ref, o_ref, lse_ref, *, half, kwin):
    tq = q_ref.shape[0]
    length = k_ref.shape[0]
    i0 = pl.program_id(2) * tq
    kstart = pl.multiple_of(jnp.clip(i0 - half, 0, length - kwin), BF16_SUBLANES)
    qs = _stack_heads(q_ref[...])
    s = lax.dot_general(qs, k_ref[pl.ds(kstart, kwin), :], (((1,), (1,)), ((), ())),
                        preferred_element_type=F32)
    qpos = i0 + lax.broadcasted_iota(jnp.int32, (N_HEADS * tq, kwin), 0) % tq
    kpos = kstart + lax.broadcasted_iota(jnp.int32, (N_HEADS * tq, kwin), 1)
    s = jnp.where(jnp.abs(qpos - kpos) <= half, s, NEG_INF)
    m = jnp.max(s, axis=-1, keepdims=True)
    p = jnp.exp(s - m)
    l = jnp.sum(p, axis=-1, keepdims=True)
    o = jnp.dot(p.astype(BF16), v_ref[pl.ds(kstart, kwin), :], preferred_element_type=F32) / l
    o_ref[...] = _unstack_heads(o, tq).astype(o_ref.dtype)
    lse = jnp.broadcast_to(m + jnp.log(l), (N_HEADS * tq, MIX_W))
    lse_ref[...] = _unstack_heads(lse, tq)


def _band(a, batch, seq, window, dil, tq):
    half = window // (2 * dil)
    length = seq // dil
    tq = min(tq, length)
    kwin = min(tq + 2 * half, length)
    nq = length // tq
    view = a.reshape(batch, length, dil * 3 * MIX_W)
    qkv = lambda part: (lambda b, r, i: (b, 0, 3 * r + part))
    out = pl.BlockSpec((None, tq, MIX_W), lambda b, r, i: (b, i, r))
    o, lse = pl.pallas_call(
        functools.partial(_band_kernel, half=half, kwin=kwin),
        grid=(batch, dil, nq),
        in_specs=[pl.BlockSpec((None, tq, MIX_W), lambda b, r, i: (b, i, 3 * r)),
                  pl.BlockSpec((None, length, MIX_W), qkv(1)),
                  pl.BlockSpec((None, length, MIX_W), qkv(2))],
        out_specs=[out, out],
        out_shape=[jax.ShapeDtypeStruct((batch, length, dil * MIX_W), BF16),
                   jax.ShapeDtypeStruct((batch, length, dil * MIX_W), F32)],
        compiler_params=_params("parallel", "parallel", "arbitrary"),
    )(view, view, view)
    return o.reshape(batch * seq, MIX_W), lse.reshape(batch * seq, MIX_W)


CUM_ROWS = 256


def _chunk_tri(n, upper):
    r = lax.broadcasted_iota(jnp.int32, (n, n), 0)
    c = lax.broadcasted_iota(jnp.int32, (n, n), 1)
    same = (r // B_CHUNK) == (c // B_CHUNK)
    keep = (r <= c) if upper else (r >= c)
    return jnp.where(same & keep, 1.0, 0.0).astype(F32)


def _delta_prep_kernel(x_ref, prev_ref, next_ref, ab_ref, conv_ref, alog_ref, dtb_ref,
                       qkv_ref, gate_ref, xe_ref):
    j = pl.program_id(1)
    rows = x_ref.shape[0]
    pad = BF16_SUBLANES
    keep_prev = jnp.where(j > 0, 1.0, 0.0)
    keep_next = jnp.where(j < pl.num_programs(1) - 1, 1.0, 0.0)
    xe_ref[0:pad, :] = prev_ref[...].astype(F32) * keep_prev
    xe_ref[pad:pad + rows, :] = x_ref[...].astype(F32)
    xe_ref[pad + rows:, :] = next_ref[...].astype(F32) * keep_next
    centre = B_CONV // 2
    y = jnp.zeros((rows, 3 * MIX_W), F32)
    for tap in range(B_CONV):
        y = y + xe_ref[pad + tap - centre:pad + tap - centre + rows, :] * conv_ref[tap:tap + 1, :]
    y = y * jax.nn.sigmoid(y)
    q, k = y[:, 0:MIX_W], y[:, MIX_W:2 * MIX_W]
    q = q * lax.rsqrt(_head_sums(q * q) + 1e-6) * (HEAD_DIM ** -0.5)
    k = k * lax.rsqrt(_head_sums(k * k) + 1e-6)
    qkv_ref[:, 0:MIX_W] = q.astype(BF16)
    qkv_ref[:, MIX_W:2 * MIX_W] = k.astype(BF16)
    qkv_ref[:, 2 * MIX_W:] = y[:, 2 * MIX_W:].astype(BF16)

    ab = ab_ref[...]
    col = lax.broadcasted_iota(jnp.int32, (1, ab.shape[-1]), 1)
    xs = ab + dtb_ref[...]
    softplus = jnp.maximum(xs, 0.0) + jnp.log(1.0 + jnp.exp(-jnp.abs(xs)))
    g = -jnp.exp(alog_ref[...]) * softplus
    beta = jax.nn.sigmoid(ab)
    tri_f, tri_b = _chunk_tri(CUM_ROWS, False), _chunk_tri(CUM_ROWS, True)
    for r0 in range(0, rows, CUM_ROWS):
        gp = g[r0:r0 + CUM_ROWS]
        cum_f = jnp.dot(tri_f, gp, preferred_element_type=F32, precision=lax.Precision.HIGHEST)
        cum_b = jnp.dot(tri_b, gp, preferred_element_type=F32, precision=lax.Precision.HIGHEST)
        gate_ref[r0:r0 + CUM_ROWS, :] = jnp.where(col < 8, beta[r0:r0 + CUM_ROWS],
                                                   jnp.where(col < 12, cum_f, cum_b))


def _delta_prep(bqkv, ab, lw, batch, seq, rows):
    t = batch * seq
    nblk = seq // rows
    per = rows // BF16_SUBLANES
    last = t // BF16_SUBLANES - 1
    w3 = 3 * MIX_W
    return pl.pallas_call(
        _delta_prep_kernel,
        grid=(batch, nblk),
        in_specs=[pl.BlockSpec((rows, w3), lambda b, j: (b * nblk + j, 0)),
                  pl.BlockSpec((BF16_SUBLANES, w3), lambda b, j: (jnp.maximum((b * nblk + j) * per - 1, 0), 0)),
                  pl.BlockSpec((BF16_SUBLANES, w3), lambda b, j: (jnp.minimum((b * nblk + j + 1) * per, last), 0)),
                  pl.BlockSpec((rows, 16), lambda b, j: (b * nblk + j, 0)),
                  _const_spec(lw["conv_w"].shape), _const_spec((1, 16)), _const_spec((1, 16))],
        out_specs=[pl.BlockSpec((rows, w3), lambda b, j: (b * nblk + j, 0)),
                   pl.BlockSpec((rows, 16), lambda b, j: (b * nblk + j, 0))],
        out_shape=[jax.ShapeDtypeStruct((t, w3), BF16), jax.ShapeDtypeStruct((t, 16), F32)],
        scratch_shapes=[pltpu.VMEM((rows + 2 * BF16_SUBLANES, w3), F32)],
        compiler_params=_params("parallel", "arbitrary"),
    )(bqkv, bqkv, bqkv, ab, lw["conv_w"], lw["a_log16"], lw["dt_bias16"])


def _expand_heads(cols):
    ids = _head_ids(MIX_W)
    out = jnp.broadcast_to(cols[:, 0:1], (cols.shape[0], MIX_W))
    for h in range(1, N_HEADS):
        out = jnp.where(ids == h, jnp.broadcast_to(cols[:, h:h + 1], out.shape), out)
    return out


def _block_diag(packed, mask):
    return jnp.where(mask, jnp.concatenate([packed] * N_HEADS, axis=0), 0.0).astype(BF16)


def _mm(a, b):
    return jnp.dot(a.astype(BF16), b.astype(BF16), preferred_element_type=F32)


def _delta_local(backward, qkv_ref, gate_ref, grow_ref, u_ref, w_ref, qh_ref, kt_ref, in_ref, eg_ref, c):
    n = B_CHUNK
    d = 1 if backward else 0
    r0 = pl.multiple_of(c * n, n)
    q = qkv_ref[pl.ds(r0, n), 0:MIX_W]
    k = qkv_ref[pl.ds(r0, n), MIX_W:2 * MIX_W]
    v = qkv_ref[pl.ds(r0, n), 2 * MIX_W:].astype(F32)
    gates = gate_ref[pl.ds(r0, n), :]
    d0 = 4 if backward else 0
    beta = _expand_heads(gates[:, d0:d0 + 4])
    gc = _expand_heads(gates[:, 8 + d0:12 + d0])
    gc_row = grow_ref[pl.ds(c, 1), d0 * n:(d0 + 4) * n]

    row = lax.broadcasted_iota(jnp.int32, (n, MIX_W), 0)
    col = lax.broadcasted_iota(jnp.int32, (n, MIX_W), 1) % n
    incl = (row <= col) if backward else (row >= col)
    strict = (row < col) if backward else (row > col)
    bd_mask = (lax.broadcasted_iota(jnp.int32, (MIX_W, MIX_W), 0) // n
               == lax.broadcasted_iota(jnp.int32, (MIX_W, MIX_W), 1) // n)
    decay = jnp.exp(jnp.where(incl, gc - gc_row, NEG_INF))

    kf = k.astype(F32)
    kq = lax.dot_general(jnp.concatenate([q, k], axis=0), _stack_heads(k), (((1,), (1,)), ((), ())),
                         preferred_element_type=F32)
    yield
    intra = kq[0:n] * decay
    low = jnp.where(strict, kq[n:] * beta * decay, 0.0)

    x = jnp.where(row == col, 1.0, 0.0) - low
    p = _mm(low, _block_diag(low, bd_mask))
    yield
    steps = int(math.log2(n)) - 1
    for step in range(steps):
        last = step == steps - 1
        lhs = x if last else jnp.concatenate([x, p], axis=0)
        r = _mm(lhs, _block_diag(p, bd_mask))
        yield
        x = x + r[0:n]
        if not last:
            p = r[n:]

    kbeta = kf * beta
    u_ref[d, pl.ds(r0, n), :] = _mm(x, _block_diag(v * beta, bd_mask))
    w_ref[d, pl.ds(r0, n), :] = _mm(x, _block_diag(kbeta * jnp.exp(gc), bd_mask)).astype(BF16)
    edge = gc[0:1] if backward else gc[n - 1:n]
    qh_ref[d, pl.ds(r0, n), :] = (q.astype(F32) * jnp.exp(gc)).astype(BF16)
    kt_ref[d, pl.ds(r0, n), :] = (kf * jnp.exp(edge - gc)).astype(BF16)
    in_ref[d, pl.ds(r0, n), :] = intra.astype(BF16)
    eg_ref[d, pl.ds(c, 1), :] = jnp.exp(edge)


def _delta_step(backward, u_ref, w_ref, qh_ref, kt_ref, in_ref, eg_ref, o_ref, s_ref, c):
    n = B_CHUNK
    d = 1 if backward else 0
    r0 = pl.multiple_of(c * n, n)
    bd_mask = (lax.broadcasted_iota(jnp.int32, (MIX_W, MIX_W), 0) // n
               == lax.broadcasted_iota(jnp.int32, (MIX_W, MIX_W), 1) // n)
    state = s_ref[d]
    lhs = jnp.concatenate([w_ref[d, pl.ds(r0, n), :], qh_ref[d, pl.ds(r0, n), :]], axis=0)
    ws_qs = jnp.dot(lhs, state.astype(BF16), preferred_element_type=F32)
    yield
    v_new = u_ref[d, pl.ds(r0, n), :] - ws_qs[0:n]
    o_ref[pl.ds(r0, n), :] = ws_qs[n:] + jnp.dot(in_ref[d, pl.ds(r0, n), :], _block_diag(v_new, bd_mask),
                                                  preferred_element_type=F32)
    kv = lax.dot_general(kt_ref[d, pl.ds(r0, n), :], v_new.astype(BF16), (((0,), (0,)), ((), ())),
                         preferred_element_type=F32)
    s_ref[d] = state * eg_ref[d, pl.ds(c, 1), :] + jnp.where(bd_mask, kv, 0.0)


def _interleave(*chains):
    live = list(chains)
    while live:
        live = [c for c in live if next(c, live) is not live]


LOCAL_UNROLL = 4


def _delta_kernel(qkv_f, gate_f, grow_f, qkv_b, gate_b, grow_b, of_ref, ob_ref,
                  s_ref, u_ref, w_ref, qh_ref, kt_ref, in_ref, eg_ref):
    @pl.when(pl.program_id(1) == 0)
    def _():
        s_ref[...] = jnp.zeros(s_ref.shape, F32)

    nchunk = qkv_f.shape[0] // B_CHUNK
    staged = (u_ref, w_ref, qh_ref, kt_ref, in_ref, eg_ref)

    def local(i, carry):
        chunks = [i * LOCAL_UNROLL + g for g in range(LOCAL_UNROLL)]
        _interleave(*[_delta_local(False, qkv_f, gate_f, grow_f, *staged, c) for c in chunks],
                    *[_delta_local(True, qkv_b, gate_b, grow_b, *staged, c) for c in chunks])
        return carry

    lax.fori_loop(0, nchunk // LOCAL_UNROLL, local, 0)

    def step(i, carry):
        _interleave(_delta_step(False, *staged, of_ref, s_ref, i),
                    _delta_step(True, *staged, ob_ref, s_ref, nchunk - 1 - i))
        return carry

    lax.fori_loop(0, nchunk, step, 0)


def _delta(qkvn, gates, batch, seq, rows):
    t = batch * seq
    nblk = seq // rows
    nchunk = rows // B_CHUNK
    grow = gates[:, 8:16].reshape(t // B_CHUNK, B_CHUNK, 8).transpose(0, 2, 1).reshape(t // B_CHUNK, 8 * B_CHUNK)
    fwd = lambda b, j: (b * nblk + j, 0)
    bwd = lambda b, j: (b * nblk + nblk - 1 - j, 0)
    specs = lambda m: [pl.BlockSpec((rows, 3 * MIX_W), m), pl.BlockSpec((rows, 16), m),
                       pl.BlockSpec((nchunk, 8 * B_CHUNK), m)]
    return pl.pallas_call(
        _delta_kernel,
        grid=(batch, nblk),
        in_specs=specs(fwd) + specs(bwd),
        out_specs=[pl.BlockSpec((rows, MIX_W), fwd), pl.BlockSpec((rows, MIX_W), bwd)],
        out_shape=[jax.ShapeDtypeStruct((t, MIX_W), F32)] * 2,
        scratch_shapes=([pltpu.VMEM((2, MIX_W, MIX_W), F32), pltpu.VMEM((2, rows, MIX_W), F32)]
                        + [pltpu.VMEM((2, rows, MIX_W), BF16)] * 4 + [pltpu.VMEM((2, nchunk, MIX_W), F32)]),
        compiler_params=_params("parallel", "arbitrary"),
    )(qkvn, gates, grow, qkvn, gates, grow)


def _merge_kernel(x_ref, gain_ref, oa0, oa1, oa2, la0, la1, la2, bf_ref, bb_ref, z_ref, bgain_ref,
                  oc_ref, od_ref, wg_ref, bg_ref, wbr_ref, wout_ref, out_ref):
    x = x_ref[...]
    h = _rms_rows(x, gain_ref[...]).astype(BF16)

    lses = [r[...] for r in (la0, la1, la2)]
    top = jnp.maximum(jnp.maximum(lses[0], lses[1]), lses[2])
    ws = [jnp.exp(l - top) for l in lses]
    o_a = sum(w * r[...].astype(F32) for w, r in zip(ws, (oa0, oa1, oa2))) / (ws[0] + ws[1] + ws[2])

    o_b = bf_ref[...] + bb_ref[...]
    z = z_ref[...].astype(F32)
    o_b = (o_b * lax.rsqrt(_head_sums(o_b * o_b) * (1.0 / HEAD_DIM) + RMS_EPS) * bgain_ref[...]
           * (z * jax.nn.sigmoid(z)))

    branches = (o_a, o_b, oc_ref[...], od_ref[...])
    y = jnp.zeros(x.shape, F32)
    for i, o in enumerate(branches):
        gate = jax.nn.sigmoid(jnp.dot(h, wg_ref[i], preferred_element_type=F32) + bg_ref[i])
        y = y + gate * jnp.dot(o.astype(BF16), wbr_ref[i], preferred_element_type=F32)
    out_ref[...] = x + jnp.dot(y.astype(BF16), wout_ref[...], preferred_element_type=F32)


def _merge(x2d, a_outs, b_outs, z, oc, od, lw, tm):
    t = x2d.shape[0]
    tok = lambda w: pl.BlockSpec((tm, w), lambda i: (i, 0))
    (o0, l0), (o1, l1), (o2, l2) = a_outs
    acts = (o0, o1, o2, l0, l1, l2, b_outs[0], b_outs[1], z)
    weights = (lw["wg"], lw["bg"], lw["wbr"], lw["wout"])
    return pl.pallas_call(
        _merge_kernel,
        grid=(t // tm,),
        in_specs=([tok(D_MODEL), _const_spec(lw["norm_mix"].shape)] + [tok(MIX_W)] * len(acts)
                  + [_const_spec(lw["b_gain"].shape), tok(MIX_W), tok(MIX_W)]
                  + [_const_spec(w.shape) for w in weights]),
        out_specs=tok(D_MODEL),
        out_shape=jax.ShapeDtypeStruct((t, D_MODEL), F32),
        compiler_params=_params("parallel"),
    )(x2d, lw["norm_mix"], *acts, lw["b_gain"], oc, od, *weights)


def _moe_kernel(x_ref, gain_ref, wr_ref, br_ref, wgt_ref, wup_ref, wdn_ref, fgain_ref, out_ref, *, final):
    x = x_ref[...]
    h = _rms_rows(x, gain_ref[...])
    logits = jnp.dot(h, wr_ref[...], preferred_element_type=F32, precision=lax.Precision.HIGHEST) + br_ref[...]
    lane = lax.broadcasted_iota(jnp.int32, (1, LANES), 1)
    none = LANES

    def first_max(vals):
        top = jnp.max(vals, axis=-1, keepdims=True)
        return top, jnp.min(jnp.where(vals == top, lane, none), axis=-1, keepdims=True)

    gl = jnp.where(lane < N_GROUPS, logits, NEG_INF)
    g_top, g_idx = first_max(gl)
    g_prob = 1.0 / jnp.sum(jnp.exp(gl - g_top), axis=-1, keepdims=True)
    in_group = (lane >= N_GROUPS) & (lane < N_GROUPS + N_EXPERTS) & ((lane - N_GROUPS) // EXPERTS_PER_GROUP == g_idx)
    el = jnp.where(in_group, logits, NEG_INF)
    top1, idx1 = first_max(el)
    top2, idx2 = first_max(jnp.where(lane == idx1, NEG_INF, el))
    e2 = jnp.exp(top2 - top1)
    w1 = g_prob / (1.0 + e2)
    weights = jnp.where(lane == idx1, w1, 0.0) + jnp.where(lane == idx2, w1 * e2, 0.0)

    hb = h.astype(BF16)

    def expert(e, acc):
        w_e = jnp.sum(jnp.where(lane == e + N_GROUPS, weights, 0.0), axis=-1, keepdims=True)
        gate = jnp.dot(hb, wgt_ref[e], preferred_element_type=F32)
        up = jnp.dot(hb, wup_ref[e], preferred_element_type=F32)
        hid = gate * jax.nn.sigmoid(gate) * up * w_e
        return acc + jnp.dot(hid.astype(BF16), wdn_ref[e], preferred_element_type=F32)

    y = x + lax.fori_loop(0, N_EXPERTS, expert, jnp.zeros(x.shape, F32))
    if final:
        y = _rms_rows(y, fgain_ref[...])
    out_ref[...] = y


def _moe(x2d, lw, final_gain, final, tm):
    t = x2d.shape[0]
    tok = pl.BlockSpec((tm, D_MODEL), lambda i: (i, 0))
    once = lambda a: pl.BlockSpec(a.shape, lambda i: (0,) * a.ndim, pipeline_mode=pl.Buffered(1))
    weights = (lw["norm_ffn"], lw["w_route"], lw["b_route"], lw["w_ff_gate"], lw["w_ff_up"], lw["w_ff_down"],
               final_gain)
    return pl.pallas_call(
        functools.partial(_moe_kernel, final=final),
        grid=(t // tm,),
        in_specs=[tok] + [once(w) for w in weights],
        out_specs=tok,
        out_shape=jax.ShapeDtypeStruct((t, D_MODEL), F32),
        compiler_params=_params("parallel"),
    )(x2d, *weights)


def _layer_weights(p, l):
    w_in = p["w_in"][l]
    offs = np.cumsum([0, 3 * MIX_W, 3 * MIX_W, MIX_W, 16, MIX_W, 2 * C_KV_HEADS * HEAD_DIM, 3 * MIX_W])
    sec = [w_in[:, offs[i]:offs[i + 1]] for i in range(7)]
    c_order = np.concatenate([np.arange(h * HEAD_DIM, (h + 1) * HEAD_DIM) for h in (0, 2, 1, 3)])
    row = lambda v: v.reshape(1, -1).astype(F32)
    gate16 = lambda v: jnp.concatenate([jnp.zeros((8,), F32), v.reshape(-1).astype(F32)]).reshape(1, 16)
    wbr = p["w_branch"][l]
    wbr = jnp.stack([wbr[0], wbr[1], wbr[2][c_order], wbr[3]])
    w_route = jnp.concatenate([p["w_route_group"][l], p["w_route_expert"][l].reshape(D_MODEL, N_EXPERTS)], axis=1)
    b_route = jnp.concatenate([p["b_route_group"][l], p["b_route_expert"][l].reshape(N_EXPERTS)])
    pad_lanes = lambda a: jnp.pad(a, [(0, 0)] * (a.ndim - 1) + [(0, LANES - a.shape[-1])])
    return dict(
        norm_mix=row(p["norm_mix"][l]),
        wa=sec[0].astype(BF16), wb=sec[1].astype(BF16), wz=sec[2].astype(BF16),
        wab=pad_lanes(sec[3]).astype(BF16), wcq=sec[4][:, c_order].astype(BF16), wck=sec[5][:, :C_KV_HEADS * HEAD_DIM].astype(BF16), wcvt=sec[5][:, C_KV_HEADS * HEAD_DIM:].T.astype(BF16),
        wd=sec[6].astype(BF16),
        cq_gain=row(jnp.tile(p["c_q_norm"][l], N_HEADS)), ck_gain=row(jnp.tile(p["c_k_norm"][l], C_KV_HEADS)),
        conv_w=p["conv_w"][l].astype(F32), a_log16=gate16(p["a_log"][l]), dt_bias16=gate16(p["dt_bias"][l]),
        b_gain=row(jnp.tile(p["b_norm"][l], N_HEADS)),
        wg=p["w_gate"][l].astype(BF16), bg=p["b_gate"][l].reshape(N_HEADS, 1, D_MODEL).astype(F32),
        wbr=wbr.astype(BF16), wout=p["w_out"][l].astype(BF16),
        norm_ffn=row(p["norm_ffn"][l]),
        w_route=pad_lanes(w_route).astype(F32), b_route=pad_lanes(b_route.reshape(1, -1)).astype(F32),
        w_ff_gate=p["w_ff_gate"][l].reshape(N_EXPERTS, D_MODEL, D_EXPERT).astype(BF16),
        w_ff_up=p["w_ff_up"][l].reshape(N_EXPERTS, D_MODEL, D_EXPERT).astype(BF16),
        w_ff_down=p["w_ff_down"][l].reshape(N_EXPERTS, D_EXPERT, D_MODEL).astype(BF16),
        d_rpb=p["d_rpb"][l],
    )


def _tiles(seq):
    return dict(tm=min(512, seq), tq_c=min(256, seq), tk_c=min(512, seq), rows_d=min(8, seq // GRID_W),
                tq_a=128, rows_prep=min(512, seq), rows_delta=min(2048, seq))


def _encoder(x, layers, final_gain):
    batch, seq, _ = x.shape
    ts = _tiles(seq)
    tables = _rotary_tables(seq)
    depth = len(layers)
    x2d = x.reshape(batch * seq, D_MODEL)
    for l, lw in enumerate(layers):
        a, bqkv, z, ab, cq, ck, cvt, dqkv = _in_proj(x2d, seq, lw, tables, ts["tm"])
        a_outs = [_band(a, batch, seq, window, dil, ts["tq_a"]) for window, dil in A_PATTERNS]
        qkvn, gates = _delta_prep(bqkv, ab, lw, batch, seq, ts["rows_prep"])
        b_outs = _delta(qkvn, gates, batch, seq, ts["rows_delta"])
        oc = _gqa(cq, ck, cvt, batch, seq, ts["tq_c"], ts["tk_c"])
        od = _nbr(dqkv, _nbr_bias(lw["d_rpb"], min(NA_ROWS, seq // GRID_W)), batch, seq, ts["rows_d"])
        x2d = _merge(x2d, a_outs, b_outs, z, oc, od, lw, ts["tm"])
        x2d = _moe(x2d, lw, final_gain, l == depth - 1, ts["tm"])
    return x2d.reshape(batch, seq, D_MODEL)


def kernel(x_prompt, x_sample, norm_mix, w_in, conv_w, a_log, dt_bias, b_norm, c_q_norm, c_k_norm, d_rpb,
           w_gate, b_gate, w_branch, w_out, norm_ffn, w_route_group, b_route_group, w_route_expert,
           b_route_expert, w_ff_gate, w_ff_up, w_ff_down, norm_final):
    p = dict(norm_mix=norm_mix, w_in=w_in, conv_w=conv_w, a_log=a_log, dt_bias=dt_bias, b_norm=b_norm,
             c_q_norm=c_q_norm, c_k_norm=c_k_norm, d_rpb=d_rpb, w_gate=w_gate, b_gate=b_gate,
             w_branch=w_branch, w_out=w_out, norm_ffn=norm_ffn, w_route_group=w_route_group,
             b_route_group=b_route_group, w_route_expert=w_route_expert, b_route_expert=b_route_expert,
             w_ff_gate=w_ff_gate, w_ff_up=w_ff_up, w_ff_down=w_ff_down)
    layers = [_layer_weights(p, l) for l in range(norm_mix.shape[0])]
    final_gain = norm_final.reshape(1, -1).astype(F32)
    return _encoder(x_prompt, layers, final_gain), _encoder(x_sample, layers, final_gain)
```

```python
import functools
import math

import jax
import jax.numpy as jnp
import numpy as np
from jax import lax
from jax.experimental import pallas as pl
from jax.experimental.pallas import tpu as pltpu

D_MODEL = 1024
HEAD_DIM = 64
N_HEADS = 4
MIX_W = N_HEADS * HEAD_DIM
GRID_W = 64
RMS_EPS = 1e-6
NEG_INF = -1e30
LOG2_E = math.log2(math.e)

A_PATTERNS = ((128, 1), (512, 4), (2048, 16))
ROPE_THETA = 500000.0
ROPE_DIMS = HEAD_DIM // 4
AXIAL_THETA = 10000.0
C_KV_HEADS = 2
B_CONV = 5
B_CHUNK = 64
NA_ROWS = 8
NA_COLS = 16
N_GROUPS = 4
EXPERTS_PER_GROUP = 4
N_EXPERTS = N_GROUPS * EXPERTS_PER_GROUP
D_EXPERT = 256

LANES = 128
BF16_SUBLANES = 16
VMEM_LIMIT = 56 * 1024 * 1024

F32 = jnp.float32
BF16 = jnp.bfloat16


def _params(*sem):
    return pltpu.CompilerParams(dimension_semantics=sem, vmem_limit_bytes=VMEM_LIMIT)


def _const_spec(shape):
    n = len(shape)
    return pl.BlockSpec(shape, lambda *_: (0,) * n)


def _rms_rows(x, gain):
    return x * lax.rsqrt(jnp.mean(x * x, axis=-1, keepdims=True) + RMS_EPS) * gain


def _head_ids(width):
    return lax.broadcasted_iota(jnp.int32, (1, width), 1) // HEAD_DIM


def _block_ones(width):
    r = lax.broadcasted_iota(jnp.int32, (width, width), 0) // HEAD_DIM
    c = lax.broadcasted_iota(jnp.int32, (width, width), 1) // HEAD_DIM
    return jnp.where(r == c, 1.0, 0.0).astype(BF16)


def _head_sums(x):
    return jnp.dot(x.astype(BF16), _block_ones(x.shape[-1]), preferred_element_type=F32)


def _interleave(*chains):
    live = list(chains)
    while live:
        live = [c for c in live if next(c, live) is not live]


def _rotate(x, cos, sin_lo, sin_hi, shift):
    w = x.shape[-1]
    return x * cos + pltpu.roll(x, shift, 1) * sin_hi + pltpu.roll(x, w - shift, 1) * sin_lo


def _in_proj_kernel(x_ref, gain_ref, wa_ref, wb_ref, wz_ref, wab_ref, wcq_ref, wck_ref, wcvt_ref, wd_ref,
                    cqg_ref, ckg_ref, rot_a_ref, rot_c_ref,
                    a_ref, b_ref, z_ref, ab_ref, cq_ref, ck_ref, cvt_ref, d_ref):
    h = _rms_rows(x_ref[...], gain_ref[...]).astype(BF16)
    dot = lambda w_ref: jnp.dot(h, w_ref[...], preferred_element_type=F32)
    scale = HEAD_DIM ** -0.5

    ua = dot(wa_ref)
    cos, s_lo, s_hi = rot_a_ref[0], rot_a_ref[1], rot_a_ref[2]
    half = ROPE_DIMS // 2
    a_ref[:, 0:MIX_W] = (_rotate(ua[:, 0:MIX_W], cos, s_lo, s_hi, half) * scale).astype(BF16)
    a_ref[:, MIX_W:2 * MIX_W] = _rotate(ua[:, MIX_W:2 * MIX_W], cos, s_lo, s_hi, half).astype(BF16)
    a_ref[:, 2 * MIX_W:] = ua[:, 2 * MIX_W:].astype(BF16)

    b_ref[...] = dot(wb_ref).astype(BF16)
    z_ref[...] = dot(wz_ref).astype(BF16)
    ab_ref[...] = dot(wab_ref)[:, :ab_ref.shape[-1]]

    cos, s_lo, s_hi = rot_c_ref[0], rot_c_ref[1], rot_c_ref[2]
    quarter = HEAD_DIM // 4
    uq = dot(wcq_ref)
    uq = uq * lax.rsqrt(_head_sums(uq * uq) * (1.0 / HEAD_DIM) + RMS_EPS) * cqg_ref[...]
    cq_ref[...] = (_rotate(uq, cos, s_lo, s_hi, quarter) * (scale * LOG2_E)).astype(BF16)
    kw = C_KV_HEADS * HEAD_DIM
    uk = dot(wck_ref)
    uk = uk * lax.rsqrt(_head_sums(uk * uk) * (1.0 / HEAD_DIM) + RMS_EPS) * ckg_ref[...]
    ck_ref[...] = _rotate(uk, cos[:, :kw], s_lo[:, :kw], s_hi[:, :kw], quarter).astype(BF16)
    cvt_ref[...] = lax.dot_general(wcvt_ref[...], h, (((1,), (1,)), ((), ())),
                                   preferred_element_type=F32).astype(BF16)

    ud = dot(wd_ref)
    d_ref[:, 0:MIX_W] = (ud[:, 0:MIX_W] * scale).astype(BF16)
    d_ref[:, MIX_W:] = ud[:, MIX_W:].astype(BF16)


def _rotary_tables(seq):
    pos = np.arange(seq, dtype=np.float64)

    def table(pos_of_lane, theta, half, lane_in_group, active):
        freq_idx = lane_in_group % half
        inv = theta ** (-(freq_idx.astype(np.float64)) / half)
        ang = pos_of_lane * inv[None, :]
        cos = np.where(active[None, :], np.cos(ang), 1.0)
        sin = np.where(active[None, :], np.sin(ang), 0.0)
        low = (lane_in_group < half)[None, :]
        return np.stack([cos, np.where(low, -sin, 0.0), np.where(low, 0.0, sin)]).astype(np.float32)

    lane = np.arange(MIX_W) % HEAD_DIM
    rot_a = table(np.broadcast_to(pos[:, None], (seq, MIX_W)), ROPE_THETA, ROPE_DIMS // 2, lane % ROPE_DIMS,
                  lane < ROPE_DIMS)
    rows, cols = pos // GRID_W, pos % GRID_W
    half = HEAD_DIM // 2
    pos_c = np.where((lane < half)[None, :], rows[:, None], cols[:, None])
    rot_c = table(pos_c, AXIAL_THETA, half // 2, lane % half, np.ones(MIX_W, bool))
    return jnp.asarray(rot_a), jnp.asarray(rot_c)


def _in_proj(x2d, seq, lw, tables, tm):
    t = x2d.shape[0]
    rot_a, rot_c = tables
    nblk = seq // tm
    tok = lambda w: pl.BlockSpec((tm, w), lambda i: (i, 0))
    rot = pl.BlockSpec((3, tm, MIX_W), lambda i: (0, i % nblk, 0))
    weights = (lw["norm_mix"], lw["wa"], lw["wb"], lw["wz"], lw["wab"], lw["wcq"], lw["wck"], lw["wcvt"], lw["wd"],
               lw["cq_gain"], lw["ck_gain"])
    kw = C_KV_HEADS * HEAD_DIM
    outs = ((3 * MIX_W, BF16), (3 * MIX_W, BF16), (MIX_W, BF16), (16, F32), (MIX_W, BF16), (kw, BF16),
            None, (3 * MIX_W, BF16))
    out_specs = [pl.BlockSpec((kw, tm), lambda i: (0, i)) if o is None else tok(o[0]) for o in outs]
    out_shape = [jax.ShapeDtypeStruct((kw, t), BF16) if o is None else jax.ShapeDtypeStruct((t, o[0]), o[1])
                 for o in outs]
    return pl.pallas_call(
        _in_proj_kernel,
        grid=(t // tm,),
        in_specs=[tok(D_MODEL)] + [_const_spec(w.shape) for w in weights] + [rot, rot],
        out_specs=out_specs,
        out_shape=out_shape,
        compiler_params=_params("parallel"),
    )(x2d, *weights, rot_a, rot_c)


GQA_SPLIT = 2


def _gqa_kernel(q_ref, k_ref, vt_ref, o_ref, acc_ref, *, tk):
    tq = q_ref.shape[0]
    seq = k_ref.shape[0]
    kw = C_KV_HEADS * HEAD_DIM
    qt = q_ref[...].astype(F32).T
    row_g = lax.broadcasted_iota(jnp.int32, (kw, 1), 0) // HEAD_DIM
    qs = jnp.concatenate([jnp.where(row_g == g, qt[blk * kw:(blk + 1) * kw], 0.0)
                          for g in range(C_KV_HEADS) for blk in range(2)], axis=1).astype(BF16)
    acc_ref[...] = jnp.zeros(acc_ref.shape, F32)

    width = 4 * tq // GQA_SPLIT

    def columns(c, start, m_old, l_old, out):
        cols = slice(c * width, (c + 1) * width)
        s = jnp.dot(k_ref[pl.ds(start, tk), :], qs[:, cols], preferred_element_type=F32)
        yield
        m_new = jnp.maximum(m_old, jnp.max(s, axis=0, keepdims=True))
        alpha = jnp.exp2(m_old - m_new)
        p = jnp.exp2(s - m_new)
        pv = jnp.dot(vt_ref[:, pl.ds(start, tk)], p.astype(BF16), preferred_element_type=F32)
        yield
        acc_ref[:, cols] = acc_ref[:, cols] * alpha + pv
        out[c] = (m_new, l_old * alpha + jnp.sum(p, axis=0, keepdims=True))

    def body(j, carry):
        start = pl.multiple_of(j * tk, tk)
        out = [None] * GQA_SPLIT
        _interleave(*[columns(c, start, *carry[c], out) for c in range(GQA_SPLIT)])
        return tuple(out)

    init = ((jnp.full((1, width), NEG_INF, F32), jnp.zeros((1, width), F32)),) * GQA_SPLIT
    carry = lax.fori_loop(0, seq // tk, body, init)
    o = acc_ref[...] / jnp.concatenate([l for _, l in carry], axis=1)
    for blk in range(2):
        lo = o[:, blk * tq:(blk + 1) * tq]
        hi = o[:, (2 + blk) * tq:(3 + blk) * tq]
        o_ref[:, blk * kw:(blk + 1) * kw] = jnp.where(row_g == 0, lo, hi).T.astype(o_ref.dtype)


def _gqa(cq, ck, cvt, batch, seq, tq, tk):
    nq = seq // tq
    kw = C_KV_HEADS * HEAD_DIM
    return pl.pallas_call(
        functools.partial(_gqa_kernel, tk=tk),
        grid=(batch, nq),
        in_specs=[pl.BlockSpec((tq, MIX_W), lambda b, i: (b * nq + i, 0)),
                  pl.BlockSpec((seq, kw), lambda b, i: (b, 0)),
                  pl.BlockSpec((kw, seq), lambda b, i: (0, b))],
        out_specs=pl.BlockSpec((tq, MIX_W), lambda b, i: (b * nq + i, 0)),
        out_shape=jax.ShapeDtypeStruct((batch * seq, MIX_W), BF16),
        scratch_shapes=[pltpu.VMEM((kw, 4 * tq), F32)],
        compiler_params=_params("parallel", "arbitrary"),
    )(cq, ck, cvt)


def _stack_heads(q):
    ids = _head_ids(MIX_W)
    zero = jnp.zeros((), q.dtype)
    return jnp.concatenate([jnp.where(ids == h, q, zero) for h in range(N_HEADS)], axis=0)


def _unstack_heads(o, n):
    ids = _head_ids(MIX_W)
    out = o[0:n]
    for h in range(1, N_HEADS):
        out = jnp.where(ids == h, o[h * n:(h + 1) * n], out)
    return out


NBR_UNROLL = 2


def _nbr_kernel(q_ref, k_ref, v_ref, bias_ref, o_ref, *, rows_per_step, grid_rows, kr):
    j = pl.program_id(1)
    win = kr * GRID_W

    def query_row(i):
        r = j * rows_per_step + i
        rs = jnp.clip(r - kr // 2, 0, grid_rows - kr)
        kstart = pl.multiple_of(rs * GRID_W, GRID_W)
        qstart = pl.multiple_of(i * GRID_W, GRID_W)
        qs = _stack_heads(q_ref[pl.ds(qstart, GRID_W), :])
        s = lax.dot_general(qs, k_ref[pl.ds(kstart, win), :], (((1,), (1,)), ((), ())),
                            preferred_element_type=F32)
        yield
        s = s + bias_ref[r - rs]
        m = jnp.max(s, axis=-1, keepdims=True)
        p = jnp.exp(s - m)
        l = jnp.sum(p, axis=-1, keepdims=True)
        o = jnp.dot(p.astype(BF16), v_ref[pl.ds(kstart, win), :], preferred_element_type=F32)
        yield
        o_ref[pl.ds(qstart, GRID_W), :] = _unstack_heads(o / l, GRID_W).astype(o_ref.dtype)

    def body(i, carry):
        _interleave(*[query_row(i * NBR_UNROLL + u) for u in range(NBR_UNROLL)])
        return carry

    lax.fori_loop(0, rows_per_step // NBR_UNROLL, body, 0)


def _nbr_bias(rpb, kr):
    c = np.arange(GRID_W)
    cs = np.clip(c - NA_COLS // 2, 0, GRID_W - NA_COLS)
    col_valid = (c[None, :] >= cs[:, None]) & (c[None, :] < cs[:, None] + NA_COLS)
    dc = np.clip(c[None, :] - c[:, None], -(NA_COLS - 1), NA_COLS - 1) + NA_COLS - 1
    dr = np.arange(kr)[None, :] - np.arange(kr)[:, None] + NA_ROWS - 1
    pick_r = (dr[..., None] == np.arange(2 * NA_ROWS - 1)).astype(np.float32)
    pick_c = (dc[..., None] == np.arange(2 * NA_COLS - 1)).astype(np.float32)
    bias = jnp.einsum('hab,vja,qkb->vhqjk', rpb.astype(F32), pick_r, pick_c, precision=lax.Precision.HIGHEST)
    bias = jnp.where(col_valid[None, None, :, None, :], bias, NEG_INF)
    return bias.reshape(kr, N_HEADS * GRID_W, kr * GRID_W)


def _nbr(dqkv, bias, batch, seq, rows_per_step):
    grid_rows = seq // GRID_W
    kr = min(NA_ROWS, grid_rows)
    nblk = grid_rows // rows_per_step
    tq = rows_per_step * GRID_W
    return pl.pallas_call(
        functools.partial(_nbr_kernel, rows_per_step=rows_per_step, grid_rows=grid_rows, kr=kr),
        grid=(batch, nblk),
        in_specs=[pl.BlockSpec((tq, MIX_W), lambda b, i: (b * nblk + i, 0)),
                  pl.BlockSpec((seq, MIX_W), lambda b, i: (b, 1)),
                  pl.BlockSpec((seq, MIX_W), lambda b, i: (b, 2)),
                  _const_spec(bias.shape)],
        out_specs=pl.BlockSpec((tq, MIX_W), lambda b, i: (b * nblk + i, 0)),
        out_shape=jax.ShapeDtypeStruct((batch * seq, MIX_W), BF16),
        compiler_params=_params("parallel", "arbitrary"),
    )(dqkv, dqkv, dqkv, bias)


BAND_UNROLL = 2


def _band_kernel(qkv_ref, o_ref, lse_ref, *, half, kwin, tq):
    length = qkv_ref.shape[0]
    rows = o_ref.shape[0]
    nq = rows // tq
    row0 = pl.program_id(2) * rows

    def unit(u):
        r, qi = u // nq, u % nq
        col = pl.multiple_of(r * (3 * MIX_W), LANES)
        ocol = pl.multiple_of(r * MIX_W, LANES)
        q_local = pl.multiple_of(qi * tq, tq)
        i0 = row0 + q_local
        kstart = pl.multiple_of(jnp.clip(i0 - half, 0, length - kwin), BF16_SUBLANES)
        qs = _stack_heads(qkv_ref[pl.ds(pl.multiple_of(i0, tq), tq), pl.ds(col, MIX_W)])
        s = lax.dot_general(qs, qkv_ref[pl.ds(kstart, kwin), pl.ds(col + MIX_W, MIX_W)],
                            (((1,), (1,)), ((), ())), preferred_element_type=F32)
        yield
        qpos = i0 + lax.broadcasted_iota(jnp.int32, (N_HEADS * tq, kwin), 0) % tq
        kpos = kstart + lax.broadcasted_iota(jnp.int32, (N_HEADS * tq, kwin), 1)
        s = jnp.where(jnp.abs(qpos - kpos) <= half, s, NEG_INF)
        m = jnp.max(s, axis=-1, keepdims=True)
        p = jnp.exp(s - m)
        l = jnp.sum(p, axis=-1, keepdims=True)
        o = jnp.dot(p.astype(BF16), qkv_ref[pl.ds(kstart, kwin), pl.ds(col + 2 * MIX_W, MIX_W)],
                    preferred_element_type=F32)
        yield
        o_ref[pl.ds(q_local, tq), pl.ds(ocol, MIX_W)] = _unstack_heads(o / l, tq).astype(o_ref.dtype)
        lse = jnp.broadcast_to(m + jnp.log(l), (N_HEADS * tq, MIX_W))
        lse_ref[pl.ds(q_local, tq), pl.ds(ocol, MIX_W)] = _unstack_heads(lse, tq)

    units = (o_ref.shape[1] // MIX_W) * nq

    def body(i, carry):
        _interleave(*[unit(i * BAND_UNROLL + j) for j in range(BAND_UNROLL)])
        return carry

    lax.fori_loop(0, units // BAND_UNROLL, body, 0)


def _band(a, batch, seq, window, dil, tq, step_rows):
    half = window // (2 * dil)
    length = seq // dil
    tq = min(tq, length)
    kwin = min(tq + 2 * half, length)
    rows = min(step_rows, length)
    nres = min(dil, max(1, step_rows // length))
    view = a.reshape(batch, length, dil * 3 * MIX_W)
    out = pl.BlockSpec((None, rows, nres * MIX_W), lambda b, r, i: (b, i, r))
    o, lse = pl.pallas_call(
        functools.partial(_band_kernel, half=half, kwin=kwin, tq=tq),
        grid=(batch, dil // nres, length // rows),
        in_specs=[pl.BlockSpec((None, length, nres * 3 * MIX_W), lambda b, r, i: (b, 0, r))],
        out_specs=[out, out],
        out_shape=[jax.ShapeDtypeStruct((batch, length, dil * MIX_W), BF16),
                   jax.ShapeDtypeStruct((batch, length, dil * MIX_W), F32)],
        compiler_params=_params("parallel", "parallel", "arbitrary"),
    )(view)
    return o.reshape(batch * seq, MIX_W), lse.reshape(batch * seq, MIX_W)


CUM_ROWS = 256


def _chunk_tri(n, upper):
    r = lax.broadcasted_iota(jnp.int32, (n, n), 0)
    c = lax.broadcasted_iota(jnp.int32, (n, n), 1)
    same = (r // B_CHUNK) == (c // B_CHUNK)
    keep = (r <= c) if upper else (r >= c)
    return jnp.where(same & keep, 1.0, 0.0).astype(F32)


def _delta_prep_kernel(x_ref, prev_ref, next_ref, ab_ref, conv_ref, alog_ref, dtb_ref,
                       qkv_ref, gate_ref, xe_ref):
    j = pl.program_id(1)
    rows = x_ref.shape[0]
    pad = BF16_SUBLANES
    keep_prev = jnp.where(j > 0, 1.0, 0.0)
    keep_next = jnp.where(j < pl.num_programs(1) - 1, 1.0, 0.0)
    xe_ref[0:pad, :] = prev_ref[...].astype(F32) * keep_prev
    xe_ref[pad:pad + rows, :] = x_ref[...].astype(F32)
    xe_ref[pad + rows:, :] = next_ref[...].astype(F32) * keep_next
    centre = B_CONV // 2
    y = jnp.zeros((rows, 3 * MIX_W), F32)
    for tap in range(B_CONV):
        y = y + xe_ref[pad + tap - centre:pad + tap - centre + rows, :] * conv_ref[tap:tap + 1, :]
    y = y * jax.nn.sigmoid(y)
    q, k = y[:, 0:MIX_W], y[:, MIX_W:2 * MIX_W]
    q = q * lax.rsqrt(_head_sums(q * q) + 1e-6) * (HEAD_DIM ** -0.5)
    k = k * lax.rsqrt(_head_sums(k * k) + 1e-6)
    qkv_ref[:, 0:MIX_W] = q.astype(BF16)
    qkv_ref[:, MIX_W:2 * MIX_W] = k.astype(BF16)
    qkv_ref[:, 2 * MIX_W:] = y[:, 2 * MIX_W:].astype(BF16)

    ab = ab_ref[...]
    col = lax.broadcasted_iota(jnp.int32, (1, ab.shape[-1]), 1)
    xs = ab + dtb_ref[...]
    softplus = jnp.maximum(xs, 0.0) + jnp.log(1.0 + jnp.exp(-jnp.abs(xs)))
    g = -jnp.exp(alog_ref[...]) * softplus
    beta = jax.nn.sigmoid(ab)
    tri_f, tri_b = _chunk_tri(CUM_ROWS, False), _chunk_tri(CUM_ROWS, True)
    for r0 in range(0, rows, CUM_ROWS):
        gp = g[r0:r0 + CUM_ROWS]
        cum_f = jnp.dot(tri_f, gp, preferred_element_type=F32, precision=lax.Precision.HIGHEST)
        cum_b = jnp.dot(tri_b, gp, preferred_element_type=F32, precision=lax.Precision.HIGHEST)
        gate_ref[r0:r0 + CUM_ROWS, :] = jnp.where(col < 8, beta[r0:r0 + CUM_ROWS],
                                                   jnp.where(col < 12, cum_f, cum_b))


def _delta_prep(bqkv, ab, lw, batch, seq, rows):
    t = batch * seq
    nblk = seq // rows
    per = rows // BF16_SUBLANES
    last = t // BF16_SUBLANES - 1
    w3 = 3 * MIX_W
    return pl.pallas_call(
        _delta_prep_kernel,
        grid=(batch, nblk),
        in_specs=[pl.BlockSpec((rows, w3), lambda b, j: (b * nblk + j, 0)),
                  pl.BlockSpec((BF16_SUBLANES, w3), lambda b, j: (jnp.maximum((b * nblk + j) * per - 1, 0), 0)),
                  pl.BlockSpec((BF16_SUBLANES, w3), lambda b, j: (jnp.minimum((b * nblk + j + 1) * per, last), 0)),
                  pl.BlockSpec((rows, 16), lambda b, j: (b * nblk + j, 0)),
                  _const_spec(lw["conv_w"].shape), _const_spec((1, 16)), _const_spec((1, 16))],
        out_specs=[pl.BlockSpec((rows, w3), lambda b, j: (b * nblk + j, 0)),
                   pl.BlockSpec((rows, 16), lambda b, j: (b * nblk + j, 0))],
        out_shape=[jax.ShapeDtypeStruct((t, w3), BF16), jax.ShapeDtypeStruct((t, 16), F32)],
        scratch_shapes=[pltpu.VMEM((rows + 2 * BF16_SUBLANES, w3), F32)],
        compiler_params=_params("parallel", "arbitrary"),
    )(bqkv, bqkv, bqkv, ab, lw["conv_w"], lw["a_log16"], lw["dt_bias16"])


def _expand_heads(cols):
    ids = _head_ids(MIX_W)
    out = jnp.broadcast_to(cols[:, 0:1], (cols.shape[0], MIX_W))
    for h in range(1, N_HEADS):
        out = jnp.where(ids == h, jnp.broadcast_to(cols[:, h:h + 1], out.shape), out)
    return out


def _block_diag(packed, mask):
    return jnp.where(mask, jnp.concatenate([packed] * N_HEADS, axis=0), 0.0).astype(BF16)


def _mm(a, b):
    return jnp.dot(a.astype(BF16), b.astype(BF16), preferred_element_type=F32)


def _delta_local(backward, qkv_ref, gate_ref, grow_ref, u_ref, w_ref, qh_ref, kt_ref, in_ref, eg_ref, c):
    n = B_CHUNK
    d = 1 if backward else 0
    r0 = pl.multiple_of(c * n, n)
    q = qkv_ref[pl.ds(r0, n), 0:MIX_W]
    k = qkv_ref[pl.ds(r0, n), MIX_W:2 * MIX_W]
    v = qkv_ref[pl.ds(r0, n), 2 * MIX_W:].astype(F32)
    gates = gate_ref[pl.ds(r0, n), :]
    d0 = 4 if backward else 0
    beta = _expand_heads(gates[:, d0:d0 + 4])
    gc = _expand_heads(gates[:, 8 + d0:12 + d0])
    gc_row = grow_ref[pl.ds(c, 1), d0 * n:(d0 + 4) * n]

    row = lax.broadcasted_iota(jnp.int32, (n, MIX_W), 0)
    col = lax.broadcasted_iota(jnp.int32, (n, MIX_W), 1) % n
    incl = (row <= col) if backward else (row >= col)
    strict = (row < col) if backward else (row > col)
    bd_mask = (lax.broadcasted_iota(jnp.int32, (MIX_W, MIX_W), 0) // n
               == lax.broadcasted_iota(jnp.int32, (MIX_W, MIX_W), 1) // n)
    decay = jnp.exp(jnp.where(incl, gc - gc_row, NEG_INF))

    kf = k.astype(F32)
    kq = lax.dot_general(jnp.concatenate([q, k], axis=0), _stack_heads(k), (((1,), (1,)), ((), ())),
                         preferred_element_type=F32)
    yield
    intra = kq[0:n] * decay
    low = jnp.where(strict, kq[n:] * beta * decay, 0.0)

    x = jnp.where(row == col, 1.0, 0.0) - low
    p = _mm(low, _block_diag(low, bd_mask))
    yield
    steps = int(math.log2(n)) - 1
    for step in range(steps):
        last = step == steps - 1
        lhs = x if last else jnp.concatenate([x, p], axis=0)
        r = _mm(lhs, _block_diag(p, bd_mask))
        yield
        x = x + r[0:n]
        if not last:
            p = r[n:]

    kbeta = kf * beta
    u_ref[d, pl.ds(r0, n), :] = _mm(x, _block_diag(v * beta, bd_mask))
    w_ref[d, pl.ds(r0, n), :] = _mm(x, _block_diag(kbeta * jnp.exp(gc), bd_mask)).astype(BF16)
    edge = gc[0:1] if backward else gc[n - 1:n]
    qh_ref[d, pl.ds(r0, n), :] = (q.astype(F32) * jnp.exp(gc)).astype(BF16)
    kt_ref[d, pl.ds(r0, n), :] = (kf * jnp.exp(edge - gc)).astype(BF16)
    in_ref[d, pl.ds(r0, n), :] = intra.astype(BF16)
    eg_ref[d, pl.ds(c, 1), :] = jnp.exp(edge)


def _delta_step(backward, u_ref, w_ref, qh_ref, kt_ref, in_ref, eg_ref, o_ref, s_ref, c):
    n = B_CHUNK
    d = 1 if backward else 0
    r0 = pl.multiple_of(c * n, n)
    bd_mask = (lax.broadcasted_iota(jnp.int32, (MIX_W, MIX_W), 0) // n
               == lax.broadcasted_iota(jnp.int32, (MIX_W, MIX_W), 1) // n)
    state = s_ref[d]
    lhs = jnp.concatenate([w_ref[d, pl.ds(r0, n), :], qh_ref[d, pl.ds(r0, n), :]], axis=0)
    ws_qs = jnp.dot(lhs, state.astype(BF16), preferred_element_type=F32)
    yield
    v_new = u_ref[d, pl.ds(r0, n), :] - ws_qs[0:n]
    o_ref[pl.ds(r0, n), :] = ws_qs[n:] + jnp.dot(in_ref[d, pl.ds(r0, n), :], _block_diag(v_new, bd_mask),
                                                  preferred_element_type=F32)
    kv = lax.dot_general(kt_ref[d, pl.ds(r0, n), :], v_new.astype(BF16), (((0,), (0,)), ((), ())),
                         preferred_element_type=F32)
    s_ref[d] = state * eg_ref[d, pl.ds(c, 1), :] + jnp.where(bd_mask, kv, 0.0)


LOCAL_UNROLL = 4


def _delta_kernel(qkv_f, gate_f, grow_f, qkv_b, gate_b, grow_b, of_ref, ob_ref,
                  s_ref, u_ref, w_ref, qh_ref, kt_ref, in_ref, eg_ref):
    @pl.when(pl.program_id(1) == 0)
    def _():
        s_ref[...] = jnp.zeros(s_ref.shape, F32)

    nchunk = qkv_f.shape[0] // B_CHUNK
    staged = (u_ref, w_ref, qh_ref, kt_ref, in_ref, eg_ref)

    def local(i, carry):
        chunks = [i * LOCAL_UNROLL + g for g in range(LOCAL_UNROLL)]
        _interleave(*[_delta_local(False, qkv_f, gate_f, grow_f, *staged, c) for c in chunks],
                    *[_delta_local(True, qkv_b, gate_b, grow_b, *staged, c) for c in chunks])
        return carry

    lax.fori_loop(0, nchunk // LOCAL_UNROLL, local, 0)

    def step(i, carry):
        _interleave(_delta_step(False, *staged, of_ref, s_ref, i),
                    _delta_step(True, *staged, ob_ref, s_ref, nchunk - 1 - i))
        return carry

    lax.fori_loop(0, nchunk, step, 0)


def _delta(qkvn, gates, batch, seq, rows):
    t = batch * seq
    nblk = seq // rows
    nchunk = rows // B_CHUNK
    grow = gates[:, 8:16].reshape(t // B_CHUNK, B_CHUNK, 8).transpose(0, 2, 1).reshape(t // B_CHUNK, 8 * B_CHUNK)
    fwd = lambda b, j: (b * nblk + j, 0)
    bwd = lambda b, j: (b * nblk + nblk - 1 - j, 0)
    specs = lambda m: [pl.BlockSpec((rows, 3 * MIX_W), m), pl.BlockSpec((rows, 16), m),
                       pl.BlockSpec((nchunk, 8 * B_CHUNK), m)]
    return pl.pallas_call(
        _delta_kernel,
        grid=(batch, nblk),
        in_specs=specs(fwd) + specs(bwd),
        out_specs=[pl.BlockSpec((rows, MIX_W), fwd), pl.BlockSpec((rows, MIX_W), bwd)],
        out_shape=[jax.ShapeDtypeStruct((t, MIX_W), F32)] * 2,
        scratch_shapes=([pltpu.VMEM((2, MIX_W, MIX_W), F32), pltpu.VMEM((2, rows, MIX_W), F32)]
                        + [pltpu.VMEM((2, rows, MIX_W), BF16)] * 4 + [pltpu.VMEM((2, nchunk, MIX_W), F32)]),
        compiler_params=_params("parallel", "arbitrary"),
    )(qkvn, gates, grow, qkvn, gates, grow)


def _merge_kernel(x_ref, gain_ref, oa0, oa1, oa2, la0, la1, la2, bf_ref, bb_ref, z_ref, bgain_ref,
                  oc_ref, od_ref, wg_ref, bg_ref, wbr_ref, wout_ref, out_ref):
    x = x_ref[...]
    h = _rms_rows(x, gain_ref[...]).astype(BF16)

    lses = [r[...] for r in (la0, la1, la2)]
    top = jnp.maximum(jnp.maximum(lses[0], lses[1]), lses[2])
    ws = [jnp.exp(l - top) for l in lses]
    o_a = sum(w * r[...].astype(F32) for w, r in zip(ws, (oa0, oa1, oa2))) / (ws[0] + ws[1] + ws[2])

    o_b = bf_ref[...] + bb_ref[...]
    z = z_ref[...].astype(F32)
    o_b = (o_b * lax.rsqrt(_head_sums(o_b * o_b) * (1.0 / HEAD_DIM) + RMS_EPS) * bgain_ref[...]
           * (z * jax.nn.sigmoid(z)))

    branches = (o_a, o_b, oc_ref[...], od_ref[...])
    y = jnp.zeros(x.shape, F32)
    for i, o in enumerate(branches):
        gate = jax.nn.sigmoid(jnp.dot(h, wg_ref[i], preferred_element_type=F32) + bg_ref[i])
        y = y + gate * jnp.dot(o.astype(BF16), wbr_ref[i], preferred_element_type=F32)
    out_ref[...] = x + jnp.dot(y.astype(BF16), wout_ref[...], preferred_element_type=F32)


def _merge(x2d, a_outs, b_outs, z, oc, od, lw, tm):
    t = x2d.shape[0]
    tok = lambda w: pl.BlockSpec((tm, w), lambda i: (i, 0))
    (o0, l0), (o1, l1), (o2, l2) = a_outs
    acts = (o0, o1, o2, l0, l1, l2, b_outs[0], b_outs[1], z)
    weights = (lw["wg"], lw["bg"], lw["wbr"], lw["wout"])
    return pl.pallas_call(
        _merge_kernel,
        grid=(t // tm,),
        in_specs=([tok(D_MODEL), _const_spec(lw["norm_mix"].shape)] + [tok(MIX_W)] * len(acts)
                  + [_const_spec(lw["b_gain"].shape), tok(MIX_W), tok(MIX_W)]
                  + [_const_spec(w.shape) for w in weights]),
        out_specs=tok(D_MODEL),
        out_shape=jax.ShapeDtypeStruct((t, D_MODEL), F32),
        compiler_params=_params("parallel"),
    )(x2d, lw["norm_mix"], *acts, lw["b_gain"], oc, od, *weights)


EXPERT_UNROLL = 2


def _moe_kernel(x_ref, gain_ref, wr_ref, br_ref, wgt_ref, wup_ref, wdn_ref, fgain_ref, out_ref,
                hs_ref, ws_ref, ys_ref, *, final, cap):
    x = x_ref[...]
    tm = x.shape[0]
    h = _rms_rows(x, gain_ref[...])
    hb = h.astype(BF16)
    h_lo = (h - hb.astype(F32)).astype(BF16)
    parts = jnp.dot(jnp.concatenate([hb, h_lo], axis=0), wr_ref[...], preferred_element_type=F32)
    logits = (parts[:tm, :LANES] + parts[:tm, LANES:]) + (parts[tm:, :LANES] + parts[tm:, LANES:]) + br_ref[...]
    lane = lax.broadcasted_iota(jnp.int32, (1, LANES), 1)
    none = LANES

    def first_max(vals):
        top = jnp.max(vals, axis=-1, keepdims=True)
        return top, jnp.min(jnp.where(vals == top, lane, none), axis=-1, keepdims=True)

    gl = jnp.where(lane < N_GROUPS, logits, NEG_INF)
    g_top, g_idx = first_max(gl)
    g_prob = 1.0 / jnp.sum(jnp.exp(gl - g_top), axis=-1, keepdims=True)
    in_group = (lane >= N_GROUPS) & (lane < N_GROUPS + N_EXPERTS) & ((lane - N_GROUPS) // EXPERTS_PER_GROUP == g_idx)
    el = jnp.where(in_group, logits, NEG_INF)
    top1, idx1 = first_max(el)
    top2, idx2 = first_max(jnp.where(lane == idx1, NEG_INF, el))
    e2 = jnp.exp(top2 - top1)
    w1 = g_prob / (1.0 + e2)
    weights = jnp.where(lane == idx1, w1, 0.0) + jnp.where(lane == idx2, w1 * e2, 0.0)

    def expert(rows, row_weights, e, outs):
        w_e = jnp.sum(jnp.where(lane == e + N_GROUPS, row_weights, 0.0), axis=-1, keepdims=True)
        gate = jnp.dot(rows, wgt_ref[e], preferred_element_type=F32)
        up = jnp.dot(rows, wup_ref[e], preferred_element_type=F32)
        yield
        hid = gate * jax.nn.sigmoid(gate) * up * w_e
        outs.append(jnp.dot(hid.astype(BF16), wdn_ref[e], preferred_element_type=F32))

    def finish(moe_out):
        y = x + moe_out
        if final:
            y = _rms_rows(y, fgain_ref[...])
        out_ref[...] = y

    member = jnp.where(lane == g_idx, 1.0, 0.0)
    fits = jnp.max(jnp.sum(member, axis=0, keepdims=True)) <= cap

    @pl.when(fits)
    def _():
        before = lax.broadcasted_iota(jnp.int32, (tm, tm), 0) > lax.broadcasted_iota(jnp.int32, (tm, tm), 1)
        ahead = jnp.dot(jnp.where(before, 1.0, 0.0).astype(BF16), member.astype(BF16),
                        preferred_element_type=F32)
        pos = jnp.sum(member * ahead, axis=-1, keepdims=True).astype(jnp.int32)
        slot = g_idx * cap + pos
        perm_t = jnp.where(lax.broadcasted_iota(jnp.int32, (1, N_GROUPS * cap), 1) == slot, 1.0, 0.0).astype(BF16)
        gather = lambda a: lax.dot_general(perm_t, a, (((0,), (0,)), ((), ())), preferred_element_type=F32)
        hs_ref[...] = gather(hb).astype(BF16)
        w_hi = weights.astype(BF16)
        w_parts = gather(jnp.concatenate([w_hi, (weights - w_hi.astype(F32)).astype(BF16)], axis=1))
        ws_ref[...] = w_parts[:, :LANES] + w_parts[:, LANES:]

        def group(g, carry):
            r0 = pl.multiple_of(g * cap, cap)
            rows, row_weights = hs_ref[pl.ds(r0, cap), :], ws_ref[pl.ds(r0, cap), :]
            outs = []
            _interleave(*[expert(rows, row_weights, g * EXPERTS_PER_GROUP + j, outs)
                          for j in range(EXPERTS_PER_GROUP)])
            ys_ref[pl.ds(r0, cap), :] = sum(outs[1:], outs[0]).astype(BF16)
            return carry

        lax.fori_loop(0, N_GROUPS, group, 0)
        finish(jnp.dot(perm_t, ys_ref[...], preferred_element_type=F32))

    @pl.when(jnp.logical_not(fits))
    def _():
        def experts(i, acc):
            outs = []
            _interleave(*[expert(hb, weights, i * EXPERT_UNROLL + j, outs) for j in range(EXPERT_UNROLL)])
            return acc + sum(outs[1:], outs[0])

        finish(lax.fori_loop(0, N_EXPERTS // EXPERT_UNROLL, experts, jnp.zeros(x.shape, F32)))


def _moe(x2d, lw, final_gain, final, tm, cap):
    t = x2d.shape[0]
    tok = pl.BlockSpec((tm, D_MODEL), lambda i: (i, 0))
    once = lambda a: pl.BlockSpec(a.shape, lambda i: (0,) * a.ndim, pipeline_mode=pl.Buffered(1))
    weights = (lw["norm_ffn"], lw["w_route"], lw["b_route"], lw["w_ff_gate"], lw["w_ff_up"], lw["w_ff_down"],
               final_gain)
    return pl.pallas_call(
        functools.partial(_moe_kernel, final=final, cap=cap),
        grid=(t // tm,),
        in_specs=[tok] + [once(w) for w in weights],
        out_specs=tok,
        out_shape=jax.ShapeDtypeStruct((t, D_MODEL), F32),
        scratch_shapes=[pltpu.VMEM((N_GROUPS * cap, D_MODEL), BF16), pltpu.VMEM((N_GROUPS * cap, LANES), F32),
                        pltpu.VMEM((N_GROUPS * cap, D_MODEL), BF16)],
        compiler_params=_params("parallel"),
    )(x2d, *weights)


def _split_bf16(w):
    hi = w.astype(BF16)
    return jnp.concatenate([hi, (w - hi.astype(F32)).astype(BF16)], axis=1)


def _layer_weights(p, l):
    w_in = p["w_in"][l]
    offs = np.cumsum([0, 3 * MIX_W, 3 * MIX_W, MIX_W, 16, MIX_W, 2 * C_KV_HEADS * HEAD_DIM, 3 * MIX_W])
    sec = [w_in[:, offs[i]:offs[i + 1]] for i in range(7)]
    c_order = np.concatenate([np.arange(h * HEAD_DIM, (h + 1) * HEAD_DIM) for h in (0, 2, 1, 3)])
    row = lambda v: v.reshape(1, -1).astype(F32)
    gate16 = lambda v: jnp.concatenate([jnp.zeros((8,), F32), v.reshape(-1).astype(F32)]).reshape(1, 16)
    wbr = p["w_branch"][l]
    wbr = jnp.stack([wbr[0], wbr[1], wbr[2][c_order], wbr[3]])
    w_route = jnp.concatenate([p["w_route_group"][l], p["w_route_expert"][l].reshape(D_MODEL, N_EXPERTS)], axis=1)
    b_route = jnp.concatenate([p["b_route_group"][l], p["b_route_expert"][l].reshape(N_EXPERTS)])
    pad_lanes = lambda a: jnp.pad(a, [(0, 0)] * (a.ndim - 1) + [(0, LANES - a.shape[-1])])
    return dict(
        norm_mix=row(p["norm_mix"][l]),
        wa=sec[0].astype(BF16), wb=sec[1].astype(BF16), wz=sec[2].astype(BF16),
        wab=pad_lanes(sec[3]).astype(BF16), wcq=sec[4][:, c_order].astype(BF16), wck=sec[5][:, :C_KV_HEADS * HEAD_DIM].astype(BF16), wcvt=sec[5][:, C_KV_HEADS * HEAD_DIM:].T.astype(BF16),
        wd=sec[6].astype(BF16),
        cq_gain=row(jnp.tile(p["c_q_norm"][l], N_HEADS)), ck_gain=row(jnp.tile(p["c_k_norm"][l], C_KV_HEADS)),
        conv_w=p["conv_w"][l].astype(F32), a_log16=gate16(p["a_log"][l]), dt_bias16=gate16(p["dt_bias"][l]),
        b_gain=row(jnp.tile(p["b_norm"][l], N_HEADS)),
        wg=p["w_gate"][l].astype(BF16), bg=p["b_gate"][l].reshape(N_HEADS, 1, D_MODEL).astype(F32),
        wbr=wbr.astype(BF16), wout=p["w_out"][l].astype(BF16),
        norm_ffn=row(p["norm_ffn"][l]),
        w_route=_split_bf16(pad_lanes(w_route).astype(F32)), b_route=pad_lanes(b_route.reshape(1, -1)).astype(F32),
        w_ff_gate=p["w_ff_gate"][l].reshape(N_EXPERTS, D_MODEL, D_EXPERT).astype(BF16),
        w_ff_up=p["w_ff_up"][l].reshape(N_EXPERTS, D_MODEL, D_EXPERT).astype(BF16),
        w_ff_down=p["w_ff_down"][l].reshape(N_EXPERTS, D_EXPERT, D_MODEL).astype(BF16),
        d_rpb=p["d_rpb"][l],
    )


def _tiles(seq):
    tm = min(512, seq)
    share = tm // N_GROUPS
    moe_cap = (share + share // 4 + BF16_SUBLANES - 1) // BF16_SUBLANES * BF16_SUBLANES
    return dict(tm=tm, moe_cap=moe_cap, tq_c=min(256, seq), tk_c=min(512, seq), rows_d=min(8, seq // GRID_W),
                tq_a=128, rows_a=2048, rows_prep=min(512, seq), rows_delta=min(2048, seq))


def _encoder(x, layers, final_gain):
    batch, seq, _ = x.shape
    ts = _tiles(seq)
    tables = _rotary_tables(seq)
    depth = len(layers)
    x2d = x.reshape(batch * seq, D_MODEL)
    for l, lw in enumerate(layers):
        a, bqkv, z, ab, cq, ck, cvt, dqkv = _in_proj(x2d, seq, lw, tables, ts["tm"])
        a_outs = [_band(a, batch, seq, window, dil, ts["tq_a"], ts["rows_a"]) for window, dil in A_PATTERNS]
        qkvn, gates = _delta_prep(bqkv, ab, lw, batch, seq, ts["rows_prep"])
        b_outs = _delta(qkvn, gates, batch, seq, ts["rows_delta"])
        oc = _gqa(cq, ck, cvt, batch, seq, ts["tq_c"], ts["tk_c"])
        od = _nbr(dqkv, _nbr_bias(lw["d_rpb"], min(NA_ROWS, seq // GRID_W)), batch, seq, ts["rows_d"])
        x2d = _merge(x2d, a_outs, b_outs, z, oc, od, lw, ts["tm"])
        x2d = _moe(x2d, lw, final_gain, l == depth - 1, ts["tm"], ts["moe_cap"])
    return x2d.reshape(batch, seq, D_MODEL)


def kernel(x_prompt, x_sample, norm_mix, w_in, conv_w, a_log, dt_bias, b_norm, c_q_norm, c_k_norm, d_rpb,
           w_gate, b_gate, w_branch, w_out, norm_ffn, w_route_group, b_route_group, w_route_expert,
           b_route_expert, w_ff_gate, w_ff_up, w_ff_down, norm_final):
    p = dict(norm_mix=norm_mix, w_in=w_in, conv_w=conv_w, a_log=a_log, dt_bias=dt_bias, b_norm=b_norm,
             c_q_norm=c_q_norm, c_k_norm=c_k_norm, d_rpb=d_rpb, w_gate=w_gate, b_gate=b_gate,
             w_branch=w_branch, w_out=w_out, norm_ffn=norm_ffn, w_route_group=w_route_group,
             b_route_group=b_route_group, w_route_expert=w_route_expert, b_route_expert=b_route_expert,
             w_ff_gate=w_ff_gate, w_ff_up=w_ff_up, w_ff_down=w_ff_down)
    layers = [_layer_weights(p, l) for l in range(norm_mix.shape[0])]
    final_gain = norm_final.reshape(1, -1).astype(F32)
    return _encoder(x_prompt, layers, final_gain), _encoder(x_sample, layers, final_gain)
```

```python
import functools
import math

import jax
import jax.numpy as jnp
import numpy as np
from jax import lax
from jax.experimental import pallas as pl
from jax.experimental.pallas import tpu as pltpu

D_MODEL = 1024
HEAD_DIM = 64
N_HEADS = 4
MIX_W = N_HEADS * HEAD_DIM
GRID_W = 64
RMS_EPS = 1e-6
NEG_INF = -1e30
LOG2_E = math.log2(math.e)

A_PATTERNS = ((128, 1), (512, 4), (2048, 16))
ROPE_THETA = 500000.0
ROPE_DIMS = HEAD_DIM // 4
AXIAL_THETA = 10000.0
C_KV_HEADS = 2
B_CONV = 5
B_CHUNK = 64
NA_ROWS = 8
NA_COLS = 16
N_GROUPS = 4
EXPERTS_PER_GROUP = 4
N_EXPERTS = N_GROUPS * EXPERTS_PER_GROUP
D_EXPERT = 256

LANES = 128
BF16_SUBLANES = 16
VMEM_LIMIT = 56 * 1024 * 1024

F32 = jnp.float32
BF16 = jnp.bfloat16


def _params(*sem):
    return pltpu.CompilerParams(dimension_semantics=sem, vmem_limit_bytes=VMEM_LIMIT)


def _const_spec(shape):
    n = len(shape)
    return pl.BlockSpec(shape, lambda *_: (0,) * n)


def _rms_rows(x, gain):
    return x * lax.rsqrt(jnp.mean(x * x, axis=-1, keepdims=True) + RMS_EPS) * gain


def _head_ids(width):
    return lax.broadcasted_iota(jnp.int32, (1, width), 1) // HEAD_DIM


def _block_ones(width):
    r = lax.broadcasted_iota(jnp.int32, (width, width), 0) // HEAD_DIM
    c = lax.broadcasted_iota(jnp.int32, (width, width), 1) // HEAD_DIM
    return jnp.where(r == c, 1.0, 0.0).astype(BF16)


def _head_sums(x):
    return jnp.dot(x.astype(BF16), _block_ones(x.shape[-1]), preferred_element_type=F32)


def _interleave(*chains):
    live = list(chains)
    while live:
        live = [c for c in live if next(c, live) is not live]


def _rotate(x, cos, sin_lo, sin_hi, shift):
    w = x.shape[-1]
    return x * cos + pltpu.roll(x, shift, 1) * sin_hi + pltpu.roll(x, w - shift, 1) * sin_lo


def _in_proj_kernel(x_ref, gain_ref, wa_ref, wb_ref, wz_ref, wab_ref, wcq_ref, wck_ref, wcvt_ref, wd_ref,
                    cqg_ref, ckg_ref, rot_a_ref, rot_c_ref,
                    a_ref, b_ref, z_ref, ab_ref, cq_ref, ck_ref, cvt_ref, d_ref):
    h = _rms_rows(x_ref[...], gain_ref[...]).astype(BF16)
    dot = lambda w_ref: jnp.dot(h, w_ref[...], preferred_element_type=F32)
    scale = HEAD_DIM ** -0.5

    ua = dot(wa_ref)
    cos, s_lo, s_hi = rot_a_ref[0], rot_a_ref[1], rot_a_ref[2]
    half = ROPE_DIMS // 2
    a_ref[:, 0:MIX_W] = (_rotate(ua[:, 0:MIX_W], cos, s_lo, s_hi, half) * scale).astype(BF16)
    a_ref[:, MIX_W:2 * MIX_W] = _rotate(ua[:, MIX_W:2 * MIX_W], cos, s_lo, s_hi, half).astype(BF16)
    a_ref[:, 2 * MIX_W:] = ua[:, 2 * MIX_W:].astype(BF16)

    b_ref[...] = dot(wb_ref).astype(BF16)
    z_ref[...] = dot(wz_ref).astype(BF16)
    ab_ref[...] = dot(wab_ref)[:, :ab_ref.shape[-1]]

    cos, s_lo, s_hi = rot_c_ref[0], rot_c_ref[1], rot_c_ref[2]
    quarter = HEAD_DIM // 4
    uq = dot(wcq_ref)
    uq = uq * lax.rsqrt(_head_sums(uq * uq) * (1.0 / HEAD_DIM) + RMS_EPS) * cqg_ref[...]
    cq_ref[...] = (_rotate(uq, cos, s_lo, s_hi, quarter) * (scale * LOG2_E)).astype(BF16)
    kw = C_KV_HEADS * HEAD_DIM
    uk = dot(wck_ref)
    uk = uk * lax.rsqrt(_head_sums(uk * uk) * (1.0 / HEAD_DIM) + RMS_EPS) * ckg_ref[...]
    ck_ref[...] = _rotate(uk, cos[:, :kw], s_lo[:, :kw], s_hi[:, :kw], quarter).astype(BF16)
    cvt_ref[...] = lax.dot_general(wcvt_ref[...], h, (((1,), (1,)), ((), ())),
                                   preferred_element_type=F32).astype(BF16)

    ud = dot(wd_ref)
    d_ref[:, 0:MIX_W] = (ud[:, 0:MIX_W] * scale).astype(BF16)
    d_ref[:, MIX_W:] = ud[:, MIX_W:].astype(BF16)


def _rotary_tables(seq):
    pos = np.arange(seq, dtype=np.float64)

    def table(pos_of_lane, theta, half, lane_in_group, active):
        freq_idx = lane_in_group % half
        inv = theta ** (-(freq_idx.astype(np.float64)) / half)
        ang = pos_of_lane * inv[None, :]
        cos = np.where(active[None, :], np.cos(ang), 1.0)
        sin = np.where(active[None, :], np.sin(ang), 0.0)
        low = (lane_in_group < half)[None, :]
        return np.stack([cos, np.where(low, -sin, 0.0), np.where(low, 0.0, sin)]).astype(np.float32)

    lane = np.arange(MIX_W) % HEAD_DIM
    rot_a = table(np.broadcast_to(pos[:, None], (seq, MIX_W)), ROPE_THETA, ROPE_DIMS // 2, lane % ROPE_DIMS,
                  lane < ROPE_DIMS)
    rows, cols = pos // GRID_W, pos % GRID_W
    half = HEAD_DIM // 2
    pos_c = np.where((lane < half)[None, :], rows[:, None], cols[:, None])
    rot_c = table(pos_c, AXIAL_THETA, half // 2, lane % half, np.ones(MIX_W, bool))
    return jnp.asarray(rot_a), jnp.asarray(rot_c)


def _in_proj(x2d, seq, lw, tables, tm):
    t = x2d.shape[0]
    rot_a, rot_c = tables
    nblk = seq // tm
    tok = lambda w: pl.BlockSpec((tm, w), lambda i: (i, 0))
    rot = pl.BlockSpec((3, tm, MIX_W), lambda i: (0, i % nblk, 0))
    weights = (lw["norm_mix"], lw["wa"], lw["wb"], lw["wz"], lw["wab"], lw["wcq"], lw["wck"], lw["wcvt"], lw["wd"],
               lw["cq_gain"], lw["ck_gain"])
    kw = C_KV_HEADS * HEAD_DIM
    outs = ((3 * MIX_W, BF16), (3 * MIX_W, BF16), (MIX_W, BF16), (16, F32), (MIX_W, BF16), (kw, BF16),
            None, (3 * MIX_W, BF16))
    out_specs = [pl.BlockSpec((kw, tm), lambda i: (0, i)) if o is None else tok(o[0]) for o in outs]
    out_shape = [jax.ShapeDtypeStruct((kw, t), BF16) if o is None else jax.ShapeDtypeStruct((t, o[0]), o[1])
                 for o in outs]
    return pl.pallas_call(
        _in_proj_kernel,
        grid=(t // tm,),
        in_specs=[tok(D_MODEL)] + [_const_spec(w.shape) for w in weights] + [rot, rot],
        out_specs=out_specs,
        out_shape=out_shape,
        compiler_params=_params("parallel"),
    )(x2d, *weights, rot_a, rot_c)


def _gqa_kernel(q_ref, k_ref, vt_ref, o_ref, acc_ref, s_ref, *, tk):
    tq = q_ref.shape[0]
    seq = k_ref.shape[0]
    kw = C_KV_HEADS * HEAD_DIM
    qt = q_ref[...].astype(F32).T
    row_g = lax.broadcasted_iota(jnp.int32, (kw, 1), 0) // HEAD_DIM
    qs = jnp.concatenate([jnp.where(row_g == g, qt[blk * kw:(blk + 1) * kw], 0.0)
                          for g in range(C_KV_HEADS) for blk in range(2)], axis=1).astype(BF16)
    acc_ref[...] = jnp.zeros(acc_ref.shape, F32)

    width = 2 * tq

    nchunk = seq // tk

    def scores(j, slot):
        start = pl.multiple_of(j * tk, tk)
        s_ref[slot] = jnp.dot(k_ref[pl.ds(start, tk), :], qs, preferred_element_type=F32)

    def kv_head(g, j, slot, m_old, l_old, out):
        cols = slice(g * width, (g + 1) * width)
        start = pl.multiple_of(j * tk, tk)
        s = s_ref[slot, :, cols]
        m_new = jnp.maximum(m_old, jnp.max(s, axis=0, keepdims=True))
        alpha = jnp.exp2(m_old - m_new)
        p = jnp.exp2(s - m_new)
        pv = jnp.dot(vt_ref[g * HEAD_DIM:(g + 1) * HEAD_DIM, pl.ds(start, tk)], p.astype(BF16),
                     preferred_element_type=F32)
        yield
        acc_ref[:, cols] = acc_ref[:, cols] * alpha + pv
        out[g] = (m_new, l_old * alpha + jnp.sum(p, axis=0, keepdims=True))

    def softmax_chunk(j, slot, carry):
        out = [None] * C_KV_HEADS
        _interleave(*[kv_head(g, j, slot, *carry[g], out) for g in range(C_KV_HEADS)])
        return tuple(out)

    def chunk_pair(i, carry, last):
        scores(2 * i + 1, 1)
        carry = softmax_chunk(2 * i, 0, carry)
        if not last:
            scores(2 * i + 2, 0)
        return softmax_chunk(2 * i + 1, 1, carry)

    scores(0, 0)
    carry = ((jnp.full((1, width), NEG_INF, F32), jnp.zeros((1, width), F32)),) * C_KV_HEADS
    if nchunk == 1:
        carry = softmax_chunk(0, 0, carry)
    else:
        assert nchunk % 2 == 0
        carry = lax.fori_loop(0, nchunk // 2 - 1, lambda i, c: chunk_pair(i, c, False), carry)
        carry = chunk_pair(nchunk // 2 - 1, carry, True)
    o = acc_ref[...] / jnp.concatenate([l for _, l in carry], axis=1)
    for blk in range(2):
        pair = jnp.concatenate([o[:, (2 * g + blk) * tq:(2 * g + blk + 1) * tq] for g in range(C_KV_HEADS)], axis=0)
        o_ref[:, blk * kw:(blk + 1) * kw] = pair.T.astype(o_ref.dtype)


def _gqa(cq, ck, cvt, batch, seq, tq, tk):
    nq = seq // tq
    kw = C_KV_HEADS * HEAD_DIM
    return pl.pallas_call(
        functools.partial(_gqa_kernel, tk=tk),
        grid=(batch, nq),
        in_specs=[pl.BlockSpec((tq, MIX_W), lambda b, i: (b * nq + i, 0)),
                  pl.BlockSpec((seq, kw), lambda b, i: (b, 0)),
                  pl.BlockSpec((kw, seq), lambda b, i: (0, b))],
        out_specs=pl.BlockSpec((tq, MIX_W), lambda b, i: (b * nq + i, 0)),
        out_shape=jax.ShapeDtypeStruct((batch * seq, MIX_W), BF16),
        scratch_shapes=[pltpu.VMEM((HEAD_DIM, 4 * tq), F32), pltpu.VMEM((2, tk, 4 * tq), F32)],
        compiler_params=_params("parallel", "arbitrary"),
    )(cq, ck, cvt)


def _stack_heads(q):
    ids = _head_ids(MIX_W)
    zero = jnp.zeros((), q.dtype)
    return jnp.concatenate([jnp.where(ids == h, q, zero) for h in range(N_HEADS)], axis=0)


def _unstack_heads(o, n):
    ids = _head_ids(MIX_W)
    out = o[0:n]
    for h in range(1, N_HEADS):
        out = jnp.where(ids == h, o[h * n:(h + 1) * n], out)
    return out


NBR_UNROLL = 4


def _nbr_kernel(q_ref, k_ref, v_ref, bias_ref, o_ref, *, rows_per_step, grid_rows, kr):
    j = pl.program_id(1)
    win = kr * GRID_W

    def query_row(i):
        r = j * rows_per_step + i
        rs = jnp.clip(r - kr // 2, 0, grid_rows - kr)
        kstart = pl.multiple_of(rs * GRID_W, GRID_W)
        qstart = pl.multiple_of(i * GRID_W, GRID_W)
        qs = _stack_heads(q_ref[pl.ds(qstart, GRID_W), :])
        s = lax.dot_general(qs, k_ref[pl.ds(kstart, win), :], (((1,), (1,)), ((), ())),
                            preferred_element_type=F32)
        yield
        s = s + bias_ref[r - rs]
        m = jnp.max(s, axis=-1, keepdims=True)
        p = jnp.exp(s - m)
        l = jnp.sum(p, axis=-1, keepdims=True)
        o = jnp.dot(p.astype(BF16), v_ref[pl.ds(kstart, win), :], preferred_element_type=F32)
        yield
        o_ref[pl.ds(qstart, GRID_W), :] = _unstack_heads(o / l, GRID_W).astype(o_ref.dtype)

    def body(i, carry):
        _interleave(*[query_row(i * NBR_UNROLL + u) for u in range(NBR_UNROLL)])
        return carry

    lax.fori_loop(0, rows_per_step // NBR_UNROLL, body, 0)


def _nbr_bias(rpb, kr):
    c = np.arange(GRID_W)
    cs = np.clip(c - NA_COLS // 2, 0, GRID_W - NA_COLS)
    col_valid = (c[None, :] >= cs[:, None]) & (c[None, :] < cs[:, None] + NA_COLS)
    dc = np.clip(c[None, :] - c[:, None], -(NA_COLS - 1), NA_COLS - 1) + NA_COLS - 1
    dr = np.arange(kr)[None, :] - np.arange(kr)[:, None] + NA_ROWS - 1
    pick_r = (dr[..., None] == np.arange(2 * NA_ROWS - 1)).astype(np.float32)
    pick_c = (dc[..., None] == np.arange(2 * NA_COLS - 1)).astype(np.float32)
    bias = jnp.einsum('hab,vja,qkb->vhqjk', rpb.astype(F32), pick_r, pick_c, precision=lax.Precision.HIGHEST)
    bias = jnp.where(col_valid[None, None, :, None, :], bias, NEG_INF)
    return bias.reshape(kr, N_HEADS * GRID_W, kr * GRID_W)


def _nbr(dqkv, bias, batch, seq, rows_per_step):
    grid_rows = seq // GRID_W
    kr = min(NA_ROWS, grid_rows)
    nblk = grid_rows // rows_per_step
    tq = rows_per_step * GRID_W
    return pl.pallas_call(
        functools.partial(_nbr_kernel, rows_per_step=rows_per_step, grid_rows=grid_rows, kr=kr),
        grid=(batch, nblk),
        in_specs=[pl.BlockSpec((tq, MIX_W), lambda b, i: (b * nblk + i, 0)),
                  pl.BlockSpec((seq, MIX_W), lambda b, i: (b, 1)),
                  pl.BlockSpec((seq, MIX_W), lambda b, i: (b, 2)),
                  _const_spec(bias.shape)],
        out_specs=pl.BlockSpec((tq, MIX_W), lambda b, i: (b * nblk + i, 0)),
        out_shape=jax.ShapeDtypeStruct((batch * seq, MIX_W), BF16),
        compiler_params=_params("parallel", "arbitrary"),
    )(dqkv, dqkv, dqkv, bias)


BAND_UNROLL = 2


def _band_kernel(qkv_ref, o_ref, lse_ref, *, half, kwin, tq):
    length = qkv_ref.shape[0]
    rows = o_ref.shape[0]
    nq = rows // tq
    row0 = pl.program_id(1) * rows

    def unit(u):
        r, qi = u // nq, u % nq
        col = pl.multiple_of(r * (3 * MIX_W), LANES)
        ocol = pl.multiple_of(r * MIX_W, LANES)
        q_local = pl.multiple_of(qi * tq, tq)
        i0 = row0 + q_local
        kstart = pl.multiple_of(jnp.clip(i0 - half, 0, length - kwin), BF16_SUBLANES)
        qs = _stack_heads(qkv_ref[pl.ds(pl.multiple_of(i0, tq), tq), pl.ds(col, MIX_W)])
        s = lax.dot_general(qs, qkv_ref[pl.ds(kstart, kwin), pl.ds(col + MIX_W, MIX_W)],
                            (((1,), (1,)), ((), ())), preferred_element_type=F32)
        yield
        qpos = i0 + lax.broadcasted_iota(jnp.int32, (N_HEADS * tq, kwin), 0) % tq
        kpos = kstart + lax.broadcasted_iota(jnp.int32, (N_HEADS * tq, kwin), 1)
        s = jnp.where(jnp.abs(qpos - kpos) <= half, s, NEG_INF)
        m = jnp.max(s, axis=-1, keepdims=True)
        p = jnp.exp(s - m)
        l = jnp.sum(p, axis=-1, keepdims=True)
        o = jnp.dot(p.astype(BF16), qkv_ref[pl.ds(kstart, kwin), pl.ds(col + 2 * MIX_W, MIX_W)],
                    preferred_element_type=F32)
        yield
        o_ref[pl.ds(q_local, tq), pl.ds(ocol, MIX_W)] = _unstack_heads(o / l, tq).astype(o_ref.dtype)
        lse = m + jnp.log(l)
        lane = lax.broadcasted_iota(jnp.int32, (1, lse_ref.shape[1]), 1)
        packed = lse_ref[pl.ds(q_local, tq), :]
        for h in range(N_HEADS):
            packed = jnp.where(lane == r * N_HEADS + h, lse[h * tq:(h + 1) * tq], packed)
        lse_ref[pl.ds(q_local, tq), :] = packed

    lse_ref[...] = jnp.zeros(lse_ref.shape, F32)
    units = (o_ref.shape[1] // MIX_W) * nq

    def body(i, carry):
        _interleave(*[unit(i * BAND_UNROLL + j) for j in range(BAND_UNROLL)])
        return carry

    lax.fori_loop(0, units // BAND_UNROLL, body, 0)


def _band(a, batch, seq, window, dil, tq, step_rows):
    half = window // (2 * dil)
    length = seq // dil
    tq = min(tq, length)
    kwin = min(tq + 2 * half, length)
    rows = min(step_rows, length) if dil == 1 else length
    view = a.reshape(batch, length, dil * 3 * MIX_W)
    o, lse = pl.pallas_call(
        functools.partial(_band_kernel, half=half, kwin=kwin, tq=tq),
        grid=(batch, length // rows),
        in_specs=[pl.BlockSpec((None, length, dil * 3 * MIX_W), lambda b, i: (b, 0, 0))],
        out_specs=[pl.BlockSpec((None, rows, dil * MIX_W), lambda b, i: (b, i, 0)),
                   pl.BlockSpec((None, rows, dil * N_HEADS), lambda b, i: (b, i, 0))],
        out_shape=[jax.ShapeDtypeStruct((batch, length, dil * MIX_W), BF16),
                   jax.ShapeDtypeStruct((batch, length, dil * N_HEADS), F32)],
        compiler_params=_params("parallel", "arbitrary"),
    )(view)
    return o.reshape(batch * seq, MIX_W), lse.reshape(batch * seq, N_HEADS)


CUM_ROWS = 256


def _chunk_tri(n, upper):
    r = lax.broadcasted_iota(jnp.int32, (n, n), 0)
    c = lax.broadcasted_iota(jnp.int32, (n, n), 1)
    same = (r // B_CHUNK) == (c // B_CHUNK)
    keep = (r <= c) if upper else (r >= c)
    return jnp.where(same & keep, 1.0, 0.0).astype(F32)


def _delta_prep_kernel(x_ref, prev_ref, next_ref, ab_ref, conv_ref, alog_ref, dtb_ref,
                       qkv_ref, gate_ref, xe_ref):
    j = pl.program_id(1)
    rows = x_ref.shape[0]
    pad = BF16_SUBLANES
    keep_prev = jnp.where(j > 0, 1.0, 0.0)
    keep_next = jnp.where(j < pl.num_programs(1) - 1, 1.0, 0.0)
    xe_ref[0:pad, :] = prev_ref[...].astype(F32) * keep_prev
    xe_ref[pad:pad + rows, :] = x_ref[...].astype(F32)
    xe_ref[pad + rows:, :] = next_ref[...].astype(F32) * keep_next
    centre = B_CONV // 2
    y = jnp.zeros((rows, 3 * MIX_W), F32)
    for tap in range(B_CONV):
        y = y + xe_ref[pad + tap - centre:pad + tap - centre + rows, :] * conv_ref[tap:tap + 1, :]
    y = y * jax.nn.sigmoid(y)
    q, k = y[:, 0:MIX_W], y[:, MIX_W:2 * MIX_W]
    q = q * lax.rsqrt(_head_sums(q * q) + 1e-6) * (HEAD_DIM ** -0.5)
    k = k * lax.rsqrt(_head_sums(k * k) + 1e-6)
    qkv_ref[:, 0:MIX_W] = q.astype(BF16)
    qkv_ref[:, MIX_W:2 * MIX_W] = k.astype(BF16)
    qkv_ref[:, 2 * MIX_W:] = y[:, 2 * MIX_W:].astype(BF16)

    ab = ab_ref[...]
    col = lax.broadcasted_iota(jnp.int32, (1, ab.shape[-1]), 1)
    xs = ab + dtb_ref[...]
    softplus = jnp.maximum(xs, 0.0) + jnp.log(1.0 + jnp.exp(-jnp.abs(xs)))
    g = -jnp.exp(alog_ref[...]) * softplus
    beta = jax.nn.sigmoid(ab)
    tri_f, tri_b = _chunk_tri(CUM_ROWS, False), _chunk_tri(CUM_ROWS, True)
    for r0 in range(0, rows, CUM_ROWS):
        gp = g[r0:r0 + CUM_ROWS]
        cum_f = jnp.dot(tri_f, gp, preferred_element_type=F32, precision=lax.Precision.HIGHEST)
        cum_b = jnp.dot(tri_b, gp, preferred_element_type=F32, precision=lax.Precision.HIGHEST)
        gate_ref[r0:r0 + CUM_ROWS, :] = jnp.where(col < 8, beta[r0:r0 + CUM_ROWS],
                                                   jnp.where(col < 12, cum_f, cum_b))


def _delta_prep(bqkv, ab, lw, batch, seq, rows):
    t = batch * seq
    nblk = seq // rows
    per = rows // BF16_SUBLANES
    last = t // BF16_SUBLANES - 1
    w3 = 3 * MIX_W
    return pl.pallas_call(
        _delta_prep_kernel,
        grid=(batch, nblk),
        in_specs=[pl.BlockSpec((rows, w3), lambda b, j: (b * nblk + j, 0)),
                  pl.BlockSpec((BF16_SUBLANES, w3), lambda b, j: (jnp.maximum((b * nblk + j) * per - 1, 0), 0)),
                  pl.BlockSpec((BF16_SUBLANES, w3), lambda b, j: (jnp.minimum((b * nblk + j + 1) * per, last), 0)),
                  pl.BlockSpec((rows, 16), lambda b, j: (b * nblk + j, 0)),
                  _const_spec(lw["conv_w"].shape), _const_spec((1, 16)), _const_spec((1, 16))],
        out_specs=[pl.BlockSpec((rows, w3), lambda b, j: (b * nblk + j, 0)),
                   pl.BlockSpec((rows, 16), lambda b, j: (b * nblk + j, 0))],
        out_shape=[jax.ShapeDtypeStruct((t, w3), BF16), jax.ShapeDtypeStruct((t, 16), F32)],
        scratch_shapes=[pltpu.VMEM((rows + 2 * BF16_SUBLANES, w3), F32)],
        compiler_params=_params("parallel", "arbitrary"),
    )(bqkv, bqkv, bqkv, ab, lw["conv_w"], lw["a_log16"], lw["dt_bias16"])


def _expand_heads(cols):
    ids = _head_ids(MIX_W)
    out = jnp.broadcast_to(cols[:, 0:1], (cols.shape[0], MIX_W))
    for h in range(1, N_HEADS):
        out = jnp.where(ids == h, jnp.broadcast_to(cols[:, h:h + 1], out.shape), out)
    return out


def _block_diag(packed, mask):
    return jnp.where(mask, jnp.concatenate([packed] * N_HEADS, axis=0), 0.0).astype(BF16)


def _mm(a, b):
    return jnp.dot(a.astype(BF16), b.astype(BF16), preferred_element_type=F32)


def _delta_local(backward, qkv_ref, gate_ref, grow_ref, u_ref, w_ref, qh_ref, kt_ref, in_ref, eg_ref, c):
    n = B_CHUNK
    d = 1 if backward else 0
    r0 = pl.multiple_of(c * n, n)
    q = qkv_ref[pl.ds(r0, n), 0:MIX_W]
    k = qkv_ref[pl.ds(r0, n), MIX_W:2 * MIX_W]
    v = qkv_ref[pl.ds(r0, n), 2 * MIX_W:].astype(F32)
    gates = gate_ref[pl.ds(r0, n), :]
    d0 = 4 if backward else 0
    beta = _expand_heads(gates[:, d0:d0 + 4])
    gc = _expand_heads(gates[:, 8 + d0:12 + d0])
    gc_row = grow_ref[pl.ds(c, 1), d0 * n:(d0 + 4) * n]

    row = lax.broadcasted_iota(jnp.int32, (n, MIX_W), 0)
    col = lax.broadcasted_iota(jnp.int32, (n, MIX_W), 1) % n
    incl = (row <= col) if backward else (row >= col)
    strict = (row < col) if backward else (row > col)
    bd_mask = (lax.broadcasted_iota(jnp.int32, (MIX_W, MIX_W), 0) // n
               == lax.broadcasted_iota(jnp.int32, (MIX_W, MIX_W), 1) // n)
    decay = jnp.exp(jnp.where(incl, gc - gc_row, NEG_INF))

    kf = k.astype(F32)
    kq = lax.dot_general(jnp.concatenate([q, k], axis=0), _stack_heads(k), (((1,), (1,)), ((), ())),
                         preferred_element_type=F32)
    yield
    intra = kq[0:n] * decay
    low = jnp.where(strict, kq[n:] * beta * decay, 0.0)

    x = jnp.where(row == col, 1.0, 0.0) - low
    p = _mm(low, _block_diag(low, bd_mask))
    yield
    steps = int(math.log2(n)) - 1
    for step in range(steps):
        last = step == steps - 1
        lhs = x if last else jnp.concatenate([x, p], axis=0)
        r = _mm(lhs, _block_diag(p, bd_mask))
        yield
        x = x + r[0:n]
        if not last:
            p = r[n:]

    kbeta = kf * beta
    u_ref[d, pl.ds(r0, n), :] = _mm(x, _block_diag(v * beta, bd_mask))
    w_ref[d, pl.ds(r0, n), :] = _mm(x, _block_diag(kbeta * jnp.exp(gc), bd_mask)).astype(BF16)
    edge = gc[0:1] if backward else gc[n - 1:n]
    qh_ref[d, pl.ds(r0, n), :] = (q.astype(F32) * jnp.exp(gc)).astype(BF16)
    kt_ref[d, pl.ds(r0, n), :] = (kf * jnp.exp(edge - gc)).astype(BF16)
    in_ref[d, pl.ds(r0, n), :] = intra.astype(BF16)
    eg_ref[d, pl.ds(c, 1), :] = jnp.exp(edge)


def _delta_step(backward, u_ref, w_ref, qh_ref, kt_ref, in_ref, eg_ref, o_ref, s_ref, c):
    n = B_CHUNK
    d = 1 if backward else 0
    r0 = pl.multiple_of(c * n, n)
    bd_mask = (lax.broadcasted_iota(jnp.int32, (MIX_W, MIX_W), 0) // n
               == lax.broadcasted_iota(jnp.int32, (MIX_W, MIX_W), 1) // n)
    state = s_ref[d]
    lhs = jnp.concatenate([w_ref[d, pl.ds(r0, n), :], qh_ref[d, pl.ds(r0, n), :]], axis=0)
    ws_qs = jnp.dot(lhs, state.astype(BF16), preferred_element_type=F32)
    yield
    v_new = u_ref[d, pl.ds(r0, n), :] - ws_qs[0:n]
    o_ref[pl.ds(r0, n), :] = ws_qs[n:] + jnp.dot(in_ref[d, pl.ds(r0, n), :], _block_diag(v_new, bd_mask),
                                                  preferred_element_type=F32)
    kv = lax.dot_general(kt_ref[d, pl.ds(r0, n), :], v_new.astype(BF16), (((0,), (0,)), ((), ())),
                         preferred_element_type=F32)
    s_ref[d] = state * eg_ref[d, pl.ds(c, 1), :] + jnp.where(bd_mask, kv, 0.0)


LOCAL_UNROLL = 8


def _delta_kernel(qkv_f, gate_f, grow_f, qkv_b, gate_b, grow_b, of_ref, ob_ref,
                  s_ref, u_ref, w_ref, qh_ref, kt_ref, in_ref, eg_ref):
    @pl.when(pl.program_id(1) == 0)
    def _():
        s_ref[...] = jnp.zeros(s_ref.shape, F32)

    nchunk = qkv_f.shape[0] // B_CHUNK
    staged = (u_ref, w_ref, qh_ref, kt_ref, in_ref, eg_ref)

    def local(i, carry):
        chunks = [i * LOCAL_UNROLL + g for g in range(LOCAL_UNROLL)]
        _interleave(*[_delta_local(False, qkv_f, gate_f, grow_f, *staged, c) for c in chunks],
                    *[_delta_local(True, qkv_b, gate_b, grow_b, *staged, c) for c in chunks])
        return carry

    lax.fori_loop(0, nchunk // LOCAL_UNROLL, local, 0)

    def step(i, carry):
        _interleave(_delta_step(False, *staged, of_ref, s_ref, i),
                    _delta_step(True, *staged, ob_ref, s_ref, nchunk - 1 - i))
        return carry

    lax.fori_loop(0, nchunk, step, 0)


def _delta(qkvn, gates, batch, seq, rows):
    t = batch * seq
    nblk = seq // rows
    nchunk = rows // B_CHUNK
    grow = gates[:, 8:16].reshape(t // B_CHUNK, B_CHUNK, 8).transpose(0, 2, 1).reshape(t // B_CHUNK, 8 * B_CHUNK)
    fwd = lambda b, j: (b * nblk + j, 0)
    bwd = lambda b, j: (b * nblk + nblk - 1 - j, 0)
    specs = lambda m: [pl.BlockSpec((rows, 3 * MIX_W), m), pl.BlockSpec((rows, 16), m),
                       pl.BlockSpec((nchunk, 8 * B_CHUNK), m)]
    return pl.pallas_call(
        _delta_kernel,
        grid=(batch, nblk),
        in_specs=specs(fwd) + specs(bwd),
        out_specs=[pl.BlockSpec((rows, MIX_W), fwd), pl.BlockSpec((rows, MIX_W), bwd)],
        out_shape=[jax.ShapeDtypeStruct((t, MIX_W), F32)] * 2,
        scratch_shapes=([pltpu.VMEM((2, MIX_W, MIX_W), F32), pltpu.VMEM((2, rows, MIX_W), F32)]
                        + [pltpu.VMEM((2, rows, MIX_W), BF16)] * 4 + [pltpu.VMEM((2, nchunk, MIX_W), F32)]),
        compiler_params=_params("parallel", "arbitrary"),
    )(qkvn, gates, grow, qkvn, gates, grow)


def _merge_kernel(x_ref, gain_ref, oa0, oa1, oa2, la0, la1, la2, bf_ref, bb_ref, z_ref, bgain_ref,
                  oc_ref, od_ref, wg_ref, bg_ref, wbr_ref, wout_ref, out_ref):
    x = x_ref[...]
    h = _rms_rows(x, gain_ref[...]).astype(BF16)

    lses = [_expand_heads(r[...]) for r in (la0, la1, la2)]
    top = jnp.maximum(jnp.maximum(lses[0], lses[1]), lses[2])
    ws = [jnp.exp(l - top) for l in lses]
    o_a = sum(w * r[...].astype(F32) for w, r in zip(ws, (oa0, oa1, oa2))) / (ws[0] + ws[1] + ws[2])

    o_b = bf_ref[...] + bb_ref[...]
    z = z_ref[...].astype(F32)
    o_b = (o_b * lax.rsqrt(_head_sums(o_b * o_b) * (1.0 / HEAD_DIM) + RMS_EPS) * bgain_ref[...]
           * (z * jax.nn.sigmoid(z)))

    branches = (o_a, o_b, oc_ref[...], od_ref[...])
    y = jnp.zeros(x.shape, F32)
    for i, o in enumerate(branches):
        gate = jax.nn.sigmoid(jnp.dot(h, wg_ref[i], preferred_element_type=F32) + bg_ref[i])
        y = y + gate * jnp.dot(o.astype(BF16), wbr_ref[i], preferred_element_type=F32)
    out_ref[...] = x + jnp.dot(y.astype(BF16), wout_ref[...], preferred_element_type=F32)


def _merge(x2d, a_outs, b_outs, z, oc, od, lw, tm):
    t = x2d.shape[0]
    tok = lambda w: pl.BlockSpec((tm, w), lambda i: (i, 0))
    (o0, l0), (o1, l1), (o2, l2) = a_outs
    acts = (o0, o1, o2, l0, l1, l2, b_outs[0], b_outs[1], z)
    weights = (lw["wg"], lw["bg"], lw["wbr"], lw["wout"])
    return pl.pallas_call(
        _merge_kernel,
        grid=(t // tm,),
        in_specs=([tok(D_MODEL), _const_spec(lw["norm_mix"].shape)] + [tok(a.shape[1]) for a in acts]
                  + [_const_spec(lw["b_gain"].shape), tok(MIX_W), tok(MIX_W)]
                  + [_const_spec(w.shape) for w in weights]),
        out_specs=tok(D_MODEL),
        out_shape=jax.ShapeDtypeStruct((t, D_MODEL), F32),
        compiler_params=_params("parallel"),
    )(x2d, lw["norm_mix"], *acts, lw["b_gain"], oc, od, *weights)


EXPERT_UNROLL = 2


def _moe_kernel(x_ref, gain_ref, wr_ref, br_ref, wgt_ref, wup_ref, wdn_ref, fgain_ref, out_ref,
                hs_ref, ws_ref, ys_ref, *, final, cap):
    x = x_ref[...]
    tm = x.shape[0]
    h = _rms_rows(x, gain_ref[...])
    hb = h.astype(BF16)
    h_lo = (h - hb.astype(F32)).astype(BF16)
    parts = jnp.dot(jnp.concatenate([hb, h_lo], axis=0), wr_ref[...], preferred_element_type=F32)
    logits = (parts[:tm, :LANES] + parts[:tm, LANES:]) + (parts[tm:, :LANES] + parts[tm:, LANES:]) + br_ref[...]
    lane = lax.broadcasted_iota(jnp.int32, (1, LANES), 1)
    none = LANES

    def first_max(vals):
        top = jnp.max(vals, axis=-1, keepdims=True)
        return top, jnp.min(jnp.where(vals == top, lane, none), axis=-1, keepdims=True)

    gl = jnp.where(lane < N_GROUPS, logits, NEG_INF)
    g_top, g_idx = first_max(gl)
    g_prob = 1.0 / jnp.sum(jnp.exp(gl - g_top), axis=-1, keepdims=True)
    in_group = (lane >= N_GROUPS) & (lane < N_GROUPS + N_EXPERTS) & ((lane - N_GROUPS) // EXPERTS_PER_GROUP == g_idx)
    el = jnp.where(in_group, logits, NEG_INF)
    top1, idx1 = first_max(el)
    top2, idx2 = first_max(jnp.where(lane == idx1, NEG_INF, el))
    e2 = jnp.exp(top2 - top1)
    w1 = g_prob / (1.0 + e2)
    weights = jnp.where(lane == idx1, w1, 0.0) + jnp.where(lane == idx2, w1 * e2, 0.0)

    def expert(rows, row_weights, e, outs):
        w_e = jnp.sum(jnp.where(lane == e + N_GROUPS, row_weights, 0.0), axis=-1, keepdims=True)
        gate = jnp.dot(rows, wgt_ref[e], preferred_element_type=F32)
        up = jnp.dot(rows, wup_ref[e], preferred_element_type=F32)
        yield
        hid = gate * jax.nn.sigmoid(gate) * up * w_e
        outs.append(jnp.dot(hid.astype(BF16), wdn_ref[e], preferred_element_type=F32))

    def finish(moe_out):
        y = x + moe_out
        if final:
            y = _rms_rows(y, fgain_ref[...])
        out_ref[...] = y

    member = jnp.where(lane == g_idx, 1.0, 0.0)
    fits = jnp.max(jnp.sum(member, axis=0, keepdims=True)) <= cap

    @pl.when(fits)
    def _():
        before = lax.broadcasted_iota(jnp.int32, (tm, tm), 0) > lax.broadcasted_iota(jnp.int32, (tm, tm), 1)
        ahead = jnp.dot(jnp.where(before, 1.0, 0.0).astype(BF16), member.astype(BF16),
                        preferred_element_type=F32)
        pos = jnp.sum(member * ahead, axis=-1, keepdims=True).astype(jnp.int32)
        slot = g_idx * cap + pos
        perm_t = jnp.where(lax.broadcasted_iota(jnp.int32, (1, N_GROUPS * cap), 1) == slot, 1.0, 0.0).astype(BF16)
        slot_row = jnp.broadcast_to(slot.astype(F32), (tm, LANES)).T[0:1]
        perm = jnp.where(lax.broadcasted_iota(jnp.int32, (N_GROUPS * cap, 1), 0).astype(F32) == slot_row,
                         1.0, 0.0).astype(BF16)
        gather = lambda a: jnp.dot(perm, a, preferred_element_type=F32)
        hs_ref[...] = gather(hb).astype(BF16)
        w_hi = weights.astype(BF16)
        w_parts = gather(jnp.concatenate([w_hi, (weights - w_hi.astype(F32)).astype(BF16)], axis=1))
        ws_ref[...] = w_parts[:, :LANES] + w_parts[:, LANES:]

        def group(g, carry):
            r0 = pl.multiple_of(g * cap, cap)
            rows, row_weights = hs_ref[pl.ds(r0, cap), :], ws_ref[pl.ds(r0, cap), :]
            outs = []
            _interleave(*[expert(rows, row_weights, g * EXPERTS_PER_GROUP + j, outs)
                          for j in range(EXPERTS_PER_GROUP)])
            ys_ref[pl.ds(r0, cap), :] = sum(outs[1:], outs[0]).astype(BF16)
            return carry

        lax.fori_loop(0, N_GROUPS, group, 0)
        finish(jnp.dot(perm_t, ys_ref[...], preferred_element_type=F32))

    @pl.when(jnp.logical_not(fits))
    def _():
        def experts(i, acc):
            outs = []
            _interleave(*[expert(hb, weights, i * EXPERT_UNROLL + j, outs) for j in range(EXPERT_UNROLL)])
            return acc + sum(outs[1:], outs[0])

        finish(lax.fori_loop(0, N_EXPERTS // EXPERT_UNROLL, experts, jnp.zeros(x.shape, F32)))


def _moe(x2d, lw, final_gain, final, tm, cap):
    t = x2d.shape[0]
    tok = pl.BlockSpec((tm, D_MODEL), lambda i: (i, 0))
    once = lambda a: pl.BlockSpec(a.shape, lambda i: (0,) * a.ndim, pipeline_mode=pl.Buffered(1))
    weights = (lw["norm_ffn"], lw["w_route"], lw["b_route"], lw["w_ff_gate"], lw["w_ff_up"], lw["w_ff_down"],
               final_gain)
    return pl.pallas_call(
        functools.partial(_moe_kernel, final=final, cap=cap),
        grid=(t // tm,),
        in_specs=[tok] + [once(w) for w in weights],
        out_specs=tok,
        out_shape=jax.ShapeDtypeStruct((t, D_MODEL), F32),
        scratch_shapes=[pltpu.VMEM((N_GROUPS * cap, D_MODEL), BF16), pltpu.VMEM((N_GROUPS * cap, LANES), F32),
                        pltpu.VMEM((N_GROUPS * cap, D_MODEL), BF16)],
        compiler_params=_params("parallel"),
    )(x2d, *weights)


def _split_bf16(w):
    hi = w.astype(BF16)
    return jnp.concatenate([hi, (w - hi.astype(F32)).astype(BF16)], axis=1)


def _layer_weights(p, l):
    w_in = p["w_in"][l]
    offs = np.cumsum([0, 3 * MIX_W, 3 * MIX_W, MIX_W, 16, MIX_W, 2 * C_KV_HEADS * HEAD_DIM, 3 * MIX_W])
    sec = [w_in[:, offs[i]:offs[i + 1]] for i in range(7)]
    c_order = np.concatenate([np.arange(h * HEAD_DIM, (h + 1) * HEAD_DIM) for h in (0, 2, 1, 3)])
    row = lambda v: v.reshape(1, -1).astype(F32)
    gate16 = lambda v: jnp.concatenate([jnp.zeros((8,), F32), v.reshape(-1).astype(F32)]).reshape(1, 16)
    wbr = p["w_branch"][l]
    wbr = jnp.stack([wbr[0], wbr[1], wbr[2][c_order], wbr[3]])
    w_route = jnp.concatenate([p["w_route_group"][l], p["w_route_expert"][l].reshape(D_MODEL, N_EXPERTS)], axis=1)
    b_route = jnp.concatenate([p["b_route_group"][l], p["b_route_expert"][l].reshape(N_EXPERTS)])
    pad_lanes = lambda a: jnp.pad(a, [(0, 0)] * (a.ndim - 1) + [(0, LANES - a.shape[-1])])
    return dict(
        norm_mix=row(p["norm_mix"][l]),
        wa=sec[0].astype(BF16), wb=sec[1].astype(BF16), wz=sec[2].astype(BF16),
        wab=pad_lanes(sec[3]).astype(BF16), wcq=sec[4][:, c_order].astype(BF16), wck=sec[5][:, :C_KV_HEADS * HEAD_DIM].astype(BF16), wcvt=sec[5][:, C_KV_HEADS * HEAD_DIM:].T.astype(BF16),
        wd=sec[6].astype(BF16),
        cq_gain=row(jnp.tile(p["c_q_norm"][l], N_HEADS)), ck_gain=row(jnp.tile(p["c_k_norm"][l], C_KV_HEADS)),
        conv_w=p["conv_w"][l].astype(F32), a_log16=gate16(p["a_log"][l]), dt_bias16=gate16(p["dt_bias"][l]),
        b_gain=row(jnp.tile(p["b_norm"][l], N_HEADS)),
        wg=p["w_gate"][l].astype(BF16), bg=p["b_gate"][l].reshape(N_HEADS, 1, D_MODEL).astype(F32),
        wbr=wbr.astype(BF16), wout=p["w_out"][l].astype(BF16),
        norm_ffn=row(p["norm_ffn"][l]),
        w_route=_split_bf16(pad_lanes(w_route).astype(F32)), b_route=pad_lanes(b_route.reshape(1, -1)).astype(F32),
        w_ff_gate=p["w_ff_gate"][l].reshape(N_EXPERTS, D_MODEL, D_EXPERT).astype(BF16),
        w_ff_up=p["w_ff_up"][l].reshape(N_EXPERTS, D_MODEL, D_EXPERT).astype(BF16),
        w_ff_down=p["w_ff_down"][l].reshape(N_EXPERTS, D_EXPERT, D_MODEL).astype(BF16),
        d_rpb=p["d_rpb"][l],
    )


def _tiles(seq):
    tm = min(512, seq)
    share = tm // N_GROUPS
    moe_cap = (share + share // 4 + BF16_SUBLANES - 1) // BF16_SUBLANES * BF16_SUBLANES
    return dict(tm=tm, moe_cap=moe_cap, tq_c=min(256, seq), tk_c=min(512, seq), rows_d=min(8, seq // GRID_W),
                tq_a=128, rows_a=2048, rows_prep=min(512, seq), rows_delta=min(2048, seq))


def _encoder(x, layers, final_gain):
    batch, seq, _ = x.shape
    ts = _tiles(seq)
    tables = _rotary_tables(seq)
    depth = len(layers)
    x2d = x.reshape(batch * seq, D_MODEL)
    for l, lw in enumerate(layers):
        a, bqkv, z, ab, cq, ck, cvt, dqkv = _in_proj(x2d, seq, lw, tables, ts["tm"])
        a_outs = [_band(a, batch, seq, window, dil, ts["tq_a"], ts["rows_a"]) for window, dil in A_PATTERNS]
        qkvn, gates = _delta_prep(bqkv, ab, lw, batch, seq, ts["rows_prep"])
        b_outs = _delta(qkvn, gates, batch, seq, ts["rows_delta"])
        oc = _gqa(cq, ck, cvt, batch, seq, ts["tq_c"], ts["tk_c"])
        od = _nbr(dqkv, _nbr_bias(lw["d_rpb"], min(NA_ROWS, seq // GRID_W)), batch, seq, ts["rows_d"])
        x2d = _merge(x2d, a_outs, b_outs, z, oc, od, lw, ts["tm"])
        x2d = _moe(x2d, lw, final_gain, l == depth - 1, ts["tm"], ts["moe_cap"])
    return x2d.reshape(batch, seq, D_MODEL)


def kernel(x_prompt, x_sample, norm_mix, w_in, conv_w, a_log, dt_bias, b_norm, c_q_norm, c_k_norm, d_rpb,
           w_gate, b_gate, w_branch, w_out, norm_ffn, w_route_group, b_route_group, w_route_expert,
           b_route_expert, w_ff_gate, w_ff_up, w_ff_down, norm_final):
    p = dict(norm_mix=norm_mix, w_in=w_in, conv_w=conv_w, a_log=a_log, dt_bias=dt_bias, b_norm=b_norm,
             c_q_norm=c_q_norm, c_k_norm=c_k_norm, d_rpb=d_rpb, w_gate=w_gate, b_gate=b_gate,
             w_branch=w_branch, w_out=w_out, norm_ffn=norm_ffn, w_route_group=w_route_group,
             b_route_group=b_route_group, w_route_expert=w_route_expert, b_route_expert=b_route_expert,
             w_ff_gate=w_ff_gate, w_ff_up=w_ff_up, w_ff_down=w_ff_down)
    layers = [_layer_weights(p, l) for l in range(norm_mix.shape[0])]
    final_gain = norm_final.reshape(1, -1).astype(F32)
    return _encoder(x_prompt, layers, final_gain), _encoder(x_sample, layers, final_gain)
```

```python
import functools
import math

import jax
import jax.numpy as jnp
import numpy as np
from jax import lax
from jax.experimental import pallas as pl
from jax.experimental.pallas import tpu as pltpu

D_MODEL = 1024
HEAD_DIM = 64
N_HEADS = 4
MIX_W = N_HEADS * HEAD_DIM
GRID_W = 64
RMS_EPS = 1e-6
NEG_INF = -1e30
LOG2_E = math.log2(math.e)

A_PATTERNS = ((128, 1), (512, 4), (2048, 16))
ROPE_THETA = 500000.0
ROPE_DIMS = HEAD_DIM // 4
AXIAL_THETA = 10000.0
C_KV_HEADS = 2
B_CONV = 5
B_CHUNK = 64
NA_ROWS = 8
NA_COLS = 16
N_GROUPS = 4
EXPERTS_PER_GROUP = 4
N_EXPERTS = N_GROUPS * EXPERTS_PER_GROUP
D_EXPERT = 256

LANES = 128
BF16_SUBLANES = 16
VMEM_LIMIT = 56 * 1024 * 1024

F32 = jnp.float32
BF16 = jnp.bfloat16


def _params(*sem):
    return pltpu.CompilerParams(dimension_semantics=sem, vmem_limit_bytes=VMEM_LIMIT)


def _const_spec(shape):
    n = len(shape)
    return pl.BlockSpec(shape, lambda *_: (0,) * n)


def _rms_rows(x, gain):
    return x * lax.rsqrt(jnp.mean(x * x, axis=-1, keepdims=True) + RMS_EPS) * gain


def _head_ids(width):
    return lax.broadcasted_iota(jnp.int32, (1, width), 1) // HEAD_DIM


def _block_ones(width):
    r = lax.broadcasted_iota(jnp.int32, (width, width), 0) // HEAD_DIM
    c = lax.broadcasted_iota(jnp.int32, (width, width), 1) // HEAD_DIM
    return jnp.where(r == c, 1.0, 0.0).astype(BF16)


def _head_sums(x):
    return jnp.dot(x.astype(BF16), _block_ones(x.shape[-1]), preferred_element_type=F32)


def _interleave(*chains):
    live = list(chains)
    while live:
        live = [c for c in live if next(c, live) is not live]


def _rotate(x, cos, sin_lo, sin_hi, shift):
    w = x.shape[-1]
    return x * cos + pltpu.roll(x, shift, 1) * sin_hi + pltpu.roll(x, w - shift, 1) * sin_lo


def _in_proj_kernel(x_ref, gain_ref, wa_ref, wb_ref, wz_ref, wab_ref, wcq_ref, wck_ref, wcvt_ref, wd_ref,
                    cqg_ref, ckg_ref, rot_a_ref, rot_c_ref,
                    a1_ref, a4_ref, a16_ref, b_ref, z_ref, ab_ref, cq_ref, ck_ref, cvt_ref, d_ref, a_scr):
    h = _rms_rows(x_ref[...], gain_ref[...]).astype(BF16)
    dot = lambda w_ref: jnp.dot(h, w_ref[...], preferred_element_type=F32)
    scale = HEAD_DIM ** -0.5
    tm = x_ref.shape[0]

    ua = dot(wa_ref)
    cos, s_lo, s_hi = rot_a_ref[0], rot_a_ref[1], rot_a_ref[2]
    half = ROPE_DIMS // 2
    qkv_a = (_rotate(ua[:, 0:MIX_W], cos, s_lo, s_hi, half) * scale,
             _rotate(ua[:, MIX_W:2 * MIX_W], cos, s_lo, s_hi, half), ua[:, 2 * MIX_W:])
    a1_ref[...] = jnp.concatenate(qkv_a, axis=1).astype(BF16)
    slabs = 3 * MIX_W // LANES
    for c in range(slabs):
        part, off = divmod(c * LANES, MIX_W)
        a_scr[c] = qkv_a[part][:, off:off + LANES]
    for (_, dil), a_ref in zip(A_PATTERNS[1:], (a4_ref, a16_ref)):
        for r in range(dil):
            for c in range(slabs):
                col = r * 3 * MIX_W + c * LANES
                a_ref[:, col:col + LANES] = a_scr[c, pl.ds(r, tm // dil, stride=dil), :].astype(BF16)

    b_ref[...] = dot(wb_ref).astype(BF16)
    z_ref[...] = dot(wz_ref).astype(BF16)
    ab_ref[...] = dot(wab_ref)[:, :ab_ref.shape[-1]]

    cos, s_lo, s_hi = rot_c_ref[0], rot_c_ref[1], rot_c_ref[2]
    quarter = HEAD_DIM // 4
    uq = dot(wcq_ref)
    uq = uq * lax.rsqrt(_head_sums(uq * uq) * (1.0 / HEAD_DIM) + RMS_EPS) * cqg_ref[...]
    cq_ref[...] = (_rotate(uq, cos, s_lo, s_hi, quarter) * (scale * LOG2_E)).astype(BF16)
    kw = C_KV_HEADS * HEAD_DIM
    uk = dot(wck_ref)
    uk = uk * lax.rsqrt(_head_sums(uk * uk) * (1.0 / HEAD_DIM) + RMS_EPS) * ckg_ref[...]
    ck_ref[...] = _rotate(uk, cos[:, :kw], s_lo[:, :kw], s_hi[:, :kw], quarter).astype(BF16)
    cvt_ref[...] = lax.dot_general(wcvt_ref[...], h, (((1,), (1,)), ((), ())),
                                   preferred_element_type=F32).astype(BF16)

    ud = dot(wd_ref)
    d_ref[:, 0:MIX_W] = (ud[:, 0:MIX_W] * scale).astype(BF16)
    d_ref[:, MIX_W:] = ud[:, MIX_W:].astype(BF16)


def _rotary_tables(seq):
    pos = np.arange(seq, dtype=np.float64)

    def table(pos_of_lane, theta, half, lane_in_group, active):
        freq_idx = lane_in_group % half
        inv = theta ** (-(freq_idx.astype(np.float64)) / half)
        ang = pos_of_lane * inv[None, :]
        cos = np.where(active[None, :], np.cos(ang), 1.0)
        sin = np.where(active[None, :], np.sin(ang), 0.0)
        low = (lane_in_group < half)[None, :]
        return np.stack([cos, np.where(low, -sin, 0.0), np.where(low, 0.0, sin)]).astype(np.float32)

    lane = np.arange(MIX_W) % HEAD_DIM
    rot_a = table(np.broadcast_to(pos[:, None], (seq, MIX_W)), ROPE_THETA, ROPE_DIMS // 2, lane % ROPE_DIMS,
                  lane < ROPE_DIMS)
    rows, cols = pos // GRID_W, pos % GRID_W
    half = HEAD_DIM // 2
    pos_c = np.where((lane < half)[None, :], rows[:, None], cols[:, None])
    rot_c = table(pos_c, AXIAL_THETA, half // 2, lane % half, np.ones(MIX_W, bool))
    return jnp.asarray(rot_a), jnp.asarray(rot_c)


def _in_proj(x2d, seq, lw, tables, tm):
    t = x2d.shape[0]
    rot_a, rot_c = tables
    nblk = seq // tm
    tok = lambda w: pl.BlockSpec((tm, w), lambda i: (i, 0))
    rot = pl.BlockSpec((3, tm, MIX_W), lambda i: (0, i % nblk, 0))
    weights = (lw["norm_mix"], lw["wa"], lw["wb"], lw["wz"], lw["wab"], lw["wcq"], lw["wck"], lw["wcvt"], lw["wd"],
               lw["cq_gain"], lw["ck_gain"])
    kw = C_KV_HEADS * HEAD_DIM
    outs = ((3 * MIX_W, BF16), (MIX_W, BF16), (16, F32), (MIX_W, BF16), (kw, BF16), None, (3 * MIX_W, BF16))
    out_specs = [pl.BlockSpec((kw, tm), lambda i: (0, i)) if o is None else tok(o[0]) for o in outs]
    out_shape = [jax.ShapeDtypeStruct((kw, t), BF16) if o is None else jax.ShapeDtypeStruct((t, o[0]), o[1])
                 for o in outs]
    a_specs = [pl.BlockSpec((tm // dil, dil * 3 * MIX_W), lambda i: (i, 0)) for _, dil in A_PATTERNS]
    a_shape = [jax.ShapeDtypeStruct((t // dil, dil * 3 * MIX_W), BF16) for _, dil in A_PATTERNS]
    res = pl.pallas_call(
        _in_proj_kernel,
        grid=(t // tm,),
        in_specs=[tok(D_MODEL)] + [_const_spec(w.shape) for w in weights] + [rot, rot],
        out_specs=a_specs + out_specs,
        out_shape=a_shape + out_shape,
        scratch_shapes=[pltpu.VMEM((3 * MIX_W // LANES, tm, LANES), F32)],
        compiler_params=_params("parallel"),
    )(x2d, *weights, rot_a, rot_c)
    return res[:len(A_PATTERNS)], res[len(A_PATTERNS):]


def _gqa_kernel(q_ref, k_ref, vt_ref, o_ref, acc_ref, s_ref, *, tk):
    tq = q_ref.shape[0]
    seq = k_ref.shape[0]
    kw = C_KV_HEADS * HEAD_DIM
    qt = q_ref[...].astype(F32).T
    row_g = lax.broadcasted_iota(jnp.int32, (kw, 1), 0) // HEAD_DIM
    qs = jnp.concatenate([jnp.where(row_g == g, qt[blk * kw:(blk + 1) * kw], 0.0)
                          for g in range(C_KV_HEADS) for blk in range(2)], axis=1).astype(BF16)
    acc_ref[...] = jnp.zeros(acc_ref.shape, F32)

    width = 2 * tq

    nchunk = seq // tk

    def scores(j, slot):
        start = pl.multiple_of(j * tk, tk)
        s_ref[slot] = jnp.dot(k_ref[pl.ds(start, tk), :], qs, preferred_element_type=F32)

    def kv_head(g, j, slot, m_old, l_old, out):
        cols = slice(g * width, (g + 1) * width)
        start = pl.multiple_of(j * tk, tk)
        s = s_ref[slot, :, cols]
        m_new = jnp.maximum(m_old, jnp.max(s, axis=0, keepdims=True))
        alpha = jnp.exp2(m_old - m_new)
        p = jnp.exp2(s - m_new)
        pv = jnp.dot(vt_ref[g * HEAD_DIM:(g + 1) * HEAD_DIM, pl.ds(start, tk)], p.astype(BF16),
                     preferred_element_type=F32)
        yield
        acc_ref[:, cols] = acc_ref[:, cols] * alpha + pv
        out[g] = (m_new, l_old * alpha + jnp.sum(p, axis=0, keepdims=True))

    def softmax_chunk(j, slot, carry):
        out = [None] * C_KV_HEADS
        _interleave(*[kv_head(g, j, slot, *carry[g], out) for g in range(C_KV_HEADS)])
        return tuple(out)

    def chunk_pair(i, carry, last):
        scores(2 * i + 1, 1)
        carry = softmax_chunk(2 * i, 0, carry)
        if not last:
            scores(2 * i + 2, 0)
        return softmax_chunk(2 * i + 1, 1, carry)

    scores(0, 0)
    carry = ((jnp.full((1, width), NEG_INF, F32), jnp.zeros((1, width), F32)),) * C_KV_HEADS
    if nchunk == 1:
        carry = softmax_chunk(0, 0, carry)
    else:
        assert nchunk % 2 == 0
        carry = lax.fori_loop(0, nchunk // 2 - 1, lambda i, c: chunk_pair(i, c, False), carry)
        carry = chunk_pair(nchunk // 2 - 1, carry, True)
    o = acc_ref[...] / jnp.concatenate([l for _, l in carry], axis=1)
    for blk in range(2):
        pair = jnp.concatenate([o[:, (2 * g + blk) * tq:(2 * g + blk + 1) * tq] for g in range(C_KV_HEADS)], axis=0)
        o_ref[:, blk * kw:(blk + 1) * kw] = pair.T.astype(o_ref.dtype)


def _gqa(cq, ck, cvt, batch, seq, tq, tk):
    nq = seq // tq
    kw = C_KV_HEADS * HEAD_DIM
    return pl.pallas_call(
        functools.partial(_gqa_kernel, tk=tk),
        grid=(batch, nq),
        in_specs=[pl.BlockSpec((tq, MIX_W), lambda b, i: (b * nq + i, 0)),
                  pl.BlockSpec((seq, kw), lambda b, i: (b, 0)),
                  pl.BlockSpec((kw, seq), lambda b, i: (0, b))],
        out_specs=pl.BlockSpec((tq, MIX_W), lambda b, i: (b * nq + i, 0)),
        out_shape=jax.ShapeDtypeStruct((batch * seq, MIX_W), BF16),
        scratch_shapes=[pltpu.VMEM((HEAD_DIM, 4 * tq), F32), pltpu.VMEM((2, tk, 4 * tq), F32)],
        compiler_params=_params("parallel", "arbitrary"),
    )(cq, ck, cvt)


def _stack_heads(q):
    ids = _head_ids(MIX_W)
    zero = jnp.zeros((), q.dtype)
    return jnp.concatenate([jnp.where(ids == h, q, zero) for h in range(N_HEADS)], axis=0)


def _unstack_heads(o, n):
    ids = _head_ids(MIX_W)
    out = o[0:n]
    for h in range(1, N_HEADS):
        out = jnp.where(ids == h, o[h * n:(h + 1) * n], out)
    return out


NBR_UNROLL = 4


def _nbr_kernel(q_ref, k_ref, v_ref, bias_ref, o_ref, *, rows_per_step, grid_rows, kr):
    j = pl.program_id(1)
    win = kr * GRID_W

    def query_row(i):
        r = j * rows_per_step + i
        rs = jnp.clip(r - kr // 2, 0, grid_rows - kr)
        kstart = pl.multiple_of(rs * GRID_W, GRID_W)
        qstart = pl.multiple_of(i * GRID_W, GRID_W)
        qs = _stack_heads(q_ref[pl.ds(qstart, GRID_W), :])
        s = lax.dot_general(qs, k_ref[pl.ds(kstart, win), :], (((1,), (1,)), ((), ())),
                            preferred_element_type=F32)
        yield
        s = s + bias_ref[r - rs]
        m = jnp.max(s, axis=-1, keepdims=True)
        p = jnp.exp(s - m)
        l = jnp.sum(p, axis=-1, keepdims=True)
        o = jnp.dot(p.astype(BF16), v_ref[pl.ds(kstart, win), :], preferred_element_type=F32)
        yield
        o_ref[pl.ds(qstart, GRID_W), :] = _unstack_heads(o / l, GRID_W).astype(o_ref.dtype)

    def body(i, carry):
        _interleave(*[query_row(i * NBR_UNROLL + u) for u in range(NBR_UNROLL)])
        return carry

    lax.fori_loop(0, rows_per_step // NBR_UNROLL, body, 0)


def _nbr_bias(rpb, kr):
    c = np.arange(GRID_W)
    cs = np.clip(c - NA_COLS // 2, 0, GRID_W - NA_COLS)
    col_valid = (c[None, :] >= cs[:, None]) & (c[None, :] < cs[:, None] + NA_COLS)
    dc = np.clip(c[None, :] - c[:, None], -(NA_COLS - 1), NA_COLS - 1) + NA_COLS - 1
    dr = np.arange(kr)[None, :] - np.arange(kr)[:, None] + NA_ROWS - 1
    pick_r = (dr[..., None] == np.arange(2 * NA_ROWS - 1)).astype(np.float32)
    pick_c = (dc[..., None] == np.arange(2 * NA_COLS - 1)).astype(np.float32)
    bias = jnp.einsum('hab,vja,qkb->vhqjk', rpb.astype(F32), pick_r, pick_c, precision=lax.Precision.HIGHEST)
    bias = jnp.where(col_valid[None, None, :, None, :], bias, NEG_INF)
    return bias.reshape(kr, N_HEADS * GRID_W, kr * GRID_W)


def _nbr(dqkv, bias, batch, seq, rows_per_step):
    grid_rows = seq // GRID_W
    kr = min(NA_ROWS, grid_rows)
    nblk = grid_rows // rows_per_step
    tq = rows_per_step * GRID_W
    return pl.pallas_call(
        functools.partial(_nbr_kernel, rows_per_step=rows_per_step, grid_rows=grid_rows, kr=kr),
        grid=(batch, nblk),
        in_specs=[pl.BlockSpec((tq, MIX_W), lambda b, i: (b * nblk + i, 0)),
                  pl.BlockSpec((seq, MIX_W), lambda b, i: (b, 1)),
                  pl.BlockSpec((seq, MIX_W), lambda b, i: (b, 2)),
                  _const_spec(bias.shape)],
        out_specs=pl.BlockSpec((tq, MIX_W), lambda b, i: (b * nblk + i, 0)),
        out_shape=jax.ShapeDtypeStruct((batch * seq, MIX_W), BF16),
        compiler_params=_params("parallel", "arbitrary"),
    )(dqkv, dqkv, dqkv, bias)


BAND_UNROLL = 2


def _band_kernel(qkv_ref, o_ref, lse_ref, *, half, kwin, tq):
    length = qkv_ref.shape[0]
    rows = o_ref.shape[0]
    nq = rows // tq
    row0 = pl.program_id(1) * rows

    def unit(u):
        r, qi = u // nq, u % nq
        col = pl.multiple_of(r * (3 * MIX_W), LANES)
        ocol = pl.multiple_of(r * MIX_W, LANES)
        q_local = pl.multiple_of(qi * tq, tq)
        i0 = row0 + q_local
        kstart = pl.multiple_of(jnp.clip(i0 - half, 0, length - kwin), BF16_SUBLANES)
        qs = _stack_heads(qkv_ref[pl.ds(pl.multiple_of(i0, tq), tq), pl.ds(col, MIX_W)])
        s = lax.dot_general(qs, qkv_ref[pl.ds(kstart, kwin), pl.ds(col + MIX_W, MIX_W)],
                            (((1,), (1,)), ((), ())), preferred_element_type=F32)
        yield
        qpos = i0 + lax.broadcasted_iota(jnp.int32, (N_HEADS * tq, kwin), 0) % tq
        kpos = kstart + lax.broadcasted_iota(jnp.int32, (N_HEADS * tq, kwin), 1)
        s = jnp.where(jnp.abs(qpos - kpos) <= half, s, NEG_INF)
        m = jnp.max(s, axis=-1, keepdims=True)
        p = jnp.exp(s - m)
        l = jnp.sum(p, axis=-1, keepdims=True)
        o = jnp.dot(p.astype(BF16), qkv_ref[pl.ds(kstart, kwin), pl.ds(col + 2 * MIX_W, MIX_W)],
                    preferred_element_type=F32)
        yield
        o_ref[pl.ds(q_local, tq), pl.ds(ocol, MIX_W)] = _unstack_heads(o / l, tq).astype(o_ref.dtype)
        lse = m + jnp.log(l)
        lane = lax.broadcasted_iota(jnp.int32, (1, lse_ref.shape[1]), 1)
        packed = lse_ref[pl.ds(q_local, tq), :]
        for h in range(N_HEADS):
            packed = jnp.where(lane == r * N_HEADS + h, lse[h * tq:(h + 1) * tq], packed)
        lse_ref[pl.ds(q_local, tq), :] = packed

    lse_ref[...] = jnp.zeros(lse_ref.shape, F32)
    units = (o_ref.shape[1] // MIX_W) * nq

    def body(i, carry):
        _interleave(*[unit(i * BAND_UNROLL + j) for j in range(BAND_UNROLL)])
        return carry

    lax.fori_loop(0, units // BAND_UNROLL, body, 0)


def _band(a, batch, seq, window, dil, tq, step_rows):
    half = window // (2 * dil)
    length = seq // dil
    tq = min(tq, length)
    kwin = min(tq + 2 * half, length)
    rows = min(step_rows, length) if dil == 1 else length
    view = a.reshape(batch, length, dil * 3 * MIX_W)
    o, lse = pl.pallas_call(
        functools.partial(_band_kernel, half=half, kwin=kwin, tq=tq),
        grid=(batch, length // rows),
        in_specs=[pl.BlockSpec((None, length, dil * 3 * MIX_W), lambda b, i: (b, 0, 0))],
        out_specs=[pl.BlockSpec((None, rows, dil * MIX_W), lambda b, i: (b, i, 0)),
                   pl.BlockSpec((None, rows, dil * N_HEADS), lambda b, i: (b, i, 0))],
        out_shape=[jax.ShapeDtypeStruct((batch, length, dil * MIX_W), BF16),
                   jax.ShapeDtypeStruct((batch, length, dil * N_HEADS), F32)],
        compiler_params=_params("parallel", "arbitrary"),
    )(view)
    return o.reshape(batch * length, dil * MIX_W), lse.reshape(batch * seq, N_HEADS)


CUM_ROWS = 256


def _chunk_tri(n, upper):
    r = lax.broadcasted_iota(jnp.int32, (n, n), 0)
    c = lax.broadcasted_iota(jnp.int32, (n, n), 1)
    same = (r // B_CHUNK) == (c // B_CHUNK)
    keep = (r <= c) if upper else (r >= c)
    return jnp.where(same & keep, 1.0, 0.0).astype(F32)


def _delta_prep_kernel(x_ref, prev_ref, next_ref, ab_ref, conv_ref, alog_ref, dtb_ref,
                       qkv_ref, gate_ref, xe_ref):
    j = pl.program_id(1)
    rows = x_ref.shape[0]
    pad = BF16_SUBLANES
    keep_prev = jnp.where(j > 0, 1.0, 0.0)
    keep_next = jnp.where(j < pl.num_programs(1) - 1, 1.0, 0.0)
    xe_ref[0:pad, :] = prev_ref[...].astype(F32) * keep_prev
    xe_ref[pad:pad + rows, :] = x_ref[...].astype(F32)
    xe_ref[pad + rows:, :] = next_ref[...].astype(F32) * keep_next
    centre = B_CONV // 2
    y = jnp.zeros((rows, 3 * MIX_W), F32)
    for tap in range(B_CONV):
        y = y + xe_ref[pad + tap - centre:pad + tap - centre + rows, :] * conv_ref[tap:tap + 1, :]
    y = y * jax.nn.sigmoid(y)
    q, k = y[:, 0:MIX_W], y[:, MIX_W:2 * MIX_W]
    q = q * lax.rsqrt(_head_sums(q * q) + 1e-6) * (HEAD_DIM ** -0.5)
    k = k * lax.rsqrt(_head_sums(k * k) + 1e-6)
    qkv_ref[:, 0:MIX_W] = q.astype(BF16)
    qkv_ref[:, MIX_W:2 * MIX_W] = k.astype(BF16)
    qkv_ref[:, 2 * MIX_W:] = y[:, 2 * MIX_W:].astype(BF16)

    ab = ab_ref[...]
    col = lax.broadcasted_iota(jnp.int32, (1, ab.shape[-1]), 1)
    xs = ab + dtb_ref[...]
    softplus = jnp.maximum(xs, 0.0) + jnp.log(1.0 + jnp.exp(-jnp.abs(xs)))
    g = -jnp.exp(alog_ref[...]) * softplus
    beta = jax.nn.sigmoid(ab)
    tri_f, tri_b = _chunk_tri(CUM_ROWS, False), _chunk_tri(CUM_ROWS, True)
    for r0 in range(0, rows, CUM_ROWS):
        gp = g[r0:r0 + CUM_ROWS]
        cum_f = jnp.dot(tri_f, gp, preferred_element_type=F32, precision=lax.Precision.HIGHEST)
        cum_b = jnp.dot(tri_b, gp, preferred_element_type=F32, precision=lax.Precision.HIGHEST)
        gate_ref[r0:r0 + CUM_ROWS, :] = jnp.where(col < 8, beta[r0:r0 + CUM_ROWS],
                                                   jnp.where(col < 12, cum_f, cum_b))


def _delta_prep(bqkv, ab, lw, batch, seq, rows):
    t = batch * seq
    nblk = seq // rows
    per = rows // BF16_SUBLANES
    last = t // BF16_SUBLANES - 1
    w3 = 3 * MIX_W
    return pl.pallas_call(
        _delta_prep_kernel,
        grid=(batch, nblk),
        in_specs=[pl.BlockSpec((rows, w3), lambda b, j: (b * nblk + j, 0)),
                  pl.BlockSpec((BF16_SUBLANES, w3), lambda b, j: (jnp.maximum((b * nblk + j) * per - 1, 0), 0)),
                  pl.BlockSpec((BF16_SUBLANES, w3), lambda b, j: (jnp.minimum((b * nblk + j + 1) * per, last), 0)),
                  pl.BlockSpec((rows, 16), lambda b, j: (b * nblk + j, 0)),
                  _const_spec(lw["conv_w"].shape), _const_spec((1, 16)), _const_spec((1, 16))],
        out_specs=[pl.BlockSpec((rows, w3), lambda b, j: (b * nblk + j, 0)),
                   pl.BlockSpec((rows, 16), lambda b, j: (b * nblk + j, 0))],
        out_shape=[jax.ShapeDtypeStruct((t, w3), BF16), jax.ShapeDtypeStruct((t, 16), F32)],
        scratch_shapes=[pltpu.VMEM((rows + 2 * BF16_SUBLANES, w3), F32)],
        compiler_params=_params("parallel", "arbitrary"),
    )(bqkv, bqkv, bqkv, ab, lw["conv_w"], lw["a_log16"], lw["dt_bias16"])


def _expand_heads(cols):
    ids = _head_ids(MIX_W)
    out = jnp.broadcast_to(cols[:, 0:1], (cols.shape[0], MIX_W))
    for h in range(1, N_HEADS):
        out = jnp.where(ids == h, jnp.broadcast_to(cols[:, h:h + 1], out.shape), out)
    return out


def _block_diag(packed, mask):
    return jnp.where(mask, jnp.concatenate([packed] * N_HEADS, axis=0), 0.0).astype(BF16)


def _mm(a, b):
    return jnp.dot(a.astype(BF16), b.astype(BF16), preferred_element_type=F32)


def _delta_local(backward, qkv_ref, gate_ref, grow_ref, u_ref, w_ref, qh_ref, kt_ref, in_ref, eg_ref, c):
    n = B_CHUNK
    d = 1 if backward else 0
    r0 = pl.multiple_of(c * n, n)
    q = qkv_ref[pl.ds(r0, n), 0:MIX_W]
    k = qkv_ref[pl.ds(r0, n), MIX_W:2 * MIX_W]
    v = qkv_ref[pl.ds(r0, n), 2 * MIX_W:].astype(F32)
    gates = gate_ref[pl.ds(r0, n), :]
    d0 = 4 if backward else 0
    beta = _expand_heads(gates[:, d0:d0 + 4])
    gc = _expand_heads(gates[:, 8 + d0:12 + d0])
    gc_row = grow_ref[pl.ds(c, 1), d0 * n:(d0 + 4) * n]

    row = lax.broadcasted_iota(jnp.int32, (n, MIX_W), 0)
    col = lax.broadcasted_iota(jnp.int32, (n, MIX_W), 1) % n
    incl = (row <= col) if backward else (row >= col)
    strict = (row < col) if backward else (row > col)
    bd_mask = (lax.broadcasted_iota(jnp.int32, (MIX_W, MIX_W), 0) // n
               == lax.broadcasted_iota(jnp.int32, (MIX_W, MIX_W), 1) // n)
    decay = jnp.exp(jnp.where(incl, gc - gc_row, NEG_INF))

    kf = k.astype(F32)
    kq = lax.dot_general(jnp.concatenate([q, k], axis=0), _stack_heads(k), (((1,), (1,)), ((), ())),
                         preferred_element_type=F32)
    yield
    intra = kq[0:n] * decay
    low = jnp.where(strict, kq[n:] * beta * decay, 0.0)

    x = jnp.where(row == col, 1.0, 0.0) - low
    p = _mm(low, _block_diag(low, bd_mask))
    yield
    steps = int(math.log2(n)) - 1
    for step in range(steps):
        last = step == steps - 1
        lhs = x if last else jnp.concatenate([x, p], axis=0)
        r = _mm(lhs, _block_diag(p, bd_mask))
        yield
        x = x + r[0:n]
        if not last:
            p = r[n:]

    kbeta = kf * beta
    u_ref[d, pl.ds(r0, n), :] = _mm(x, _block_diag(v * beta, bd_mask))
    w_ref[d, pl.ds(r0, n), :] = _mm(x, _block_diag(kbeta * jnp.exp(gc), bd_mask)).astype(BF16)
    edge = gc[0:1] if backward else gc[n - 1:n]
    qh_ref[d, pl.ds(r0, n), :] = (q.astype(F32) * jnp.exp(gc)).astype(BF16)
    kt_ref[d, pl.ds(r0, n), :] = (kf * jnp.exp(edge - gc)).astype(BF16)
    in_ref[d, pl.ds(r0, n), :] = intra.astype(BF16)
    eg_ref[d, pl.ds(c, 1), :] = jnp.exp(edge)


def _delta_step(backward, u_ref, w_ref, qh_ref, kt_ref, in_ref, eg_ref, o_ref, s_ref, c):
    n = B_CHUNK
    d = 1 if backward else 0
    r0 = pl.multiple_of(c * n, n)
    bd_mask = (lax.broadcasted_iota(jnp.int32, (MIX_W, MIX_W), 0) // n
               == lax.broadcasted_iota(jnp.int32, (MIX_W, MIX_W), 1) // n)
    state = s_ref[d]
    lhs = jnp.concatenate([w_ref[d, pl.ds(r0, n), :], qh_ref[d, pl.ds(r0, n), :]], axis=0)
    ws_qs = jnp.dot(lhs, state.astype(BF16), preferred_element_type=F32)
    yield
    v_new = u_ref[d, pl.ds(r0, n), :] - ws_qs[0:n]
    o_ref[pl.ds(r0, n), :] = ws_qs[n:] + jnp.dot(in_ref[d, pl.ds(r0, n), :], _block_diag(v_new, bd_mask),
                                                  preferred_element_type=F32)
    kv = lax.dot_general(kt_ref[d, pl.ds(r0, n), :], v_new.astype(BF16), (((0,), (0,)), ((), ())),
                         preferred_element_type=F32)
    s_ref[d] = state * eg_ref[d, pl.ds(c, 1), :] + jnp.where(bd_mask, kv, 0.0)


LOCAL_UNROLL = 8


def _delta_kernel(qkv_f, gate_f, grow_f, qkv_b, gate_b, grow_b, of_ref, ob_ref,
                  s_ref, u_ref, w_ref, qh_ref, kt_ref, in_ref, eg_ref):
    @pl.when(pl.program_id(1) == 0)
    def _():
        s_ref[...] = jnp.zeros(s_ref.shape, F32)

    nchunk = qkv_f.shape[0] // B_CHUNK
    staged = (u_ref, w_ref, qh_ref, kt_ref, in_ref, eg_ref)

    def local(i, carry):
        chunks = [i * LOCAL_UNROLL + g for g in range(LOCAL_UNROLL)]
        _interleave(*[_delta_local(False, qkv_f, gate_f, grow_f, *staged, c) for c in chunks],
                    *[_delta_local(True, qkv_b, gate_b, grow_b, *staged, c) for c in chunks])
        return carry

    lax.fori_loop(0, nchunk // LOCAL_UNROLL, local, 0)

    def step(i, carry):
        _interleave(_delta_step(False, *staged, of_ref, s_ref, i),
                    _delta_step(True, *staged, ob_ref, s_ref, nchunk - 1 - i))
        return carry

    lax.fori_loop(0, nchunk, step, 0)


def _delta(qkvn, gates, batch, seq, rows):
    t = batch * seq
    nblk = seq // rows
    nchunk = rows // B_CHUNK
    grow = gates[:, 8:16].reshape(t // B_CHUNK, B_CHUNK, 8).transpose(0, 2, 1).reshape(t // B_CHUNK, 8 * B_CHUNK)
    fwd = lambda b, j: (b * nblk + j, 0)
    bwd = lambda b, j: (b * nblk + nblk - 1 - j, 0)
    specs = lambda m: [pl.BlockSpec((rows, 3 * MIX_W), m), pl.BlockSpec((rows, 16), m),
                       pl.BlockSpec((nchunk, 8 * B_CHUNK), m)]
    return pl.pallas_call(
        _delta_kernel,
        grid=(batch, nblk),
        in_specs=specs(fwd) + specs(bwd),
        out_specs=[pl.BlockSpec((rows, MIX_W), fwd), pl.BlockSpec((rows, MIX_W), bwd)],
        out_shape=[jax.ShapeDtypeStruct((t, MIX_W), F32)] * 2,
        scratch_shapes=([pltpu.VMEM((2, MIX_W, MIX_W), F32), pltpu.VMEM((2, rows, MIX_W), F32)]
                        + [pltpu.VMEM((2, rows, MIX_W), BF16)] * 4 + [pltpu.VMEM((2, nchunk, MIX_W), F32)]),
        compiler_params=_params("parallel", "arbitrary"),
    )(qkvn, gates, grow, qkvn, gates, grow)


def _merge_kernel(x_ref, gain_ref, oa0, oa1, oa2, la0, la1, la2, bf_ref, bb_ref, z_ref, bgain_ref,
                  oc_ref, od_ref, wg_ref, bg_ref, wbr_ref, wout_ref, out_ref, o_scr):
    x = x_ref[...]
    tm = x.shape[0]
    h = _rms_rows(x, gain_ref[...]).astype(BF16)

    slabs = MIX_W // LANES
    outs_a = [oa0[...].astype(F32)]
    for p, ((_, dil), oa) in enumerate(zip(A_PATTERNS[1:], (oa1, oa2))):
        for r in range(dil):
            for c in range(slabs):
                col = r * MIX_W + c * LANES
                o_scr[p, c, pl.ds(r, tm // dil, stride=dil), :] = oa[:, col:col + LANES].astype(F32)
        outs_a.append(jnp.concatenate([o_scr[p, c] for c in range(slabs)], axis=1))

    lses = [_expand_heads(r[...]) for r in (la0, la1, la2)]
    top = jnp.maximum(jnp.maximum(lses[0], lses[1]), lses[2])
    ws = [jnp.exp(l - top) for l in lses]
    o_a = sum(w * o for w, o in zip(ws, outs_a)) / (ws[0] + ws[1] + ws[2])

    o_b = bf_ref[...] + bb_ref[...]
    z = z_ref[...].astype(F32)
    o_b = (o_b * lax.rsqrt(_head_sums(o_b * o_b) * (1.0 / HEAD_DIM) + RMS_EPS) * bgain_ref[...]
           * (z * jax.nn.sigmoid(z)))

    branches = (o_a, o_b, oc_ref[...], od_ref[...])
    y = jnp.zeros(x.shape, F32)
    for i, o in enumerate(branches):
        gate = jax.nn.sigmoid(jnp.dot(h, wg_ref[i], preferred_element_type=F32) + bg_ref[i])
        y = y + gate * jnp.dot(o.astype(BF16), wbr_ref[i], preferred_element_type=F32)
    out_ref[...] = x + jnp.dot(y.astype(BF16), wout_ref[...], preferred_element_type=F32)


def _merge(x2d, a_outs, b_outs, z, oc, od, lw, tm):
    t = x2d.shape[0]
    tok = lambda w: pl.BlockSpec((tm, w), lambda i: (i, 0))
    (o0, l0), (o1, l1), (o2, l2) = a_outs
    acts = (o0, o1, o2, l0, l1, l2, b_outs[0], b_outs[1], z)
    weights = (lw["wg"], lw["bg"], lw["wbr"], lw["wout"])
    return pl.pallas_call(
        _merge_kernel,
        grid=(t // tm,),
        in_specs=([tok(D_MODEL), _const_spec(lw["norm_mix"].shape)]
                  + [pl.BlockSpec((tm * a.shape[0] // t, a.shape[1]), lambda i: (i, 0)) for a in acts]
                  + [_const_spec(lw["b_gain"].shape), tok(MIX_W), tok(MIX_W)]
                  + [_const_spec(w.shape) for w in weights]),
        out_specs=tok(D_MODEL),
        out_shape=jax.ShapeDtypeStruct((t, D_MODEL), F32),
        scratch_shapes=[pltpu.VMEM((len(A_PATTERNS) - 1, MIX_W // LANES, tm, LANES), F32)],
        compiler_params=_params("parallel"),
    )(x2d, lw["norm_mix"], *acts, lw["b_gain"], oc, od, *weights)


EXPERT_UNROLL = 2


def _moe_kernel(x_ref, gain_ref, wr_ref, br_ref, wgt_ref, wup_ref, wdn_ref, fgain_ref, out_ref,
                hs_ref, ws_ref, ys_ref, *, final, cap):
    x = x_ref[...]
    tm = x.shape[0]
    h = _rms_rows(x, gain_ref[...])
    hb = h.astype(BF16)
    h_lo = (h - hb.astype(F32)).astype(BF16)
    parts = jnp.dot(jnp.concatenate([hb, h_lo], axis=0), wr_ref[...], preferred_element_type=F32)
    logits = (parts[:tm, :LANES] + parts[:tm, LANES:]) + (parts[tm:, :LANES] + parts[tm:, LANES:]) + br_ref[...]
    lane = lax.broadcasted_iota(jnp.int32, (1, LANES), 1)
    none = LANES

    def first_max(vals):
        top = jnp.max(vals, axis=-1, keepdims=True)
        return top, jnp.min(jnp.where(vals == top, lane, none), axis=-1, keepdims=True)

    gl = jnp.where(lane < N_GROUPS, logits, NEG_INF)
    g_top, g_idx = first_max(gl)
    g_prob = 1.0 / jnp.sum(jnp.exp(gl - g_top), axis=-1, keepdims=True)
    in_group = (lane >= N_GROUPS) & (lane < N_GROUPS + N_EXPERTS) & ((lane - N_GROUPS) // EXPERTS_PER_GROUP == g_idx)
    el = jnp.where(in_group, logits, NEG_INF)
    top1, idx1 = first_max(el)
    top2, idx2 = first_max(jnp.where(lane == idx1, NEG_INF, el))
    e2 = jnp.exp(top2 - top1)
    w1 = g_prob / (1.0 + e2)
    weights = jnp.where(lane == idx1, w1, 0.0) + jnp.where(lane == idx2, w1 * e2, 0.0)

    def expert(rows, row_weights, e, outs):
        w_e = jnp.sum(jnp.where(lane == e + N_GROUPS, row_weights, 0.0), axis=-1, keepdims=True)
        gate = jnp.dot(rows, wgt_ref[e], preferred_element_type=F32)
        up = jnp.dot(rows, wup_ref[e], preferred_element_type=F32)
        yield
        hid = gate * jax.nn.sigmoid(gate) * up * w_e
        outs.append(jnp.dot(hid.astype(BF16), wdn_ref[e], preferred_element_type=F32))

    def finish(moe_out):
        y = x + moe_out
        if final:
            y = _rms_rows(y, fgain_ref[...])
        out_ref[...] = y

    member = jnp.where(lane == g_idx, 1.0, 0.0)
    fits = jnp.max(jnp.sum(member, axis=0, keepdims=True)) <= cap

    @pl.when(fits)
    def _():
        before = lax.broadcasted_iota(jnp.int32, (tm, tm), 0) > lax.broadcasted_iota(jnp.int32, (tm, tm), 1)
        ahead = jnp.dot(jnp.where(before, 1.0, 0.0).astype(BF16), member.astype(BF16),
                        preferred_element_type=F32)
        pos = jnp.sum(member * ahead, axis=-1, keepdims=True).astype(jnp.int32)
        slot = g_idx * cap + pos
        perm_t = jnp.where(lax.broadcasted_iota(jnp.int32, (1, N_GROUPS * cap), 1) == slot, 1.0, 0.0).astype(BF16)
        slot_row = jnp.broadcast_to(slot.astype(F32), (tm, LANES)).T[0:1]
        perm = jnp.where(lax.broadcasted_iota(jnp.int32, (N_GROUPS * cap, 1), 0).astype(F32) == slot_row,
                         1.0, 0.0).astype(BF16)
        gather = lambda a: jnp.dot(perm, a, preferred_element_type=F32)
        hs_ref[...] = gather(hb).astype(BF16)
        w_hi = weights.astype(BF16)
        w_parts = gather(jnp.concatenate([w_hi, (weights - w_hi.astype(F32)).astype(BF16)], axis=1))
        ws_ref[...] = w_parts[:, :LANES] + w_parts[:, LANES:]

        def group(g, carry):
            r0 = pl.multiple_of(g * cap, cap)
            rows, row_weights = hs_ref[pl.ds(r0, cap), :], ws_ref[pl.ds(r0, cap), :]
            outs = []
            _interleave(*[expert(rows, row_weights, g * EXPERTS_PER_GROUP + j, outs)
                          for j in range(EXPERTS_PER_GROUP)])
            ys_ref[pl.ds(r0, cap), :] = sum(outs[1:], outs[0]).astype(BF16)
            return carry

        lax.fori_loop(0, N_GROUPS, group, 0)
        finish(jnp.dot(perm_t, ys_ref[...], preferred_element_type=F32))

    @pl.when(jnp.logical_not(fits))
    def _():
        def experts(i, acc):
            outs = []
            _interleave(*[expert(hb, weights, i * EXPERT_UNROLL + j, outs) for j in range(EXPERT_UNROLL)])
            return acc + sum(outs[1:], outs[0])

        finish(lax.fori_loop(0, N_EXPERTS // EXPERT_UNROLL, experts, jnp.zeros(x.shape, F32)))


def _moe(x2d, lw, final_gain, final, tm, cap):
    t = x2d.shape[0]
    tok = pl.BlockSpec((tm, D_MODEL), lambda i: (i, 0))
    once = lambda a: pl.BlockSpec(a.shape, lambda i: (0,) * a.ndim, pipeline_mode=pl.Buffered(1))
    weights = (lw["norm_ffn"], lw["w_route"], lw["b_route"], lw["w_ff_gate"], lw["w_ff_up"], lw["w_ff_down"],
               final_gain)
    return pl.pallas_call(
        functools.partial(_moe_kernel, final=final, cap=cap),
        grid=(t // tm,),
        in_specs=[tok] + [once(w) for w in weights],
        out_specs=tok,
        out_shape=jax.ShapeDtypeStruct((t, D_MODEL), F32),
        scratch_shapes=[pltpu.VMEM((N_GROUPS * cap, D_MODEL), BF16), pltpu.VMEM((N_GROUPS * cap, LANES), F32),
                        pltpu.VMEM((N_GROUPS * cap, D_MODEL), BF16)],
        compiler_params=_params("parallel"),
    )(x2d, *weights)


def _split_bf16(w):
    hi = w.astype(BF16)
    return jnp.concatenate([hi, (w - hi.astype(F32)).astype(BF16)], axis=1)


def _layer_weights(p, l):
    w_in = p["w_in"][l]
    offs = np.cumsum([0, 3 * MIX_W, 3 * MIX_W, MIX_W, 16, MIX_W, 2 * C_KV_HEADS * HEAD_DIM, 3 * MIX_W])
    sec = [w_in[:, offs[i]:offs[i + 1]] for i in range(7)]
    c_order = np.concatenate([np.arange(h * HEAD_DIM, (h + 1) * HEAD_DIM) for h in (0, 2, 1, 3)])
    row = lambda v: v.reshape(1, -1).astype(F32)
    gate16 = lambda v: jnp.concatenate([jnp.zeros((8,), F32), v.reshape(-1).astype(F32)]).reshape(1, 16)
    wbr = p["w_branch"][l]
    wbr = jnp.stack([wbr[0], wbr[1], wbr[2][c_order], wbr[3]])
    w_route = jnp.concatenate([p["w_route_group"][l], p["w_route_expert"][l].reshape(D_MODEL, N_EXPERTS)], axis=1)
    b_route = jnp.concatenate([p["b_route_group"][l], p["b_route_expert"][l].reshape(N_EXPERTS)])
    pad_lanes = lambda a: jnp.pad(a, [(0, 0)] * (a.ndim - 1) + [(0, LANES - a.shape[-1])])
    return dict(
        norm_mix=row(p["norm_mix"][l]),
        wa=sec[0].astype(BF16), wb=sec[1].astype(BF16), wz=sec[2].astype(BF16),
        wab=pad_lanes(sec[3]).astype(BF16), wcq=sec[4][:, c_order].astype(BF16), wck=sec[5][:, :C_KV_HEADS * HEAD_DIM].astype(BF16), wcvt=sec[5][:, C_KV_HEADS * HEAD_DIM:].T.astype(BF16),
        wd=sec[6].astype(BF16),
        cq_gain=row(jnp.tile(p["c_q_norm"][l], N_HEADS)), ck_gain=row(jnp.tile(p["c_k_norm"][l], C_KV_HEADS)),
        conv_w=p["conv_w"][l].astype(F32), a_log16=gate16(p["a_log"][l]), dt_bias16=gate16(p["dt_bias"][l]),
        b_gain=row(jnp.tile(p["b_norm"][l], N_HEADS)),
        wg=p["w_gate"][l].astype(BF16), bg=p["b_gate"][l].reshape(N_HEADS, 1, D_MODEL).astype(F32),
        wbr=wbr.astype(BF16), wout=p["w_out"][l].astype(BF16),
        norm_ffn=row(p["norm_ffn"][l]),
        w_route=_split_bf16(pad_lanes(w_route).astype(F32)), b_route=pad_lanes(b_route.reshape(1, -1)).astype(F32),
        w_ff_gate=p["w_ff_gate"][l].reshape(N_EXPERTS, D_MODEL, D_EXPERT).astype(BF16),
        w_ff_up=p["w_ff_up"][l].reshape(N_EXPERTS, D_MODEL, D_EXPERT).astype(BF16),
        w_ff_down=p["w_ff_down"][l].reshape(N_EXPERTS, D_EXPERT, D_MODEL).astype(BF16),
        d_rpb=p["d_rpb"][l],
    )


def _tiles(seq):
    tm = min(512, seq)
    share = tm // N_GROUPS
    moe_cap = (share + share // 4 + BF16_SUBLANES - 1) // BF16_SUBLANES * BF16_SUBLANES
    return dict(tm=tm, moe_cap=moe_cap, tq_c=min(256, seq), tk_c=min(512, seq), rows_d=min(8, seq // GRID_W),
                tq_a=128, rows_a=2048, rows_prep=min(512, seq), rows_delta=min(2048, seq))


def _encoder(x, layers, final_gain):
    batch, seq, _ = x.shape
    ts = _tiles(seq)
    tables = _rotary_tables(seq)
    depth = len(layers)
    x2d = x.reshape(batch * seq, D_MODEL)
    for l, lw in enumerate(layers):
        a_views, (bqkv, z, ab, cq, ck, cvt, dqkv) = _in_proj(x2d, seq, lw, tables, ts["tm"])
        a_outs = [_band(a, batch, seq, window, dil, ts["tq_a"], ts["rows_a"])
                  for a, (window, dil) in zip(a_views, A_PATTERNS)]
        qkvn, gates = _delta_prep(bqkv, ab, lw, batch, seq, ts["rows_prep"])
        b_outs = _delta(qkvn, gates, batch, seq, ts["rows_delta"])
        oc = _gqa(cq, ck, cvt, batch, seq, ts["tq_c"], ts["tk_c"])
        od = _nbr(dqkv, _nbr_bias(lw["d_rpb"], min(NA_ROWS, seq // GRID_W)), batch, seq, ts["rows_d"])
        x2d = _merge(x2d, a_outs, b_outs, z, oc, od, lw, ts["tm"])
        x2d = _moe(x2d, lw, final_gain, l == depth - 1, ts["tm"], ts["moe_cap"])
    return x2d.reshape(batch, seq, D_MODEL)


def kernel(x_prompt, x_sample, norm_mix, w_in, conv_w, a_log, dt_bias, b_norm, c_q_norm, c_k_norm, d_rpb,
           w_gate, b_gate, w_branch, w_out, norm_ffn, w_route_group, b_route_group, w_route_expert,
           b_route_expert, w_ff_gate, w_ff_up, w_ff_down, norm_final):
    p = dict(norm_mix=norm_mix, w_in=w_in, conv_w=conv_w, a_log=a_log, dt_bias=dt_bias, b_norm=b_norm,
             c_q_norm=c_q_norm, c_k_norm=c_k_norm, d_rpb=d_rpb, w_gate=w_gate, b_gate=b_gate,
             w_branch=w_branch, w_out=w_out, norm_ffn=norm_ffn, w_route_group=w_route_group,
             b_route_group=b_route_group, w_route_expert=w_route_expert, b_route_expert=b_route_expert,
             w_ff_gate=w_ff_gate, w_ff_up=w_ff_up, w_ff_down=w_ff_down)
    layers = [_layer_weights(p, l) for l in range(norm_mix.shape[0])]
    final_gain = norm_final.reshape(1, -1).astype(F32)
    return _encoder(x_prompt, layers, final_gain), _encoder(x_sample, layers, final_gain)
```

```python
import functools
import math

import jax
import jax.numpy as jnp
import numpy as np
from jax import lax
from jax.experimental import pallas as pl
from jax.experimental.pallas import tpu as pltpu

D_MODEL = 1024
HEAD_DIM = 64
N_HEADS = 4
MIX_W = N_HEADS * HEAD_DIM
GRID_W = 64
RMS_EPS = 1e-6
NEG_INF = -1e30
LOG2_E = math.log2(math.e)

A_PATTERNS = ((128, 1), (512, 4), (2048, 16))
ROPE_THETA = 500000.0
ROPE_DIMS = HEAD_DIM // 4
AXIAL_THETA = 10000.0
C_KV_HEADS = 2
B_CONV = 5
B_CHUNK = 64
NA_ROWS = 8
NA_COLS = 16
N_GROUPS = 4
EXPERTS_PER_GROUP = 4
N_EXPERTS = N_GROUPS * EXPERTS_PER_GROUP
D_EXPERT = 256

LANES = 128
BF16_SUBLANES = 16
VMEM_LIMIT = 56 * 1024 * 1024

F32 = jnp.float32
BF16 = jnp.bfloat16


def _params(*sem):
    return pltpu.CompilerParams(dimension_semantics=sem, vmem_limit_bytes=VMEM_LIMIT)


def _const_spec(shape):
    n = len(shape)
    return pl.BlockSpec(shape, lambda *_: (0,) * n)


def _rms_rows(x, gain):
    return x * lax.rsqrt(jnp.mean(x * x, axis=-1, keepdims=True) + RMS_EPS) * gain


def _head_ids(width):
    return lax.broadcasted_iota(jnp.int32, (1, width), 1) // HEAD_DIM


def _block_ones(width):
    r = lax.broadcasted_iota(jnp.int32, (width, width), 0) // HEAD_DIM
    c = lax.broadcasted_iota(jnp.int32, (width, width), 1) // HEAD_DIM
    return jnp.where(r == c, 1.0, 0.0).astype(BF16)


def _head_sums(x):
    return jnp.dot(x.astype(BF16), _block_ones(x.shape[-1]), preferred_element_type=F32)


def _interleave(*chains):
    live = list(chains)
    while live:
        live = [c for c in live if next(c, live) is not live]


def _rotate(x, cos, sin_lo, sin_hi, shift):
    w = x.shape[-1]
    return x * cos + pltpu.roll(x, shift, 1) * sin_hi + pltpu.roll(x, w - shift, 1) * sin_lo


def _in_proj_kernel(x_ref, gain_ref, wa_ref, wb_ref, wz_ref, wab_ref, wcq_ref, wck_ref, wcvt_ref, wd_ref,
                    cqg_ref, ckg_ref, rot_a_ref, rot_c_ref,
                    a1_ref, a4_ref, a16_ref, b_ref, z_ref, ab_ref, cq_ref, ck_ref, cvt_ref, d_ref, a_scr):
    h = _rms_rows(x_ref[...], gain_ref[...]).astype(BF16)
    dot = lambda w_ref: jnp.dot(h, w_ref[...], preferred_element_type=F32)
    scale = HEAD_DIM ** -0.5
    tm = x_ref.shape[0]

    ua = dot(wa_ref)
    cos, s_lo, s_hi = rot_a_ref[0], rot_a_ref[1], rot_a_ref[2]
    half = ROPE_DIMS // 2
    qkv_a = (_rotate(ua[:, 0:MIX_W], cos, s_lo, s_hi, half) * scale,
             _rotate(ua[:, MIX_W:2 * MIX_W], cos, s_lo, s_hi, half), ua[:, 2 * MIX_W:])
    a1_ref[...] = jnp.concatenate(qkv_a, axis=1).astype(BF16)
    slabs = 3 * MIX_W // LANES
    for c in range(slabs):
        part, off = divmod(c * LANES, MIX_W)
        a_scr[c] = qkv_a[part][:, off:off + LANES]
    for (_, dil), a_ref in zip(A_PATTERNS[1:], (a4_ref, a16_ref)):
        for r in range(dil):
            for c in range(slabs):
                col = r * 3 * MIX_W + c * LANES
                a_ref[:, col:col + LANES] = a_scr[c, pl.ds(r, tm // dil, stride=dil), :].astype(BF16)

    b_ref[...] = dot(wb_ref).astype(BF16)
    z_ref[...] = dot(wz_ref).astype(BF16)
    ab_ref[...] = dot(wab_ref)[:, :ab_ref.shape[-1]]

    cos, s_lo, s_hi = rot_c_ref[0], rot_c_ref[1], rot_c_ref[2]
    quarter = HEAD_DIM // 4
    uq = dot(wcq_ref)
    uq = uq * lax.rsqrt(_head_sums(uq * uq) * (1.0 / HEAD_DIM) + RMS_EPS) * cqg_ref[...]
    cq_ref[...] = (_rotate(uq, cos, s_lo, s_hi, quarter) * (scale * LOG2_E)).astype(BF16)
    kw = C_KV_HEADS * HEAD_DIM
    uk = dot(wck_ref)
    uk = uk * lax.rsqrt(_head_sums(uk * uk) * (1.0 / HEAD_DIM) + RMS_EPS) * ckg_ref[...]
    ck_ref[...] = _rotate(uk, cos[:, :kw], s_lo[:, :kw], s_hi[:, :kw], quarter).astype(BF16)
    cvt_ref[...] = lax.dot_general(wcvt_ref[...], h, (((1,), (1,)), ((), ())),
                                   preferred_element_type=F32).astype(BF16)

    ud = dot(wd_ref)
    d_ref[:, 0:MIX_W] = (ud[:, 0:MIX_W] * scale).astype(BF16)
    d_ref[:, MIX_W:] = ud[:, MIX_W:].astype(BF16)


def _rotary_tables(seq):
    pos = np.arange(seq, dtype=np.float64)

    def table(pos_of_lane, theta, half, lane_in_group, active):
        freq_idx = lane_in_group % half
        inv = theta ** (-(freq_idx.astype(np.float64)) / half)
        ang = pos_of_lane * inv[None, :]
        cos = np.where(active[None, :], np.cos(ang), 1.0)
        sin = np.where(active[None, :], np.sin(ang), 0.0)
        low = (lane_in_group < half)[None, :]
        return np.stack([cos, np.where(low, -sin, 0.0), np.where(low, 0.0, sin)]).astype(np.float32)

    lane = np.arange(MIX_W) % HEAD_DIM
    rot_a = table(np.broadcast_to(pos[:, None], (seq, MIX_W)), ROPE_THETA, ROPE_DIMS // 2, lane % ROPE_DIMS,
                  lane < ROPE_DIMS)
    rows, cols = pos // GRID_W, pos % GRID_W
    half = HEAD_DIM // 2
    pos_c = np.where((lane < half)[None, :], rows[:, None], cols[:, None])
    rot_c = table(pos_c, AXIAL_THETA, half // 2, lane % half, np.ones(MIX_W, bool))
    return jnp.asarray(rot_a), jnp.asarray(rot_c)


def _in_proj(x2d, seq, lw, tables, tm):
    t = x2d.shape[0]
    rot_a, rot_c = tables
    nblk = seq // tm
    tok = lambda w: pl.BlockSpec((tm, w), lambda i: (i, 0))
    rot = pl.BlockSpec((3, tm, MIX_W), lambda i: (0, i % nblk, 0))
    weights = (lw["norm_mix"], lw["wa"], lw["wb"], lw["wz"], lw["wab"], lw["wcq"], lw["wck"], lw["wcvt"], lw["wd"],
               lw["cq_gain"], lw["ck_gain"])
    kw = C_KV_HEADS * HEAD_DIM
    outs = ((3 * MIX_W, BF16), (MIX_W, BF16), (16, F32), (MIX_W, BF16), (kw, BF16), None, (3 * MIX_W, BF16))
    out_specs = [pl.BlockSpec((kw, tm), lambda i: (0, i)) if o is None else tok(o[0]) for o in outs]
    out_shape = [jax.ShapeDtypeStruct((kw, t), BF16) if o is None else jax.ShapeDtypeStruct((t, o[0]), o[1])
                 for o in outs]
    a_specs = [pl.BlockSpec((tm // dil, dil * 3 * MIX_W), lambda i: (i, 0)) for _, dil in A_PATTERNS]
    a_shape = [jax.ShapeDtypeStruct((t // dil, dil * 3 * MIX_W), BF16) for _, dil in A_PATTERNS]
    res = pl.pallas_call(
        _in_proj_kernel,
        grid=(t // tm,),
        in_specs=[tok(D_MODEL)] + [_const_spec(w.shape) for w in weights] + [rot, rot],
        out_specs=a_specs + out_specs,
        out_shape=a_shape + out_shape,
        scratch_shapes=[pltpu.VMEM((3 * MIX_W // LANES, tm, LANES), F32)],
        compiler_params=_params("parallel"),
    )(x2d, *weights, rot_a, rot_c)
    return res[:len(A_PATTERNS)], res[len(A_PATTERNS):]


def _gqa_kernel(q_ref, k_ref, vt_ref, o_ref, acc_ref, s_ref, *, tk):
    tq = q_ref.shape[0]
    seq = k_ref.shape[0]
    kw = C_KV_HEADS * HEAD_DIM
    qt = q_ref[...].astype(F32).T
    row_g = lax.broadcasted_iota(jnp.int32, (kw, 1), 0) // HEAD_DIM
    qs = jnp.concatenate([jnp.where(row_g == g, qt[blk * kw:(blk + 1) * kw], 0.0)
                          for g in range(C_KV_HEADS) for blk in range(2)], axis=1).astype(BF16)
    acc_ref[...] = jnp.zeros(acc_ref.shape, F32)

    width = 2 * tq

    nchunk = seq // tk

    def scores(j, slot):
        start = pl.multiple_of(j * tk, tk)
        s_ref[slot] = jnp.dot(k_ref[pl.ds(start, tk), :], qs, preferred_element_type=F32)

    def kv_head(g, j, slot, m_old, l_old, out):
        cols = slice(g * width, (g + 1) * width)
        start = pl.multiple_of(j * tk, tk)
        s = s_ref[slot, :, cols]
        m_new = jnp.maximum(m_old, jnp.max(s, axis=0, keepdims=True))
        alpha = jnp.exp2(m_old - m_new)
        p = jnp.exp2(s - m_new)
        pv = jnp.dot(vt_ref[g * HEAD_DIM:(g + 1) * HEAD_DIM, pl.ds(start, tk)], p.astype(BF16),
                     preferred_element_type=F32)
        yield
        acc_ref[:, cols] = acc_ref[:, cols] * alpha + pv
        out[g] = (m_new, l_old * alpha + jnp.sum(p, axis=0, keepdims=True))

    def softmax_chunk(j, slot, carry):
        out = [None] * C_KV_HEADS
        _interleave(*[kv_head(g, j, slot, *carry[g], out) for g in range(C_KV_HEADS)])
        return tuple(out)

    def chunk_pair(i, carry, last):
        scores(2 * i + 1, 1)
        carry = softmax_chunk(2 * i, 0, carry)
        if not last:
            scores(2 * i + 2, 0)
        return softmax_chunk(2 * i + 1, 1, carry)

    scores(0, 0)
    carry = ((jnp.full((1, width), NEG_INF, F32), jnp.zeros((1, width), F32)),) * C_KV_HEADS
    if nchunk == 1:
        carry = softmax_chunk(0, 0, carry)
    else:
        assert nchunk % 2 == 0
        carry = lax.fori_loop(0, nchunk // 2 - 1, lambda i, c: chunk_pair(i, c, False), carry)
        carry = chunk_pair(nchunk // 2 - 1, carry, True)
    o = acc_ref[...] / jnp.concatenate([l for _, l in carry], axis=1)
    for blk in range(2):
        pair = jnp.concatenate([o[:, (2 * g + blk) * tq:(2 * g + blk + 1) * tq] for g in range(C_KV_HEADS)], axis=0)
        o_ref[:, blk * kw:(blk + 1) * kw] = pair.T.astype(o_ref.dtype)


def _gqa(cq, ck, cvt, batch, seq, tq, tk):
    nq = seq // tq
    kw = C_KV_HEADS * HEAD_DIM
    return pl.pallas_call(
        functools.partial(_gqa_kernel, tk=tk),
        grid=(batch, nq),
        in_specs=[pl.BlockSpec((tq, MIX_W), lambda b, i: (b * nq + i, 0)),
                  pl.BlockSpec((seq, kw), lambda b, i: (b, 0)),
                  pl.BlockSpec((kw, seq), lambda b, i: (0, b))],
        out_specs=pl.BlockSpec((tq, MIX_W), lambda b, i: (b * nq + i, 0)),
        out_shape=jax.ShapeDtypeStruct((batch * seq, MIX_W), BF16),
        scratch_shapes=[pltpu.VMEM((HEAD_DIM, 4 * tq), F32), pltpu.VMEM((2, tk, 4 * tq), F32)],
        compiler_params=_params("parallel", "arbitrary"),
    )(cq, ck, cvt)


def _stack_heads(q):
    ids = _head_ids(MIX_W)
    zero = jnp.zeros((), q.dtype)
    return jnp.concatenate([jnp.where(ids == h, q, zero) for h in range(N_HEADS)], axis=0)


def _unstack_heads(o, n):
    ids = _head_ids(MIX_W)
    out = o[0:n]
    for h in range(1, N_HEADS):
        out = jnp.where(ids == h, o[h * n:(h + 1) * n], out)
    return out


NBR_UNROLL = 4


def _nbr_kernel(q_ref, k_ref, v_ref, bias_ref, o_ref, *, rows_per_step, grid_rows, kr):
    j = pl.program_id(1)
    win = kr * GRID_W

    def query_row(i):
        r = j * rows_per_step + i
        rs = jnp.clip(r - kr // 2, 0, grid_rows - kr)
        kstart = pl.multiple_of(rs * GRID_W, GRID_W)
        qstart = pl.multiple_of(i * GRID_W, GRID_W)
        qs = _stack_heads(q_ref[pl.ds(qstart, GRID_W), :])
        s = lax.dot_general(qs, k_ref[pl.ds(kstart, win), :], (((1,), (1,)), ((), ())),
                            preferred_element_type=F32)
        yield
        s = s + bias_ref[r - rs]
        m = jnp.max(s, axis=-1, keepdims=True)
        p = jnp.exp(s - m)
        l = jnp.sum(p, axis=-1, keepdims=True)
        o = jnp.dot(p.astype(BF16), v_ref[pl.ds(kstart, win), :], preferred_element_type=F32)
        yield
        o_ref[pl.ds(qstart, GRID_W), :] = _unstack_heads(o / l, GRID_W).astype(o_ref.dtype)

    def body(i, carry):
        _interleave(*[query_row(i * NBR_UNROLL + u) for u in range(NBR_UNROLL)])
        return carry

    lax.fori_loop(0, rows_per_step // NBR_UNROLL, body, 0)


def _nbr_bias(rpb, kr):
    c = np.arange(GRID_W)
    cs = np.clip(c - NA_COLS // 2, 0, GRID_W - NA_COLS)
    col_valid = (c[None, :] >= cs[:, None]) & (c[None, :] < cs[:, None] + NA_COLS)
    dc = np.clip(c[None, :] - c[:, None], -(NA_COLS - 1), NA_COLS - 1) + NA_COLS - 1
    dr = np.arange(kr)[None, :] - np.arange(kr)[:, None] + NA_ROWS - 1
    pick_r = (dr[..., None] == np.arange(2 * NA_ROWS - 1)).astype(np.float32)
    pick_c = (dc[..., None] == np.arange(2 * NA_COLS - 1)).astype(np.float32)
    bias = jnp.einsum('hab,vja,qkb->vhqjk', rpb.astype(F32), pick_r, pick_c, precision=lax.Precision.HIGHEST)
    bias = jnp.where(col_valid[None, None, :, None, :], bias, NEG_INF)
    return bias.reshape(kr, N_HEADS * GRID_W, kr * GRID_W)


def _nbr(dqkv, bias, batch, seq, rows_per_step):
    grid_rows = seq // GRID_W
    kr = min(NA_ROWS, grid_rows)
    nblk = grid_rows // rows_per_step
    tq = rows_per_step * GRID_W
    return pl.pallas_call(
        functools.partial(_nbr_kernel, rows_per_step=rows_per_step, grid_rows=grid_rows, kr=kr),
        grid=(batch, nblk),
        in_specs=[pl.BlockSpec((tq, MIX_W), lambda b, i: (b * nblk + i, 0)),
                  pl.BlockSpec((seq, MIX_W), lambda b, i: (b, 1)),
                  pl.BlockSpec((seq, MIX_W), lambda b, i: (b, 2)),
                  _const_spec(bias.shape)],
        out_specs=pl.BlockSpec((tq, MIX_W), lambda b, i: (b * nblk + i, 0)),
        out_shape=jax.ShapeDtypeStruct((batch * seq, MIX_W), BF16),
        compiler_params=_params("parallel", "arbitrary"),
    )(dqkv, dqkv, dqkv, bias)


BAND_UNROLL = 2


def _band_kernel(qkv_ref, o_ref, lse_ref, *, half, kwin, tq):
    length = qkv_ref.shape[0]
    rows = o_ref.shape[0]
    nq = rows // tq
    row0 = pl.program_id(1) * rows

    def unit(u):
        r, qi = u // nq, u % nq
        col = pl.multiple_of(r * (3 * MIX_W), LANES)
        ocol = pl.multiple_of(r * MIX_W, LANES)
        q_local = pl.multiple_of(qi * tq, tq)
        i0 = row0 + q_local
        kstart = pl.multiple_of(jnp.clip(i0 - half, 0, length - kwin), BF16_SUBLANES)
        qs = _stack_heads(qkv_ref[pl.ds(pl.multiple_of(i0, tq), tq), pl.ds(col, MIX_W)])
        s = lax.dot_general(qs, qkv_ref[pl.ds(kstart, kwin), pl.ds(col + MIX_W, MIX_W)],
                            (((1,), (1,)), ((), ())), preferred_element_type=F32)
        yield
        qpos = i0 + lax.broadcasted_iota(jnp.int32, (N_HEADS * tq, kwin), 0) % tq
        kpos = kstart + lax.broadcasted_iota(jnp.int32, (N_HEADS * tq, kwin), 1)
        s = jnp.where(jnp.abs(qpos - kpos) <= half, s, NEG_INF)
        m = jnp.max(s, axis=-1, keepdims=True)
        p = jnp.exp(s - m)
        l = jnp.sum(p, axis=-1, keepdims=True)
        o = jnp.dot(p.astype(BF16), qkv_ref[pl.ds(kstart, kwin), pl.ds(col + 2 * MIX_W, MIX_W)],
                    preferred_element_type=F32)
        yield
        o_ref[pl.ds(q_local, tq), pl.ds(ocol, MIX_W)] = _unstack_heads(o / l, tq).astype(o_ref.dtype)
        lse = m + jnp.log(l)
        lane = lax.broadcasted_iota(jnp.int32, (1, lse_ref.shape[1]), 1)
        packed = lse_ref[pl.ds(q_local, tq), :]
        for h in range(N_HEADS):
            packed = jnp.where(lane == r * N_HEADS + h, lse[h * tq:(h + 1) * tq], packed)
        lse_ref[pl.ds(q_local, tq), :] = packed

    lse_ref[...] = jnp.zeros(lse_ref.shape, F32)
    units = (o_ref.shape[1] // MIX_W) * nq

    def body(i, carry):
        _interleave(*[unit(i * BAND_UNROLL + j) for j in range(BAND_UNROLL)])
        return carry

    lax.fori_loop(0, units // BAND_UNROLL, body, 0)


def _band(a, batch, seq, window, dil, tq, step_rows):
    half = window // (2 * dil)
    length = seq // dil
    tq = min(tq, length)
    kwin = min(tq + 2 * half, length)
    rows = min(step_rows, length) if dil == 1 else length
    view = a.reshape(batch, length, dil * 3 * MIX_W)
    o, lse = pl.pallas_call(
        functools.partial(_band_kernel, half=half, kwin=kwin, tq=tq),
        grid=(batch, length // rows),
        in_specs=[pl.BlockSpec((None, length, dil * 3 * MIX_W), lambda b, i: (b, 0, 0))],
        out_specs=[pl.BlockSpec((None, rows, dil * MIX_W), lambda b, i: (b, i, 0)),
                   pl.BlockSpec((None, rows, dil * N_HEADS), lambda b, i: (b, i, 0))],
        out_shape=[jax.ShapeDtypeStruct((batch, length, dil * MIX_W), BF16),
                   jax.ShapeDtypeStruct((batch, length, dil * N_HEADS), F32)],
        compiler_params=_params("parallel", "arbitrary"),
    )(view)
    return o.reshape(batch * length, dil * MIX_W), lse.reshape(batch * seq, N_HEADS)


CUM_ROWS = 256


def _chunk_tri(n, upper):
    r = lax.broadcasted_iota(jnp.int32, (n, n), 0)
    c = lax.broadcasted_iota(jnp.int32, (n, n), 1)
    same = (r // B_CHUNK) == (c // B_CHUNK)
    keep = (r <= c) if upper else (r >= c)
    return jnp.where(same & keep, 1.0, 0.0).astype(F32)


def _delta_prep_kernel(x_ref, prev_ref, next_ref, ab_ref, conv_ref, alog_ref, dtb_ref,
                       qkv_ref, gate_ref, xe_ref):
    j = pl.program_id(1)
    rows = x_ref.shape[0]
    pad = BF16_SUBLANES
    keep_prev = jnp.where(j > 0, 1.0, 0.0)
    keep_next = jnp.where(j < pl.num_programs(1) - 1, 1.0, 0.0)
    xe_ref[0:pad, :] = prev_ref[...].astype(F32) * keep_prev
    xe_ref[pad:pad + rows, :] = x_ref[...].astype(F32)
    xe_ref[pad + rows:, :] = next_ref[...].astype(F32) * keep_next
    centre = B_CONV // 2
    y = jnp.zeros((rows, 3 * MIX_W), F32)
    for tap in range(B_CONV):
        y = y + xe_ref[pad + tap - centre:pad + tap - centre + rows, :] * conv_ref[tap:tap + 1, :]
    y = y * jax.nn.sigmoid(y)
    q, k = y[:, 0:MIX_W], y[:, MIX_W:2 * MIX_W]
    q = q * lax.rsqrt(_head_sums(q * q) + 1e-6) * (HEAD_DIM ** -0.5)
    k = k * lax.rsqrt(_head_sums(k * k) + 1e-6)
    qkv_ref[:, 0:MIX_W] = q.astype(BF16)
    qkv_ref[:, MIX_W:2 * MIX_W] = k.astype(BF16)
    qkv_ref[:, 2 * MIX_W:] = y[:, 2 * MIX_W:].astype(BF16)

    ab = ab_ref[...]
    col = lax.broadcasted_iota(jnp.int32, (1, ab.shape[-1]), 1)
    xs = ab + dtb_ref[...]
    softplus = jnp.maximum(xs, 0.0) + jnp.log(1.0 + jnp.exp(-jnp.abs(xs)))
    g = -jnp.exp(alog_ref[...]) * softplus
    beta = jax.nn.sigmoid(ab)
    tri_f, tri_b = _chunk_tri(CUM_ROWS, False).astype(BF16), _chunk_tri(CUM_ROWS, True).astype(BF16)
    g_hi = g.astype(BF16)
    g_mid = (g - g_hi.astype(F32)).astype(BF16)
    g_lo = (g - g_hi.astype(F32) - g_mid.astype(F32)).astype(BF16)
    for r0 in range(0, rows, CUM_ROWS):
        pieces = [p[r0:r0 + CUM_ROWS] for p in (g_hi, g_mid, g_lo)]
        cum_f = sum(jnp.dot(tri_f, p, preferred_element_type=F32) for p in pieces)
        cum_b = sum(jnp.dot(tri_b, p, preferred_element_type=F32) for p in pieces)
        gate_ref[r0:r0 + CUM_ROWS, :] = jnp.where(col < 8, beta[r0:r0 + CUM_ROWS],
                                                   jnp.where(col < 12, cum_f, cum_b))


def _delta_prep(bqkv, ab, lw, batch, seq, rows):
    t = batch * seq
    nblk = seq // rows
    per = rows // BF16_SUBLANES
    last = t // BF16_SUBLANES - 1
    w3 = 3 * MIX_W
    return pl.pallas_call(
        _delta_prep_kernel,
        grid=(batch, nblk),
        in_specs=[pl.BlockSpec((rows, w3), lambda b, j: (b * nblk + j, 0)),
                  pl.BlockSpec((BF16_SUBLANES, w3), lambda b, j: (jnp.maximum((b * nblk + j) * per - 1, 0), 0)),
                  pl.BlockSpec((BF16_SUBLANES, w3), lambda b, j: (jnp.minimum((b * nblk + j + 1) * per, last), 0)),
                  pl.BlockSpec((rows, 16), lambda b, j: (b * nblk + j, 0)),
                  _const_spec(lw["conv_w"].shape), _const_spec((1, 16)), _const_spec((1, 16))],
        out_specs=[pl.BlockSpec((rows, w3), lambda b, j: (b * nblk + j, 0)),
                   pl.BlockSpec((rows, 16), lambda b, j: (b * nblk + j, 0))],
        out_shape=[jax.ShapeDtypeStruct((t, w3), BF16), jax.ShapeDtypeStruct((t, 16), F32)],
        scratch_shapes=[pltpu.VMEM((rows + 2 * BF16_SUBLANES, w3), F32)],
        compiler_params=_params("parallel", "arbitrary"),
    )(bqkv, bqkv, bqkv, ab, lw["conv_w"], lw["a_log16"], lw["dt_bias16"])


def _expand_heads(cols):
    ids = _head_ids(MIX_W)
    out = jnp.broadcast_to(cols[:, 0:1], (cols.shape[0], MIX_W))
    for h in range(1, N_HEADS):
        out = jnp.where(ids == h, jnp.broadcast_to(cols[:, h:h + 1], out.shape), out)
    return out


def _block_diag(packed, mask):
    return jnp.where(mask, jnp.concatenate([packed] * N_HEADS, axis=0), 0.0).astype(BF16)


def _mm(a, b):
    return jnp.dot(a.astype(BF16), b.astype(BF16), preferred_element_type=F32)


def _delta_local(s, backward, qkv_ref, gate_ref, grow_ref, u_ref, w_ref, qh_ref, kt_ref, in_ref, eg_ref, c):
    n = B_CHUNK
    d = 2 * s + (1 if backward else 0)
    r0 = pl.multiple_of(c * n, n)
    q = qkv_ref[s, pl.ds(r0, n), 0:MIX_W]
    k = qkv_ref[s, pl.ds(r0, n), MIX_W:2 * MIX_W]
    v = qkv_ref[s, pl.ds(r0, n), 2 * MIX_W:].astype(F32)
    gates = gate_ref[s, pl.ds(r0, n), :]
    d0 = 4 if backward else 0
    beta = _expand_heads(gates[:, d0:d0 + 4])
    gc = _expand_heads(gates[:, 8 + d0:12 + d0])
    gc_row = grow_ref[s, pl.ds(c, 1), d0 * n:(d0 + 4) * n]

    row = lax.broadcasted_iota(jnp.int32, (n, MIX_W), 0)
    col = lax.broadcasted_iota(jnp.int32, (n, MIX_W), 1) % n
    incl = (row <= col) if backward else (row >= col)
    strict = (row < col) if backward else (row > col)
    bd_mask = (lax.broadcasted_iota(jnp.int32, (MIX_W, MIX_W), 0) // n
               == lax.broadcasted_iota(jnp.int32, (MIX_W, MIX_W), 1) // n)
    decay = jnp.exp(jnp.where(incl, gc - gc_row, NEG_INF))

    kf = k.astype(F32)
    kq = lax.dot_general(jnp.concatenate([q, k], axis=0), _stack_heads(k), (((1,), (1,)), ((), ())),
                         preferred_element_type=F32)
    yield
    intra = kq[0:n] * decay
    low = jnp.where(strict, kq[n:] * beta * decay, 0.0)

    x = jnp.where(row == col, 1.0, 0.0) - low
    p = _mm(low, _block_diag(low, bd_mask))
    yield
    steps = int(math.log2(n)) - 1
    for step in range(steps):
        last = step == steps - 1
        lhs = x if last else jnp.concatenate([x, p], axis=0)
        r = _mm(lhs, _block_diag(p, bd_mask))
        yield
        x = x + r[0:n]
        if not last:
            p = r[n:]

    kbeta = kf * beta
    u_ref[d, pl.ds(r0, n), :] = _mm(x, _block_diag(v * beta, bd_mask))
    w_ref[d, pl.ds(r0, n), :] = _mm(x, _block_diag(kbeta * jnp.exp(gc), bd_mask)).astype(BF16)
    edge = gc[0:1] if backward else gc[n - 1:n]
    qh_ref[d, pl.ds(r0, n), :] = (q.astype(F32) * jnp.exp(gc)).astype(BF16)
    kt_ref[d, pl.ds(r0, n), :] = (kf * jnp.exp(edge - gc)).astype(BF16)
    in_ref[d, pl.ds(r0, n), :] = intra.astype(BF16)
    eg_ref[d, pl.ds(c, 1), :] = jnp.exp(edge)


def _delta_step(s, backward, u_ref, w_ref, qh_ref, kt_ref, in_ref, eg_ref, o_ref, s_ref, c):
    n = B_CHUNK
    d = 2 * s + (1 if backward else 0)
    r0 = pl.multiple_of(c * n, n)
    bd_mask = (lax.broadcasted_iota(jnp.int32, (MIX_W, MIX_W), 0) // n
               == lax.broadcasted_iota(jnp.int32, (MIX_W, MIX_W), 1) // n)
    state = s_ref[d]
    lhs = jnp.concatenate([w_ref[d, pl.ds(r0, n), :], qh_ref[d, pl.ds(r0, n), :]], axis=0)
    ws_qs = jnp.dot(lhs, state.astype(BF16), preferred_element_type=F32)
    yield
    v_new = u_ref[d, pl.ds(r0, n), :] - ws_qs[0:n]
    o_ref[s, pl.ds(r0, n), :] = ws_qs[n:] + jnp.dot(in_ref[d, pl.ds(r0, n), :], _block_diag(v_new, bd_mask),
                                                  preferred_element_type=F32)
    kv = lax.dot_general(kt_ref[d, pl.ds(r0, n), :], v_new.astype(BF16), (((0,), (0,)), ((), ())),
                         preferred_element_type=F32)
    s_ref[d] = state * eg_ref[d, pl.ds(c, 1), :] + jnp.where(bd_mask, kv, 0.0)


LOCAL_UNROLL = 8


def _delta_kernel(qkv_f, gate_f, grow_f, qkv_b, gate_b, grow_b, of_ref, ob_ref,
                  s_ref, u_ref, w_ref, qh_ref, kt_ref, in_ref, eg_ref):
    @pl.when(pl.program_id(1) == 0)
    def _():
        s_ref[...] = jnp.zeros(s_ref.shape, F32)

    nseq = qkv_f.shape[0]
    nchunk = qkv_f.shape[1] // B_CHUNK
    staged = (u_ref, w_ref, qh_ref, kt_ref, in_ref, eg_ref)

    for s in range(nseq):
        def local(i, carry, s=s):
            chunks = [i * LOCAL_UNROLL + g for g in range(LOCAL_UNROLL)]
            _interleave(*[_delta_local(s, False, qkv_f, gate_f, grow_f, *staged, c) for c in chunks],
                        *[_delta_local(s, True, qkv_b, gate_b, grow_b, *staged, c) for c in chunks])
            return carry

        lax.fori_loop(0, nchunk // LOCAL_UNROLL, local, 0)

    def step(i, carry):
        _interleave(*[_delta_step(s, False, *staged, of_ref, s_ref, i) for s in range(nseq)],
                    *[_delta_step(s, True, *staged, ob_ref, s_ref, nchunk - 1 - i) for s in range(nseq)])
        return carry

    lax.fori_loop(0, nchunk, step, 0)


def _delta(qkvn, gates, batch, seq, rows, nseq):
    t = batch * seq
    nblk = seq // rows
    nchunk = rows // B_CHUNK
    grow = gates[:, 8:16].reshape(t // B_CHUNK, B_CHUNK, 8).transpose(0, 2, 1).reshape(t // B_CHUNK, 8 * B_CHUNK)
    group = lambda a, n: a.reshape(batch // nseq, nseq, n, a.shape[-1])
    qkvn4, gates4, grow4 = group(qkvn, seq), group(gates, seq), group(grow, seq // B_CHUNK)
    fwd = lambda b, j: (b, 0, j, 0)
    bwd = lambda b, j: (b, 0, nblk - 1 - j, 0)
    blk = lambda n, w, m: pl.BlockSpec((None, nseq, n, w), m)
    specs = lambda m: [blk(rows, 3 * MIX_W, m), blk(rows, 16, m), blk(nchunk, 8 * B_CHUNK, m)]
    slots = 2 * nseq
    of, ob = pl.pallas_call(
        _delta_kernel,
        grid=(batch // nseq, nblk),
        in_specs=specs(fwd) + specs(bwd),
        out_specs=[blk(rows, MIX_W, fwd), blk(rows, MIX_W, bwd)],
        out_shape=[jax.ShapeDtypeStruct((batch // nseq, nseq, seq, MIX_W), F32)] * 2,
        scratch_shapes=([pltpu.VMEM((slots, MIX_W, MIX_W), F32), pltpu.VMEM((slots, rows, MIX_W), F32)]
                        + [pltpu.VMEM((slots, rows, MIX_W), BF16)] * 4 + [pltpu.VMEM((slots, nchunk, MIX_W), F32)]),
        compiler_params=_params("parallel", "arbitrary"),
    )(qkvn4, gates4, grow4, qkvn4, gates4, grow4)
    return of.reshape(t, MIX_W), ob.reshape(t, MIX_W)


def _merge_kernel(x_ref, gain_ref, oa0, oa1, oa2, la0, la1, la2, bf_ref, bb_ref, z_ref, bgain_ref,
                  oc_ref, od_ref, wg_ref, bg_ref, wbr_ref, wout_ref, out_ref, o_scr):
    x = x_ref[...]
    tm = x.shape[0]
    h = _rms_rows(x, gain_ref[...]).astype(BF16)

    slabs = MIX_W // LANES
    outs_a = [oa0[...].astype(F32)]
    for p, ((_, dil), oa) in enumerate(zip(A_PATTERNS[1:], (oa1, oa2))):
        for r in range(dil):
            for c in range(slabs):
                col = r * MIX_W + c * LANES
                o_scr[p, c, pl.ds(r, tm // dil, stride=dil), :] = oa[:, col:col + LANES].astype(F32)
        outs_a.append(jnp.concatenate([o_scr[p, c] for c in range(slabs)], axis=1))

    lses = [_expand_heads(r[...]) for r in (la0, la1, la2)]
    top = jnp.maximum(jnp.maximum(lses[0], lses[1]), lses[2])
    ws = [jnp.exp(l - top) for l in lses]
    o_a = sum(w * o for w, o in zip(ws, outs_a)) / (ws[0] + ws[1] + ws[2])

    o_b = bf_ref[...] + bb_ref[...]
    z = z_ref[...].astype(F32)
    o_b = (o_b * lax.rsqrt(_head_sums(o_b * o_b) * (1.0 / HEAD_DIM) + RMS_EPS) * bgain_ref[...]
           * (z * jax.nn.sigmoid(z)))

    branches = (o_a, o_b, oc_ref[...], od_ref[...])
    y = jnp.zeros(x.shape, F32)
    for i, o in enumerate(branches):
        gate = jax.nn.sigmoid(jnp.dot(h, wg_ref[i], preferred_element_type=F32) + bg_ref[i])
        y = y + gate * jnp.dot(o.astype(BF16), wbr_ref[i], preferred_element_type=F32)
    out_ref[...] = x + jnp.dot(y.astype(BF16), wout_ref[...], preferred_element_type=F32)


def _merge(x2d, a_outs, b_outs, z, oc, od, lw, tm):
    t = x2d.shape[0]
    tok = lambda w: pl.BlockSpec((tm, w), lambda i: (i, 0))
    (o0, l0), (o1, l1), (o2, l2) = a_outs
    acts = (o0, o1, o2, l0, l1, l2, b_outs[0], b_outs[1], z)
    weights = (lw["wg"], lw["bg"], lw["wbr"], lw["wout"])
    return pl.pallas_call(
        _merge_kernel,
        grid=(t // tm,),
        in_specs=([tok(D_MODEL), _const_spec(lw["norm_mix"].shape)]
                  + [pl.BlockSpec((tm * a.shape[0] // t, a.shape[1]), lambda i: (i, 0)) for a in acts]
                  + [_const_spec(lw["b_gain"].shape), tok(MIX_W), tok(MIX_W)]
                  + [_const_spec(w.shape) for w in weights]),
        out_specs=tok(D_MODEL),
        out_shape=jax.ShapeDtypeStruct((t, D_MODEL), F32),
        scratch_shapes=[pltpu.VMEM((len(A_PATTERNS) - 1, MIX_W // LANES, tm, LANES), F32)],
        compiler_params=_params("parallel"),
    )(x2d, lw["norm_mix"], *acts, lw["b_gain"], oc, od, *weights)


EXPERT_UNROLL = 2


def _moe_kernel(x_ref, gain_ref, wr_ref, br_ref, wgt_ref, wup_ref, wdn_ref, fgain_ref, out_ref,
                hs_ref, ws_ref, ys_ref, *, final, cap):
    x = x_ref[...]
    tm = x.shape[0]
    h = _rms_rows(x, gain_ref[...])
    hb = h.astype(BF16)
    h_lo = (h - hb.astype(F32)).astype(BF16)
    parts = jnp.dot(jnp.concatenate([hb, h_lo], axis=0), wr_ref[...], preferred_element_type=F32)
    logits = (parts[:tm, :LANES] + parts[:tm, LANES:]) + (parts[tm:, :LANES] + parts[tm:, LANES:]) + br_ref[...]
    lane = lax.broadcasted_iota(jnp.int32, (1, LANES), 1)
    none = LANES

    def first_max(vals):
        top = jnp.max(vals, axis=-1, keepdims=True)
        return top, jnp.min(jnp.where(vals == top, lane, none), axis=-1, keepdims=True)

    gl = jnp.where(lane < N_GROUPS, logits, NEG_INF)
    g_top, g_idx = first_max(gl)
    g_prob = 1.0 / jnp.sum(jnp.exp(gl - g_top), axis=-1, keepdims=True)
    in_group = (lane >= N_GROUPS) & (lane < N_GROUPS + N_EXPERTS) & ((lane - N_GROUPS) // EXPERTS_PER_GROUP == g_idx)
    el = jnp.where(in_group, logits, NEG_INF)
    top1, idx1 = first_max(el)
    top2, idx2 = first_max(jnp.where(lane == idx1, NEG_INF, el))
    e2 = jnp.exp(top2 - top1)
    w1 = g_prob / (1.0 + e2)
    weights = jnp.where(lane == idx1, w1, 0.0) + jnp.where(lane == idx2, w1 * e2, 0.0)

    def expert(rows, row_weights, e, outs):
        w_e = jnp.sum(jnp.where(lane == e + N_GROUPS, row_weights, 0.0), axis=-1, keepdims=True)
        gate = jnp.dot(rows, wgt_ref[e], preferred_element_type=F32)
        up = jnp.dot(rows, wup_ref[e], preferred_element_type=F32)
        yield
        hid = gate * jax.nn.sigmoid(gate) * up * w_e
        outs.append(jnp.dot(hid.astype(BF16), wdn_ref[e], preferred_element_type=F32))

    def finish(moe_out):
        y = x + moe_out
        if final:
            y = _rms_rows(y, fgain_ref[...])
        out_ref[...] = y

    member = jnp.where(lane == g_idx, 1.0, 0.0)
    fits = jnp.max(jnp.sum(member, axis=0, keepdims=True)) <= cap

    @pl.when(fits)
    def _():
        before = lax.broadcasted_iota(jnp.int32, (tm, tm), 0) > lax.broadcasted_iota(jnp.int32, (tm, tm), 1)
        ahead = jnp.dot(jnp.where(before, 1.0, 0.0).astype(BF16), member.astype(BF16),
                        preferred_element_type=F32)
        pos = jnp.sum(member * ahead, axis=-1, keepdims=True).astype(jnp.int32)
        slot = g_idx * cap + pos
        perm_t = jnp.where(lax.broadcasted_iota(jnp.int32, (1, N_GROUPS * cap), 1) == slot, 1.0, 0.0).astype(BF16)
        slot_row = jnp.broadcast_to(slot.astype(F32), (tm, LANES)).T[0:1]
        perm = jnp.where(lax.broadcasted_iota(jnp.int32, (N_GROUPS * cap, 1), 0).astype(F32) == slot_row,
                         1.0, 0.0).astype(BF16)
        gather = lambda a: jnp.dot(perm, a, preferred_element_type=F32)
        hs_ref[...] = gather(hb).astype(BF16)
        w_hi = weights.astype(BF16)
        w_parts = gather(jnp.concatenate([w_hi, (weights - w_hi.astype(F32)).astype(BF16)], axis=1))
        ws_ref[...] = w_parts[:, :LANES] + w_parts[:, LANES:]

        def group(g, carry):
            r0 = pl.multiple_of(g * cap, cap)
            rows, row_weights = hs_ref[pl.ds(r0, cap), :], ws_ref[pl.ds(r0, cap), :]
            outs = []
            _interleave(*[expert(rows, row_weights, g * EXPERTS_PER_GROUP + j, outs)
                          for j in range(EXPERTS_PER_GROUP)])
            ys_ref[pl.ds(r0, cap), :] = sum(outs[1:], outs[0]).astype(BF16)
            return carry

        lax.fori_loop(0, N_GROUPS, group, 0)
        finish(jnp.dot(perm_t, ys_ref[...], preferred_element_type=F32))

    @pl.when(jnp.logical_not(fits))
    def _():
        def experts(i, acc):
            outs = []
            _interleave(*[expert(hb, weights, i * EXPERT_UNROLL + j, outs) for j in range(EXPERT_UNROLL)])
            return acc + sum(outs[1:], outs[0])

        finish(lax.fori_loop(0, N_EXPERTS // EXPERT_UNROLL, experts, jnp.zeros(x.shape, F32)))


def _moe(x2d, lw, final_gain, final, tm, cap):
    t = x2d.shape[0]
    tok = pl.BlockSpec((tm, D_MODEL), lambda i: (i, 0))
    once = lambda a: pl.BlockSpec(a.shape, lambda i: (0,) * a.ndim, pipeline_mode=pl.Buffered(1))
    weights = (lw["norm_ffn"], lw["w_route"], lw["b_route"], lw["w_ff_gate"], lw["w_ff_up"], lw["w_ff_down"],
               final_gain)
    return pl.pallas_call(
        functools.partial(_moe_kernel, final=final, cap=cap),
        grid=(t // tm,),
        in_specs=[tok] + [once(w) for w in weights],
        out_specs=tok,
        out_shape=jax.ShapeDtypeStruct((t, D_MODEL), F32),
        scratch_shapes=[pltpu.VMEM((N_GROUPS * cap, D_MODEL), BF16), pltpu.VMEM((N_GROUPS * cap, LANES), F32),
                        pltpu.VMEM((N_GROUPS * cap, D_MODEL), BF16)],
        compiler_params=_params("parallel"),
    )(x2d, *weights)


def _split_bf16(w):
    hi = w.astype(BF16)
    return jnp.concatenate([hi, (w - hi.astype(F32)).astype(BF16)], axis=1)


def _layer_weights(p, l):
    w_in = p["w_in"][l]
    offs = np.cumsum([0, 3 * MIX_W, 3 * MIX_W, MIX_W, 16, MIX_W, 2 * C_KV_HEADS * HEAD_DIM, 3 * MIX_W])
    sec = [w_in[:, offs[i]:offs[i + 1]] for i in range(7)]
    c_order = np.concatenate([np.arange(h * HEAD_DIM, (h + 1) * HEAD_DIM) for h in (0, 2, 1, 3)])
    row = lambda v: v.reshape(1, -1).astype(F32)
    gate16 = lambda v: jnp.concatenate([jnp.zeros((8,), F32), v.reshape(-1).astype(F32)]).reshape(1, 16)
    wbr = p["w_branch"][l]
    wbr = jnp.stack([wbr[0], wbr[1], wbr[2][c_order], wbr[3]])
    w_route = jnp.concatenate([p["w_route_group"][l], p["w_route_expert"][l].reshape(D_MODEL, N_EXPERTS)], axis=1)
    b_route = jnp.concatenate([p["b_route_group"][l], p["b_route_expert"][l].reshape(N_EXPERTS)])
    pad_lanes = lambda a: jnp.pad(a, [(0, 0)] * (a.ndim - 1) + [(0, LANES - a.shape[-1])])
    return dict(
        norm_mix=row(p["norm_mix"][l]),
        wa=sec[0].astype(BF16), wb=sec[1].astype(BF16), wz=sec[2].astype(BF16),
        wab=pad_lanes(sec[3]).astype(BF16), wcq=sec[4][:, c_order].astype(BF16), wck=sec[5][:, :C_KV_HEADS * HEAD_DIM].astype(BF16), wcvt=sec[5][:, C_KV_HEADS * HEAD_DIM:].T.astype(BF16),
        wd=sec[6].astype(BF16),
        cq_gain=row(jnp.tile(p["c_q_norm"][l], N_HEADS)), ck_gain=row(jnp.tile(p["c_k_norm"][l], C_KV_HEADS)),
        conv_w=p["conv_w"][l].astype(F32), a_log16=gate16(p["a_log"][l]), dt_bias16=gate16(p["dt_bias"][l]),
        b_gain=row(jnp.tile(p["b_norm"][l], N_HEADS)),
        wg=p["w_gate"][l].astype(BF16), bg=p["b_gate"][l].reshape(N_HEADS, 1, D_MODEL).astype(F32),
        wbr=wbr.astype(BF16), wout=p["w_out"][l].astype(BF16),
        norm_ffn=row(p["norm_ffn"][l]),
        w_route=_split_bf16(pad_lanes(w_route).astype(F32)), b_route=pad_lanes(b_route.reshape(1, -1)).astype(F32),
        w_ff_gate=p["w_ff_gate"][l].reshape(N_EXPERTS, D_MODEL, D_EXPERT).astype(BF16),
        w_ff_up=p["w_ff_up"][l].reshape(N_EXPERTS, D_MODEL, D_EXPERT).astype(BF16),
        w_ff_down=p["w_ff_down"][l].reshape(N_EXPERTS, D_EXPERT, D_MODEL).astype(BF16),
        d_rpb=p["d_rpb"][l],
    )


def _tiles(seq):
    tm = min(512, seq)
    share = tm // N_GROUPS
    moe_cap = (share + share // 4 + BF16_SUBLANES - 1) // BF16_SUBLANES * BF16_SUBLANES
    return dict(tm=tm, moe_cap=moe_cap, tq_c=min(256, seq), tk_c=min(512, seq), rows_d=min(8, seq // GRID_W),
                tq_a=128, rows_a=2048, rows_prep=min(512, seq), rows_delta=min(1024, seq))


def _encoder(x, layers, final_gain):
    batch, seq, _ = x.shape
    ts = _tiles(seq)
    tables = _rotary_tables(seq)
    depth = len(layers)
    x2d = x.reshape(batch * seq, D_MODEL)
    for l, lw in enumerate(layers):
        a_views, (bqkv, z, ab, cq, ck, cvt, dqkv) = _in_proj(x2d, seq, lw, tables, ts["tm"])
        a_outs = [_band(a, batch, seq, window, dil, ts["tq_a"], ts["rows_a"])
                  for a, (window, dil) in zip(a_views, A_PATTERNS)]
        qkvn, gates = _delta_prep(bqkv, ab, lw, batch, seq, ts["rows_prep"])
        b_outs = _delta(qkvn, gates, batch, seq, ts["rows_delta"], 2 if batch % 2 == 0 else 1)
        oc = _gqa(cq, ck, cvt, batch, seq, ts["tq_c"], ts["tk_c"])
        od = _nbr(dqkv, _nbr_bias(lw["d_rpb"], min(NA_ROWS, seq // GRID_W)), batch, seq, ts["rows_d"])
        x2d = _merge(x2d, a_outs, b_outs, z, oc, od, lw, ts["tm"])
        x2d = _moe(x2d, lw, final_gain, l == depth - 1, ts["tm"], ts["moe_cap"])
    return x2d.reshape(batch, seq, D_MODEL)


def kernel(x_prompt, x_sample, norm_mix, w_in, conv_w, a_log, dt_bias, b_norm, c_q_norm, c_k_norm, d_rpb,
           w_gate, b_gate, w_branch, w_out, norm_ffn, w_route_group, b_route_group, w_route_expert,
           b_route_expert, w_ff_gate, w_ff_up, w_ff_down, norm_final):
    p = dict(norm_mix=norm_mix, w_in=w_in, conv_w=conv_w, a_log=a_log, dt_bias=dt_bias, b_norm=b_norm,
             c_q_norm=c_q_norm, c_k_norm=c_k_norm, d_rpb=d_rpb, w_gate=w_gate, b_gate=b_gate,
             w_branch=w_branch, w_out=w_out, norm_ffn=norm_ffn, w_route_group=w_route_group,
             b_route_group=b_route_group, w_route_expert=w_route_expert, b_route_expert=b_route_expert,
             w_ff_gate=w_ff_gate, w_ff_up=w_ff_up, w_ff_down=w_ff_down)
    layers = [_layer_weights(p, l) for l in range(norm_mix.shape[0])]
    final_gain = norm_final.reshape(1, -1).astype(F32)
    return _encoder(x_prompt, layers, final_gain), _encoder(x_sample, layers, final_gain)
```

```python
import functools
import math

import jax
import jax.numpy as jnp
import numpy as np
from jax import lax
from jax.experimental import pallas as pl
from jax.experimental.pallas import tpu as pltpu

D_MODEL = 1024
HEAD_DIM = 64
N_HEADS = 4
MIX_W = N_HEADS * HEAD_DIM
GRID_W = 64
RMS_EPS = 1e-6
NEG_INF = -1e30
LOG2_E = math.log2(math.e)

A_PATTERNS = ((128, 1), (512, 4), (2048, 16))
ROPE_THETA = 500000.0
ROPE_DIMS = HEAD_DIM // 4
AXIAL_THETA = 10000.0
C_KV_HEADS = 2
B_CONV = 5
B_CHUNK = 64
NA_ROWS = 8
NA_COLS = 16
N_GROUPS = 4
EXPERTS_PER_GROUP = 4
N_EXPERTS = N_GROUPS * EXPERTS_PER_GROUP
D_EXPERT = 256

LANES = 128
BF16_SUBLANES = 16
HALO = 8
VMEM_LIMIT = 56 * 1024 * 1024

F32 = jnp.float32
BF16 = jnp.bfloat16


def _params(*sem):
    return pltpu.CompilerParams(dimension_semantics=sem, vmem_limit_bytes=VMEM_LIMIT)


def _const_spec(shape):
    n = len(shape)
    return pl.BlockSpec(shape, lambda *_: (0,) * n)


def _rms_rows(x, gain):
    return x * lax.rsqrt(jnp.mean(x * x, axis=-1, keepdims=True) + RMS_EPS) * gain


def _head_ids(width):
    return lax.broadcasted_iota(jnp.int32, (1, width), 1) // HEAD_DIM


def _block_ones(width):
    r = lax.broadcasted_iota(jnp.int32, (width, width), 0) // HEAD_DIM
    c = lax.broadcasted_iota(jnp.int32, (width, width), 1) // HEAD_DIM
    return jnp.where(r == c, 1.0, 0.0).astype(BF16)


def _head_sums(x):
    return jnp.dot(x.astype(BF16), _block_ones(x.shape[-1]), preferred_element_type=F32)


def _interleave(*chains):
    live = list(chains)
    while live:
        live = [c for c in live if next(c, live) is not live]


def _rotate(x, cos, sin_lo, sin_hi, shift):
    w = x.shape[-1]
    return x * cos + pltpu.roll(x, shift, 1) * sin_hi + pltpu.roll(x, w - shift, 1) * sin_lo


def _in_proj_kernel(x_ref, xprev_ref, xnext_ref, gain_ref, wa_ref, wb_ref, wz_ref, wab_ref, wcq_ref, wck_ref,
                    wcvt_ref, wd_ref, cqg_ref, ckg_ref, conv_ref, alog_ref, dtb_ref, rot_a_ref, rot_c_ref,
                    a1_ref, a4_ref, a16_ref, b_ref, z_ref, gate_ref, cq_ref, ck_ref, cvt_ref, d_ref,
                    a_scr, xe_ref, *, nblk):
    h = _rms_rows(x_ref[...], gain_ref[...]).astype(BF16)
    dot = lambda w_ref: jnp.dot(h, w_ref[...], preferred_element_type=F32)
    scale = HEAD_DIM ** -0.5
    tm = x_ref.shape[0]

    halo = jnp.concatenate([xprev_ref[...], xnext_ref[...]], axis=0)
    ub_halo = jnp.dot(_rms_rows(halo, gain_ref[...]).astype(BF16), wb_ref[...], preferred_element_type=F32)
    pos = pl.program_id(0) % nblk
    xe_ref[0:HALO, :] = ub_halo[0:HALO] * jnp.where(pos > 0, 1.0, 0.0)
    xe_ref[HALO:HALO + tm, :] = dot(wb_ref)
    xe_ref[HALO + tm:, :] = ub_halo[HALO:] * jnp.where(pos < nblk - 1, 1.0, 0.0)
    _delta_prep(xe_ref, dot(wab_ref)[:, :gate_ref.shape[-1]], conv_ref, alog_ref, dtb_ref, b_ref, gate_ref)

    ua = dot(wa_ref)
    cos, s_lo, s_hi = rot_a_ref[0], rot_a_ref[1], rot_a_ref[2]
    half = ROPE_DIMS // 2
    qkv_a = (_rotate(ua[:, 0:MIX_W], cos, s_lo, s_hi, half) * scale,
             _rotate(ua[:, MIX_W:2 * MIX_W], cos, s_lo, s_hi, half), ua[:, 2 * MIX_W:])
    a1_ref[...] = jnp.concatenate(qkv_a, axis=1).astype(BF16)
    slabs = 3 * MIX_W // LANES
    for c in range(slabs):
        part, off = divmod(c * LANES, MIX_W)
        a_scr[c] = qkv_a[part][:, off:off + LANES]
    for (_, dil), a_ref in zip(A_PATTERNS[1:], (a4_ref, a16_ref)):
        for r in range(dil):
            for c in range(slabs):
                col = r * 3 * MIX_W + c * LANES
                a_ref[:, col:col + LANES] = a_scr[c, pl.ds(r, tm // dil, stride=dil), :].astype(BF16)

    z_ref[...] = dot(wz_ref).astype(BF16)

    cos, s_lo, s_hi = rot_c_ref[0], rot_c_ref[1], rot_c_ref[2]
    quarter = HEAD_DIM // 4
    uq = dot(wcq_ref)
    uq = uq * lax.rsqrt(_head_sums(uq * uq) * (1.0 / HEAD_DIM) + RMS_EPS) * cqg_ref[...]
    cq_ref[...] = (_rotate(uq, cos, s_lo, s_hi, quarter) * (scale * LOG2_E)).astype(BF16)
    kw = C_KV_HEADS * HEAD_DIM
    uk = dot(wck_ref)
    uk = uk * lax.rsqrt(_head_sums(uk * uk) * (1.0 / HEAD_DIM) + RMS_EPS) * ckg_ref[...]
    ck_ref[...] = _rotate(uk, cos[:, :kw], s_lo[:, :kw], s_hi[:, :kw], quarter).astype(BF16)
    cvt_ref[...] = lax.dot_general(wcvt_ref[...], h, (((1,), (1,)), ((), ())),
                                   preferred_element_type=F32).astype(BF16)

    ud = dot(wd_ref)
    d_ref[:, 0:MIX_W] = (ud[:, 0:MIX_W] * scale).astype(BF16)
    d_ref[:, MIX_W:] = ud[:, MIX_W:].astype(BF16)


def _rotary_tables(seq):
    pos = np.arange(seq, dtype=np.float64)

    def table(pos_of_lane, theta, half, lane_in_group, active):
        freq_idx = lane_in_group % half
        inv = theta ** (-(freq_idx.astype(np.float64)) / half)
        ang = pos_of_lane * inv[None, :]
        cos = np.where(active[None, :], np.cos(ang), 1.0)
        sin = np.where(active[None, :], np.sin(ang), 0.0)
        low = (lane_in_group < half)[None, :]
        return np.stack([cos, np.where(low, -sin, 0.0), np.where(low, 0.0, sin)]).astype(np.float32)

    lane = np.arange(MIX_W) % HEAD_DIM
    rot_a = table(np.broadcast_to(pos[:, None], (seq, MIX_W)), ROPE_THETA, ROPE_DIMS // 2, lane % ROPE_DIMS,
                  lane < ROPE_DIMS)
    rows, cols = pos // GRID_W, pos % GRID_W
    half = HEAD_DIM // 2
    pos_c = np.where((lane < half)[None, :], rows[:, None], cols[:, None])
    rot_c = table(pos_c, AXIAL_THETA, half // 2, lane % half, np.ones(MIX_W, bool))
    return jnp.asarray(rot_a), jnp.asarray(rot_c)


def _in_proj(x2d, seq, lw, tables, tm):
    t = x2d.shape[0]
    rot_a, rot_c = tables
    nblk = seq // tm
    tok = lambda w: pl.BlockSpec((tm, w), lambda i: (i, 0))
    rot = pl.BlockSpec((3, tm, MIX_W), lambda i: (0, i % nblk, 0))
    weights = (lw["norm_mix"], lw["wa"], lw["wb"], lw["wz"], lw["wab"], lw["wcq"], lw["wck"], lw["wcvt"], lw["wd"],
               lw["cq_gain"], lw["ck_gain"], lw["conv_w"], lw["a_log16"], lw["dt_bias16"])
    kw = C_KV_HEADS * HEAD_DIM
    per = tm // HALO
    halo_specs = [pl.BlockSpec((HALO, D_MODEL), lambda i: (jnp.maximum(i * per - 1, 0), 0)),
                  pl.BlockSpec((HALO, D_MODEL), lambda i: (jnp.minimum((i + 1) * per, t // HALO - 1), 0))]
    outs = ((3 * MIX_W, BF16), (MIX_W, BF16), (16, F32), (MIX_W, BF16), (kw, BF16), None, (3 * MIX_W, BF16))
    out_specs = [pl.BlockSpec((kw, tm), lambda i: (0, i)) if o is None else tok(o[0]) for o in outs]
    out_shape = [jax.ShapeDtypeStruct((kw, t), BF16) if o is None else jax.ShapeDtypeStruct((t, o[0]), o[1])
                 for o in outs]
    a_specs = [pl.BlockSpec((tm // dil, dil * 3 * MIX_W), lambda i: (i, 0)) for _, dil in A_PATTERNS]
    a_shape = [jax.ShapeDtypeStruct((t // dil, dil * 3 * MIX_W), BF16) for _, dil in A_PATTERNS]
    res = pl.pallas_call(
        functools.partial(_in_proj_kernel, nblk=nblk),
        grid=(t // tm,),
        in_specs=[tok(D_MODEL)] + halo_specs + [_const_spec(w.shape) for w in weights] + [rot, rot],
        out_specs=a_specs + out_specs,
        out_shape=a_shape + out_shape,
        scratch_shapes=[pltpu.VMEM((3 * MIX_W // LANES, tm, LANES), F32),
                        pltpu.VMEM((tm + 2 * HALO, 3 * MIX_W), F32)],
        compiler_params=_params("parallel"),
    )(x2d, x2d, x2d, *weights, rot_a, rot_c)
    return res[:len(A_PATTERNS)], res[len(A_PATTERNS):]


def _gqa_kernel(q_ref, k_ref, vt_ref, o_ref, acc_ref, s_ref, *, tk):
    tq = q_ref.shape[0]
    seq = k_ref.shape[0]
    kw = C_KV_HEADS * HEAD_DIM
    qt = q_ref[...].astype(F32).T
    row_g = lax.broadcasted_iota(jnp.int32, (kw, 1), 0) // HEAD_DIM
    qs = jnp.concatenate([jnp.where(row_g == g, qt[blk * kw:(blk + 1) * kw], 0.0)
                          for g in range(C_KV_HEADS) for blk in range(2)], axis=1).astype(BF16)
    acc_ref[...] = jnp.zeros(acc_ref.shape, F32)

    width = 2 * tq

    nchunk = seq // tk

    def scores(j, slot):
        start = pl.multiple_of(j * tk, tk)
        s_ref[slot] = jnp.dot(k_ref[pl.ds(start, tk), :], qs, preferred_element_type=F32)

    def kv_head(g, j, slot, m_old, l_old, out):
        cols = slice(g * width, (g + 1) * width)
        start = pl.multiple_of(j * tk, tk)
        s = s_ref[slot, :, cols]
        m_new = jnp.maximum(m_old, jnp.max(s, axis=0, keepdims=True))
        alpha = jnp.exp2(m_old - m_new)
        p = jnp.exp2(s - m_new)
        pv = jnp.dot(vt_ref[g * HEAD_DIM:(g + 1) * HEAD_DIM, pl.ds(start, tk)], p.astype(BF16),
                     preferred_element_type=F32)
        yield
        acc_ref[:, cols] = acc_ref[:, cols] * alpha + pv
        out[g] = (m_new, l_old * alpha + jnp.sum(p, axis=0, keepdims=True))

    def softmax_chunk(j, slot, carry):
        out = [None] * C_KV_HEADS
        _interleave(*[kv_head(g, j, slot, *carry[g], out) for g in range(C_KV_HEADS)])
        return tuple(out)

    def chunk_pair(i, carry, last):
        scores(2 * i + 1, 1)
        carry = softmax_chunk(2 * i, 0, carry)
        if not last:
            scores(2 * i + 2, 0)
        return softmax_chunk(2 * i + 1, 1, carry)

    scores(0, 0)
    carry = ((jnp.full((1, width), NEG_INF, F32), jnp.zeros((1, width), F32)),) * C_KV_HEADS
    if nchunk == 1:
        carry = softmax_chunk(0, 0, carry)
    else:
        assert nchunk % 2 == 0
        carry = lax.fori_loop(0, nchunk // 2 - 1, lambda i, c: chunk_pair(i, c, False), carry)
        carry = chunk_pair(nchunk // 2 - 1, carry, True)
    o = acc_ref[...] / jnp.concatenate([l for _, l in carry], axis=1)
    for blk in range(2):
        pair = jnp.concatenate([o[:, (2 * g + blk) * tq:(2 * g + blk + 1) * tq] for g in range(C_KV_HEADS)], axis=0)
        o_ref[:, blk * kw:(blk + 1) * kw] = pair.T.astype(o_ref.dtype)


def _gqa(cq, ck, cvt, batch, seq, tq, tk):
    nq = seq // tq
    kw = C_KV_HEADS * HEAD_DIM
    return pl.pallas_call(
        functools.partial(_gqa_kernel, tk=tk),
        grid=(batch, nq),
        in_specs=[pl.BlockSpec((tq, MIX_W), lambda b, i: (b * nq + i, 0)),
                  pl.BlockSpec((seq, kw), lambda b, i: (b, 0)),
                  pl.BlockSpec((kw, seq), lambda b, i: (0, b))],
        out_specs=pl.BlockSpec((tq, MIX_W), lambda b, i: (b * nq + i, 0)),
        out_shape=jax.ShapeDtypeStruct((batch * seq, MIX_W), BF16),
        scratch_shapes=[pltpu.VMEM((HEAD_DIM, 4 * tq), F32), pltpu.VMEM((2, tk, 4 * tq), F32)],
        compiler_params=_params("parallel", "arbitrary"),
    )(cq, ck, cvt)


def _stack_heads(q):
    ids = _head_ids(MIX_W)
    zero = jnp.zeros((), q.dtype)
    return jnp.concatenate([jnp.where(ids == h, q, zero) for h in range(N_HEADS)], axis=0)


def _unstack_heads(o, n):
    ids = _head_ids(MIX_W)
    out = o[0:n]
    for h in range(1, N_HEADS):
        out = jnp.where(ids == h, o[h * n:(h + 1) * n], out)
    return out


NBR_UNROLL = 4


def _nbr_kernel(q_ref, k_ref, v_ref, bias_ref, o_ref, *, rows_per_step, grid_rows, kr):
    j = pl.program_id(1)
    win = kr * GRID_W

    def query_row(i):
        r = j * rows_per_step + i
        rs = jnp.clip(r - kr // 2, 0, grid_rows - kr)
        kstart = pl.multiple_of(rs * GRID_W, GRID_W)
        qstart = pl.multiple_of(i * GRID_W, GRID_W)
        qs = _stack_heads(q_ref[pl.ds(qstart, GRID_W), :])
        s = lax.dot_general(qs, k_ref[pl.ds(kstart, win), :], (((1,), (1,)), ((), ())),
                            preferred_element_type=F32)
        yield
        s = s + bias_ref[r - rs]
        m = jnp.max(s, axis=-1, keepdims=True)
        p = jnp.exp(s - m)
        l = jnp.sum(p, axis=-1, keepdims=True)
        o = jnp.dot(p.astype(BF16), v_ref[pl.ds(kstart, win), :], preferred_element_type=F32)
        yield
        o_ref[pl.ds(qstart, GRID_W), :] = _unstack_heads(o / l, GRID_W).astype(o_ref.dtype)

    def body(i, carry):
        _interleave(*[query_row(i * NBR_UNROLL + u) for u in range(NBR_UNROLL)])
        return carry

    lax.fori_loop(0, rows_per_step // NBR_UNROLL, body, 0)


def _nbr_bias(rpb, kr):
    c = np.arange(GRID_W)
    cs = np.clip(c - NA_COLS // 2, 0, GRID_W - NA_COLS)
    col_valid = (c[None, :] >= cs[:, None]) & (c[None, :] < cs[:, None] + NA_COLS)
    dc = np.clip(c[None, :] - c[:, None], -(NA_COLS - 1), NA_COLS - 1) + NA_COLS - 1
    dr = np.arange(kr)[None, :] - np.arange(kr)[:, None] + NA_ROWS - 1
    pick_r = (dr[..., None] == np.arange(2 * NA_ROWS - 1)).astype(np.float32)
    pick_c = (dc[..., None] == np.arange(2 * NA_COLS - 1)).astype(np.float32)
    bias = jnp.einsum('hab,vja,qkb->vhqjk', rpb.astype(F32), pick_r, pick_c, precision=lax.Precision.HIGHEST)
    bias = jnp.where(col_valid[None, None, :, None, :], bias, NEG_INF)
    return bias.reshape(kr, N_HEADS * GRID_W, kr * GRID_W)


def _nbr(dqkv, bias, batch, seq, rows_per_step):
    grid_rows = seq // GRID_W
    kr = min(NA_ROWS, grid_rows)
    nblk = grid_rows // rows_per_step
    tq = rows_per_step * GRID_W
    return pl.pallas_call(
        functools.partial(_nbr_kernel, rows_per_step=rows_per_step, grid_rows=grid_rows, kr=kr),
        grid=(batch, nblk),
        in_specs=[pl.BlockSpec((tq, MIX_W), lambda b, i: (b * nblk + i, 0)),
                  pl.BlockSpec((seq, MIX_W), lambda b, i: (b, 1)),
                  pl.BlockSpec((seq, MIX_W), lambda b, i: (b, 2)),
                  _const_spec(bias.shape)],
        out_specs=pl.BlockSpec((tq, MIX_W), lambda b, i: (b * nblk + i, 0)),
        out_shape=jax.ShapeDtypeStruct((batch * seq, MIX_W), BF16),
        compiler_params=_params("parallel", "arbitrary"),
    )(dqkv, dqkv, dqkv, bias)


BAND_UNROLL = 2


def _band_kernel(qkv_ref, o_ref, lse_ref, *, half, kwin, tq):
    length = qkv_ref.shape[0]
    rows = o_ref.shape[0]
    nq = rows // tq
    row0 = pl.program_id(1) * rows

    def unit(u):
        r, qi = u // nq, u % nq
        col = pl.multiple_of(r * (3 * MIX_W), LANES)
        ocol = pl.multiple_of(r * MIX_W, LANES)
        q_local = pl.multiple_of(qi * tq, tq)
        i0 = row0 + q_local
        kstart = pl.multiple_of(jnp.clip(i0 - half, 0, length - kwin), BF16_SUBLANES)
        qs = _stack_heads(qkv_ref[pl.ds(pl.multiple_of(i0, tq), tq), pl.ds(col, MIX_W)])
        s = lax.dot_general(qs, qkv_ref[pl.ds(kstart, kwin), pl.ds(col + MIX_W, MIX_W)],
                            (((1,), (1,)), ((), ())), preferred_element_type=F32)
        yield
        qpos = i0 + lax.broadcasted_iota(jnp.int32, (N_HEADS * tq, kwin), 0) % tq
        kpos = kstart + lax.broadcasted_iota(jnp.int32, (N_HEADS * tq, kwin), 1)
        s = jnp.where(jnp.abs(qpos - kpos) <= half, s, NEG_INF)
        m = jnp.max(s, axis=-1, keepdims=True)
        p = jnp.exp(s - m)
        l = jnp.sum(p, axis=-1, keepdims=True)
        o = jnp.dot(p.astype(BF16), qkv_ref[pl.ds(kstart, kwin), pl.ds(col + 2 * MIX_W, MIX_W)],
                    preferred_element_type=F32)
        yield
        o_ref[pl.ds(q_local, tq), pl.ds(ocol, MIX_W)] = _unstack_heads(o / l, tq).astype(o_ref.dtype)
        lse = m + jnp.log(l)
        lane = lax.broadcasted_iota(jnp.int32, (1, lse_ref.shape[1]), 1)
        packed = lse_ref[pl.ds(q_local, tq), :]
        for h in range(N_HEADS):
            packed = jnp.where(lane == r * N_HEADS + h, lse[h * tq:(h + 1) * tq], packed)
        lse_ref[pl.ds(q_local, tq), :] = packed

    lse_ref[...] = jnp.zeros(lse_ref.shape, F32)
    units = (o_ref.shape[1] // MIX_W) * nq

    def body(i, carry):
        _interleave(*[unit(i * BAND_UNROLL + j) for j in range(BAND_UNROLL)])
        return carry

    lax.fori_loop(0, units // BAND_UNROLL, body, 0)


def _band(a, batch, seq, window, dil, tq, step_rows):
    half = window // (2 * dil)
    length = seq // dil
    tq = min(tq, length)
    kwin = min(tq + 2 * half, length)
    rows = min(step_rows, length) if dil == 1 else length
    view = a.reshape(batch, length, dil * 3 * MIX_W)
    o, lse = pl.pallas_call(
        functools.partial(_band_kernel, half=half, kwin=kwin, tq=tq),
        grid=(batch, length // rows),
        in_specs=[pl.BlockSpec((None, length, dil * 3 * MIX_W), lambda b, i: (b, 0, 0))],
        out_specs=[pl.BlockSpec((None, rows, dil * MIX_W), lambda b, i: (b, i, 0)),
                   pl.BlockSpec((None, rows, dil * N_HEADS), lambda b, i: (b, i, 0))],
        out_shape=[jax.ShapeDtypeStruct((batch, length, dil * MIX_W), BF16),
                   jax.ShapeDtypeStruct((batch, length, dil * N_HEADS), F32)],
        compiler_params=_params("parallel", "arbitrary"),
    )(view)
    return o.reshape(batch * length, dil * MIX_W), lse.reshape(batch * seq, N_HEADS)


CUM_ROWS = 256


def _chunk_tri(n, upper):
    r = lax.broadcasted_iota(jnp.int32, (n, n), 0)
    c = lax.broadcasted_iota(jnp.int32, (n, n), 1)
    same = (r // B_CHUNK) == (c // B_CHUNK)
    keep = (r <= c) if upper else (r >= c)
    return jnp.where(same & keep, 1.0, 0.0).astype(F32)


def _delta_prep(xe_ref, ab, conv_ref, alog_ref, dtb_ref, qkv_ref, gate_ref):
    pad = HALO
    rows = qkv_ref.shape[0]
    centre = B_CONV // 2
    y = jnp.zeros((rows, 3 * MIX_W), F32)
    for tap in range(B_CONV):
        y = y + xe_ref[pad + tap - centre:pad + tap - centre + rows, :] * conv_ref[tap:tap + 1, :]
    y = y * jax.nn.sigmoid(y)
    q, k = y[:, 0:MIX_W], y[:, MIX_W:2 * MIX_W]
    q = q * lax.rsqrt(_head_sums(q * q) + 1e-6) * (HEAD_DIM ** -0.5)
    k = k * lax.rsqrt(_head_sums(k * k) + 1e-6)
    qkv_ref[:, 0:MIX_W] = q.astype(BF16)
    qkv_ref[:, MIX_W:2 * MIX_W] = k.astype(BF16)
    qkv_ref[:, 2 * MIX_W:] = y[:, 2 * MIX_W:].astype(BF16)

    col = lax.broadcasted_iota(jnp.int32, (1, ab.shape[-1]), 1)
    xs = ab + dtb_ref[...]
    softplus = jnp.maximum(xs, 0.0) + jnp.log(1.0 + jnp.exp(-jnp.abs(xs)))
    g = -jnp.exp(alog_ref[...]) * softplus
    beta = jax.nn.sigmoid(ab)
    tri_f, tri_b = _chunk_tri(CUM_ROWS, False).astype(BF16), _chunk_tri(CUM_ROWS, True).astype(BF16)
    g_hi = g.astype(BF16)
    g_mid = (g - g_hi.astype(F32)).astype(BF16)
    g_lo = (g - g_hi.astype(F32) - g_mid.astype(F32)).astype(BF16)
    for r0 in range(0, rows, CUM_ROWS):
        pieces = [p[r0:r0 + CUM_ROWS] for p in (g_hi, g_mid, g_lo)]
        cum_f = sum(jnp.dot(tri_f, p, preferred_element_type=F32) for p in pieces)
        cum_b = sum(jnp.dot(tri_b, p, preferred_element_type=F32) for p in pieces)
        gate_ref[r0:r0 + CUM_ROWS, :] = jnp.where(col < 8, beta[r0:r0 + CUM_ROWS],
                                                   jnp.where(col < 12, cum_f, cum_b))


def _expand_heads(cols):
    ids = _head_ids(MIX_W)
    out = jnp.broadcast_to(cols[:, 0:1], (cols.shape[0], MIX_W))
    for h in range(1, N_HEADS):
        out = jnp.where(ids == h, jnp.broadcast_to(cols[:, h:h + 1], out.shape), out)
    return out


def _block_diag(packed, mask):
    return jnp.where(mask, jnp.concatenate([packed] * N_HEADS, axis=0), 0.0).astype(BF16)


def _mm(a, b):
    return jnp.dot(a.astype(BF16), b.astype(BF16), preferred_element_type=F32)


def _delta_local(s, backward, qkv_ref, gate_ref, grow_ref, u_ref, w_ref, qh_ref, kt_ref, in_ref, eg_ref, c):
    n = B_CHUNK
    d = 2 * s + (1 if backward else 0)
    r0 = pl.multiple_of(c * n, n)
    q = qkv_ref[s, pl.ds(r0, n), 0:MIX_W]
    k = qkv_ref[s, pl.ds(r0, n), MIX_W:2 * MIX_W]
    v = qkv_ref[s, pl.ds(r0, n), 2 * MIX_W:].astype(F32)
    gates = gate_ref[s, pl.ds(r0, n), :]
    d0 = 4 if backward else 0
    beta = _expand_heads(gates[:, d0:d0 + 4])
    gc = _expand_heads(gates[:, 8 + d0:12 + d0])
    gc_row = grow_ref[s, pl.ds(c, 1), d0 * n:(d0 + 4) * n]

    row = lax.broadcasted_iota(jnp.int32, (n, MIX_W), 0)
    col = lax.broadcasted_iota(jnp.int32, (n, MIX_W), 1) % n
    incl = (row <= col) if backward else (row >= col)
    strict = (row < col) if backward else (row > col)
    bd_mask = (lax.broadcasted_iota(jnp.int32, (MIX_W, MIX_W), 0) // n
               == lax.broadcasted_iota(jnp.int32, (MIX_W, MIX_W), 1) // n)
    decay = jnp.exp(jnp.where(incl, gc - gc_row, NEG_INF))

    kf = k.astype(F32)
    kq = lax.dot_general(jnp.concatenate([q, k], axis=0), _stack_heads(k), (((1,), (1,)), ((), ())),
                         preferred_element_type=F32)
    yield
    intra = kq[0:n] * decay
    low = jnp.where(strict, kq[n:] * beta * decay, 0.0)

    x = jnp.where(row == col, 1.0, 0.0) - low
    p = _mm(low, _block_diag(low, bd_mask))
    yield
    steps = int(math.log2(n)) - 1
    for step in range(steps):
        last = step == steps - 1
        lhs = x if last else jnp.concatenate([x, p], axis=0)
        r = _mm(lhs, _block_diag(p, bd_mask))
        yield
        x = x + r[0:n]
        if not last:
            p = r[n:]

    kbeta = kf * beta
    u_ref[d, pl.ds(r0, n), :] = _mm(x, _block_diag(v * beta, bd_mask))
    w_ref[d, pl.ds(r0, n), :] = _mm(x, _block_diag(kbeta * jnp.exp(gc), bd_mask)).astype(BF16)
    edge = gc[0:1] if backward else gc[n - 1:n]
    qh_ref[d, pl.ds(r0, n), :] = (q.astype(F32) * jnp.exp(gc)).astype(BF16)
    kt_ref[d, pl.ds(r0, n), :] = (kf * jnp.exp(edge - gc)).astype(BF16)
    in_ref[d, pl.ds(r0, n), :] = intra.astype(BF16)
    eg_ref[d, pl.ds(c, 1), :] = jnp.exp(edge)


def _delta_step(s, backward, u_ref, w_ref, qh_ref, kt_ref, in_ref, eg_ref, o_ref, s_ref, c):
    n = B_CHUNK
    d = 2 * s + (1 if backward else 0)
    r0 = pl.multiple_of(c * n, n)
    bd_mask = (lax.broadcasted_iota(jnp.int32, (MIX_W, MIX_W), 0) // n
               == lax.broadcasted_iota(jnp.int32, (MIX_W, MIX_W), 1) // n)
    state = s_ref[d]
    lhs = jnp.concatenate([w_ref[d, pl.ds(r0, n), :], qh_ref[d, pl.ds(r0, n), :]], axis=0)
    ws_qs = jnp.dot(lhs, state.astype(BF16), preferred_element_type=F32)
    yield
    v_new = u_ref[d, pl.ds(r0, n), :] - ws_qs[0:n]
    o_ref[s, pl.ds(r0, n), :] = ws_qs[n:] + jnp.dot(in_ref[d, pl.ds(r0, n), :], _block_diag(v_new, bd_mask),
                                                  preferred_element_type=F32)
    kv = lax.dot_general(kt_ref[d, pl.ds(r0, n), :], v_new.astype(BF16), (((0,), (0,)), ((), ())),
                         preferred_element_type=F32)
    s_ref[d] = state * eg_ref[d, pl.ds(c, 1), :] + jnp.where(bd_mask, kv, 0.0)


LOCAL_UNROLL = 8


def _delta_kernel(qkv_f, gate_f, grow_f, qkv_b, gate_b, grow_b, of_ref, ob_ref,
                  s_ref, u_ref, w_ref, qh_ref, kt_ref, in_ref, eg_ref):
    @pl.when(pl.program_id(1) == 0)
    def _():
        s_ref[...] = jnp.zeros(s_ref.shape, F32)

    nseq = qkv_f.shape[0]
    nchunk = qkv_f.shape[1] // B_CHUNK
    staged = (u_ref, w_ref, qh_ref, kt_ref, in_ref, eg_ref)

    for s in range(nseq):
        def local(i, carry, s=s):
            chunks = [i * LOCAL_UNROLL + g for g in range(LOCAL_UNROLL)]
            _interleave(*[_delta_local(s, False, qkv_f, gate_f, grow_f, *staged, c) for c in chunks],
                        *[_delta_local(s, True, qkv_b, gate_b, grow_b, *staged, c) for c in chunks])
            return carry

        lax.fori_loop(0, nchunk // LOCAL_UNROLL, local, 0)

    def step(i, carry):
        _interleave(*[_delta_step(s, False, *staged, of_ref, s_ref, i) for s in range(nseq)],
                    *[_delta_step(s, True, *staged, ob_ref, s_ref, nchunk - 1 - i) for s in range(nseq)])
        return carry

    lax.fori_loop(0, nchunk, step, 0)


def _delta(qkvn, gates, batch, seq, rows, nseq):
    t = batch * seq
    nblk = seq // rows
    nchunk = rows // B_CHUNK
    grow = gates[:, 8:16].reshape(t // B_CHUNK, B_CHUNK, 8).transpose(0, 2, 1).reshape(t // B_CHUNK, 8 * B_CHUNK)
    group = lambda a, n: a.reshape(batch // nseq, nseq, n, a.shape[-1])
    qkvn4, gates4, grow4 = group(qkvn, seq), group(gates, seq), group(grow, seq // B_CHUNK)
    fwd = lambda b, j: (b, 0, j, 0)
    bwd = lambda b, j: (b, 0, nblk - 1 - j, 0)
    blk = lambda n, w, m: pl.BlockSpec((None, nseq, n, w), m)
    specs = lambda m: [blk(rows, 3 * MIX_W, m), blk(rows, 16, m), blk(nchunk, 8 * B_CHUNK, m)]
    slots = 2 * nseq
    of, ob = pl.pallas_call(
        _delta_kernel,
        grid=(batch // nseq, nblk),
        in_specs=specs(fwd) + specs(bwd),
        out_specs=[blk(rows, MIX_W, fwd), blk(rows, MIX_W, bwd)],
        out_shape=[jax.ShapeDtypeStruct((batch // nseq, nseq, seq, MIX_W), F32)] * 2,
        scratch_shapes=([pltpu.VMEM((slots, MIX_W, MIX_W), F32), pltpu.VMEM((slots, rows, MIX_W), F32)]
                        + [pltpu.VMEM((slots, rows, MIX_W), BF16)] * 4 + [pltpu.VMEM((slots, nchunk, MIX_W), F32)]),
        compiler_params=_params("parallel", "arbitrary"),
    )(qkvn4, gates4, grow4, qkvn4, gates4, grow4)
    return of.reshape(t, MIX_W), ob.reshape(t, MIX_W)


def _merge_kernel(x_ref, gain_ref, oa0, oa1, oa2, la0, la1, la2, bf_ref, bb_ref, z_ref, bgain_ref,
                  oc_ref, od_ref, wg_ref, bg_ref, wbr_ref, wout_ref, out_ref, o_scr):
    x = x_ref[...]
    tm = x.shape[0]
    h = _rms_rows(x, gain_ref[...]).astype(BF16)

    slabs = MIX_W // LANES
    outs_a = [oa0[...].astype(F32)]
    for p, ((_, dil), oa) in enumerate(zip(A_PATTERNS[1:], (oa1, oa2))):
        for r in range(dil):
            for c in range(slabs):
                col = r * MIX_W + c * LANES
                o_scr[p, c, pl.ds(r, tm // dil, stride=dil), :] = oa[:, col:col + LANES].astype(F32)
        outs_a.append(jnp.concatenate([o_scr[p, c] for c in range(slabs)], axis=1))

    lses = [_expand_heads(r[...]) for r in (la0, la1, la2)]
    top = jnp.maximum(jnp.maximum(lses[0], lses[1]), lses[2])
    ws = [jnp.exp(l - top) for l in lses]
    o_a = sum(w * o for w, o in zip(ws, outs_a)) / (ws[0] + ws[1] + ws[2])

    o_b = bf_ref[...] + bb_ref[...]
    z = z_ref[...].astype(F32)
    o_b = (o_b * lax.rsqrt(_head_sums(o_b * o_b) * (1.0 / HEAD_DIM) + RMS_EPS) * bgain_ref[...]
           * (z * jax.nn.sigmoid(z)))

    branches = (o_a, o_b, oc_ref[...], od_ref[...])
    y = jnp.zeros(x.shape, F32)
    for i, o in enumerate(branches):
        gate = jax.nn.sigmoid(jnp.dot(h, wg_ref[i], preferred_element_type=F32) + bg_ref[i])
        y = y + gate * jnp.dot(o.astype(BF16), wbr_ref[i], preferred_element_type=F32)
    out_ref[...] = x + jnp.dot(y.astype(BF16), wout_ref[...], preferred_element_type=F32)


def _merge(x2d, a_outs, b_outs, z, oc, od, lw, tm):
    t = x2d.shape[0]
    tok = lambda w: pl.BlockSpec((tm, w), lambda i: (i, 0))
    (o0, l0), (o1, l1), (o2, l2) = a_outs
    acts = (o0, o1, o2, l0, l1, l2, b_outs[0], b_outs[1], z)
    weights = (lw["wg"], lw["bg"], lw["wbr"], lw["wout"])
    return pl.pallas_call(
        _merge_kernel,
        grid=(t // tm,),
        in_specs=([tok(D_MODEL), _const_spec(lw["norm_mix"].shape)]
                  + [pl.BlockSpec((tm * a.shape[0] // t, a.shape[1]), lambda i: (i, 0)) for a in acts]
                  + [_const_spec(lw["b_gain"].shape), tok(MIX_W), tok(MIX_W)]
                  + [_const_spec(w.shape) for w in weights]),
        out_specs=tok(D_MODEL),
        out_shape=jax.ShapeDtypeStruct((t, D_MODEL), F32),
        scratch_shapes=[pltpu.VMEM((len(A_PATTERNS) - 1, MIX_W // LANES, tm, LANES), F32)],
        compiler_params=_params("parallel"),
    )(x2d, lw["norm_mix"], *acts, lw["b_gain"], oc, od, *weights)


EXPERT_UNROLL = 2


def _moe_kernel(x_ref, gain_ref, wr_ref, br_ref, wgt_ref, wup_ref, wdn_ref, fgain_ref, out_ref,
                hs_ref, ws_ref, ys_ref, *, final, cap):
    x = x_ref[...]
    tm = x.shape[0]
    h = _rms_rows(x, gain_ref[...])
    hb = h.astype(BF16)
    h_lo = (h - hb.astype(F32)).astype(BF16)
    parts = jnp.dot(jnp.concatenate([hb, h_lo], axis=0), wr_ref[...], preferred_element_type=F32)
    logits = (parts[:tm, :LANES] + parts[:tm, LANES:]) + (parts[tm:, :LANES] + parts[tm:, LANES:]) + br_ref[...]
    lane = lax.broadcasted_iota(jnp.int32, (1, LANES), 1)
    none = LANES

    def first_max(vals):
        top = jnp.max(vals, axis=-1, keepdims=True)
        return top, jnp.min(jnp.where(vals == top, lane, none), axis=-1, keepdims=True)

    gl = jnp.where(lane < N_GROUPS, logits, NEG_INF)
    g_top, g_idx = first_max(gl)
    g_prob = 1.0 / jnp.sum(jnp.exp(gl - g_top), axis=-1, keepdims=True)
    in_group = (lane >= N_GROUPS) & (lane < N_GROUPS + N_EXPERTS) & ((lane - N_GROUPS) // EXPERTS_PER_GROUP == g_idx)
    el = jnp.where(in_group, logits, NEG_INF)
    top1, idx1 = first_max(el)
    top2, idx2 = first_max(jnp.where(lane == idx1, NEG_INF, el))
    e2 = jnp.exp(top2 - top1)
    w1 = g_prob / (1.0 + e2)
    weights = jnp.where(lane == idx1, w1, 0.0) + jnp.where(lane == idx2, w1 * e2, 0.0)

    def expert(rows, row_weights, e, outs):
        w_e = jnp.sum(jnp.where(lane == e + N_GROUPS, row_weights, 0.0), axis=-1, keepdims=True)
        gate = jnp.dot(rows, wgt_ref[e], preferred_element_type=F32)
        up = jnp.dot(rows, wup_ref[e], preferred_element_type=F32)
        yield
        hid = gate * jax.nn.sigmoid(gate) * up * w_e
        outs.append(jnp.dot(hid.astype(BF16), wdn_ref[e], preferred_element_type=F32))

    def finish(moe_out):
        y = x + moe_out
        if final:
            y = _rms_rows(y, fgain_ref[...])
        out_ref[...] = y

    member = jnp.where(lane == g_idx, 1.0, 0.0)
    fits = jnp.max(jnp.sum(member, axis=0, keepdims=True)) <= cap

    @pl.when(fits)
    def _():
        before = lax.broadcasted_iota(jnp.int32, (tm, tm), 0) > lax.broadcasted_iota(jnp.int32, (tm, tm), 1)
        ahead = jnp.dot(jnp.where(before, 1.0, 0.0).astype(BF16), member.astype(BF16),
                        preferred_element_type=F32)
        pos = jnp.sum(member * ahead, axis=-1, keepdims=True).astype(jnp.int32)
        slot = g_idx * cap + pos
        perm_t = jnp.where(lax.broadcasted_iota(jnp.int32, (1, N_GROUPS * cap), 1) == slot, 1.0, 0.0).astype(BF16)
        slot_row = jnp.broadcast_to(slot.astype(F32), (tm, LANES)).T[0:1]
        perm = jnp.where(lax.broadcasted_iota(jnp.int32, (N_GROUPS * cap, 1), 0).astype(F32) == slot_row,
                         1.0, 0.0).astype(BF16)
        gather = lambda a: jnp.dot(perm, a, preferred_element_type=F32)
        hs_ref[...] = gather(hb).astype(BF16)
        w_hi = weights.astype(BF16)
        w_parts = gather(jnp.concatenate([w_hi, (weights - w_hi.astype(F32)).astype(BF16)], axis=1))
        ws_ref[...] = w_parts[:, :LANES] + w_parts[:, LANES:]

        def group(g, carry):
            r0 = pl.multiple_of(g * cap, cap)
            rows, row_weights = hs_ref[pl.ds(r0, cap), :], ws_ref[pl.ds(r0, cap), :]
            outs = []
            _interleave(*[expert(rows, row_weights, g * EXPERTS_PER_GROUP + j, outs)
                          for j in range(EXPERTS_PER_GROUP)])
            ys_ref[pl.ds(r0, cap), :] = sum(outs[1:], outs[0]).astype(BF16)
            return carry

        lax.fori_loop(0, N_GROUPS, group, 0)
        finish(jnp.dot(perm_t, ys_ref[...], preferred_element_type=F32))

    @pl.when(jnp.logical_not(fits))
    def _():
        def experts(i, acc):
            outs = []
            _interleave(*[expert(hb, weights, i * EXPERT_UNROLL + j, outs) for j in range(EXPERT_UNROLL)])
            return acc + sum(outs[1:], outs[0])

        finish(lax.fori_loop(0, N_EXPERTS // EXPERT_UNROLL, experts, jnp.zeros(x.shape, F32)))


def _moe(x2d, lw, final_gain, final, tm, cap):
    t = x2d.shape[0]
    tok = pl.BlockSpec((tm, D_MODEL), lambda i: (i, 0))
    once = lambda a: pl.BlockSpec(a.shape, lambda i: (0,) * a.ndim, pipeline_mode=pl.Buffered(1))
    weights = (lw["norm_ffn"], lw["w_route"], lw["b_route"], lw["w_ff_gate"], lw["w_ff_up"], lw["w_ff_down"],
               final_gain)
    return pl.pallas_call(
        functools.partial(_moe_kernel, final=final, cap=cap),
        grid=(t // tm,),
        in_specs=[tok] + [once(w) for w in weights],
        out_specs=tok,
        out_shape=jax.ShapeDtypeStruct((t, D_MODEL), F32),
        scratch_shapes=[pltpu.VMEM((N_GROUPS * cap, D_MODEL), BF16), pltpu.VMEM((N_GROUPS * cap, LANES), F32),
                        pltpu.VMEM((N_GROUPS * cap, D_MODEL), BF16)],
        compiler_params=_params("parallel"),
    )(x2d, *weights)


def _split_bf16(w):
    hi = w.astype(BF16)
    return jnp.concatenate([hi, (w - hi.astype(F32)).astype(BF16)], axis=1)


def _layer_weights(p, l):
    w_in = p["w_in"][l]
    offs = np.cumsum([0, 3 * MIX_W, 3 * MIX_W, MIX_W, 16, MIX_W, 2 * C_KV_HEADS * HEAD_DIM, 3 * MIX_W])
    sec = [w_in[:, offs[i]:offs[i + 1]] for i in range(7)]
    c_order = np.concatenate([np.arange(h * HEAD_DIM, (h + 1) * HEAD_DIM) for h in (0, 2, 1, 3)])
    row = lambda v: v.reshape(1, -1).astype(F32)
    gate16 = lambda v: jnp.concatenate([jnp.zeros((8,), F32), v.reshape(-1).astype(F32)]).reshape(1, 16)
    wbr = p["w_branch"][l]
    wbr = jnp.stack([wbr[0], wbr[1], wbr[2][c_order], wbr[3]])
    w_route = jnp.concatenate([p["w_route_group"][l], p["w_route_expert"][l].reshape(D_MODEL, N_EXPERTS)], axis=1)
    b_route = jnp.concatenate([p["b_route_group"][l], p["b_route_expert"][l].reshape(N_EXPERTS)])
    pad_lanes = lambda a: jnp.pad(a, [(0, 0)] * (a.ndim - 1) + [(0, LANES - a.shape[-1])])
    return dict(
        norm_mix=row(p["norm_mix"][l]),
        wa=sec[0].astype(BF16), wb=sec[1].astype(BF16), wz=sec[2].astype(BF16),
        wab=pad_lanes(sec[3]).astype(BF16), wcq=sec[4][:, c_order].astype(BF16), wck=sec[5][:, :C_KV_HEADS * HEAD_DIM].astype(BF16), wcvt=sec[5][:, C_KV_HEADS * HEAD_DIM:].T.astype(BF16),
        wd=sec[6].astype(BF16),
        cq_gain=row(jnp.tile(p["c_q_norm"][l], N_HEADS)), ck_gain=row(jnp.tile(p["c_k_norm"][l], C_KV_HEADS)),
        conv_w=p["conv_w"][l].astype(F32), a_log16=gate16(p["a_log"][l]), dt_bias16=gate16(p["dt_bias"][l]),
        b_gain=row(jnp.tile(p["b_norm"][l], N_HEADS)),
        wg=p["w_gate"][l].astype(BF16), bg=p["b_gate"][l].reshape(N_HEADS, 1, D_MODEL).astype(F32),
        wbr=wbr.astype(BF16), wout=p["w_out"][l].astype(BF16),
        norm_ffn=row(p["norm_ffn"][l]),
        w_route=_split_bf16(pad_lanes(w_route).astype(F32)), b_route=pad_lanes(b_route.reshape(1, -1)).astype(F32),
        w_ff_gate=p["w_ff_gate"][l].reshape(N_EXPERTS, D_MODEL, D_EXPERT).astype(BF16),
        w_ff_up=p["w_ff_up"][l].reshape(N_EXPERTS, D_MODEL, D_EXPERT).astype(BF16),
        w_ff_down=p["w_ff_down"][l].reshape(N_EXPERTS, D_EXPERT, D_MODEL).astype(BF16),
        d_rpb=p["d_rpb"][l],
    )


def _tiles(seq):
    tm = min(512, seq)
    share = tm // N_GROUPS
    moe_cap = (share + share // 4 + BF16_SUBLANES - 1) // BF16_SUBLANES * BF16_SUBLANES
    return dict(tm=tm, moe_cap=moe_cap, tq_c=min(256, seq), tk_c=min(512, seq), rows_d=min(8, seq // GRID_W),
                tq_a=128, rows_a=2048, rows_delta=min(1024, seq))


def _encoder(x, layers, final_gain):
    batch, seq, _ = x.shape
    ts = _tiles(seq)
    tables = _rotary_tables(seq)
    depth = len(layers)
    x2d = x.reshape(batch * seq, D_MODEL)
    for l, lw in enumerate(layers):
        a_views, (qkvn, z, gates, cq, ck, cvt, dqkv) = _in_proj(x2d, seq, lw, tables, ts["tm"])
        a_outs = [_band(a, batch, seq, window, dil, ts["tq_a"], ts["rows_a"])
                  for a, (window, dil) in zip(a_views, A_PATTERNS)]
        b_outs = _delta(qkvn, gates, batch, seq, ts["rows_delta"], 2 if batch % 2 == 0 else 1)
        oc = _gqa(cq, ck, cvt, batch, seq, ts["tq_c"], ts["tk_c"])
        od = _nbr(dqkv, _nbr_bias(lw["d_rpb"], min(NA_ROWS, seq // GRID_W)), batch, seq, ts["rows_d"])
        x2d = _merge(x2d, a_outs, b_outs, z, oc, od, lw, ts["tm"])
        x2d = _moe(x2d, lw, final_gain, l == depth - 1, ts["tm"], ts["moe_cap"])
    return x2d.reshape(batch, seq, D_MODEL)


def kernel(x_prompt, x_sample, norm_mix, w_in, conv_w, a_log, dt_bias, b_norm, c_q_norm, c_k_norm, d_rpb,
           w_gate, b_gate, w_branch, w_out, norm_ffn, w_route_group, b_route_group, w_route_expert,
           b_route_expert, w_ff_gate, w_ff_up, w_ff_down, norm_final):
    p = dict(norm_mix=norm_mix, w_in=w_in, conv_w=conv_w, a_log=a_log, dt_bias=dt_bias, b_norm=b_norm,
             c_q_norm=c_q_norm, c_k_norm=c_k_norm, d_rpb=d_rpb, w_gate=w_gate, b_gate=b_gate,
             w_branch=w_branch, w_out=w_out, norm_ffn=norm_ffn, w_route_group=w_route_group,
             b_route_group=b_route_group, w_route_expert=w_route_expert, b_route_expert=b_route_expert,
             w_ff_gate=w_ff_gate, w_ff_up=w_ff_up, w_ff_down=w_ff_down)
    layers = [_layer_weights(p, l) for l in range(norm_mix.shape[0])]
    final_gain = norm_final.reshape(1, -1).astype(F32)
    return _encoder(x_prompt, layers, final_gain), _encoder(x_sample, layers, final_gain)
```

```python
import functools
import math

import jax
import jax.numpy as jnp
import numpy as np
from jax import lax
from jax.experimental import pallas as pl
from jax.experimental.pallas import tpu as pltpu

D_MODEL = 1024
HEAD_DIM = 64
N_HEADS = 4
MIX_W = N_HEADS * HEAD_DIM
GRID_W = 64
RMS_EPS = 1e-6
NEG_INF = -1e30
LOG2_E = math.log2(math.e)

A_PATTERNS = ((128, 1), (512, 4), (2048, 16))
ROPE_THETA = 500000.0
ROPE_DIMS = HEAD_DIM // 4
AXIAL_THETA = 10000.0
C_KV_HEADS = 2
B_CONV = 5
B_CHUNK = 64
NA_ROWS = 8
NA_COLS = 16
N_GROUPS = 4
EXPERTS_PER_GROUP = 4
N_EXPERTS = N_GROUPS * EXPERTS_PER_GROUP
D_EXPERT = 256

LANES = 128
BF16_SUBLANES = 16
HALO = 8
VMEM_LIMIT = 56 * 1024 * 1024

F32 = jnp.float32
BF16 = jnp.bfloat16


def _params(*sem):
    return pltpu.CompilerParams(dimension_semantics=sem, vmem_limit_bytes=VMEM_LIMIT)


def _const_spec(shape):
    n = len(shape)
    return pl.BlockSpec(shape, lambda *_: (0,) * n)


def _rms_rows(x, gain):
    return x * lax.rsqrt(jnp.mean(x * x, axis=-1, keepdims=True) + RMS_EPS) * gain


def _head_ids(width):
    return lax.broadcasted_iota(jnp.int32, (1, width), 1) // HEAD_DIM


def _block_ones(width):
    r = lax.broadcasted_iota(jnp.int32, (width, width), 0) // HEAD_DIM
    c = lax.broadcasted_iota(jnp.int32, (width, width), 1) // HEAD_DIM
    return jnp.where(r == c, 1.0, 0.0).astype(BF16)


def _head_sums(x):
    return jnp.dot(x.astype(BF16), _block_ones(x.shape[-1]), preferred_element_type=F32)


def _interleave(*chains):
    live = list(chains)
    while live:
        live = [c for c in live if next(c, live) is not live]


def _rotate(x, cos, sin_lo, sin_hi, shift):
    w = x.shape[-1]
    return x * cos + pltpu.roll(x, shift, 1) * sin_hi + pltpu.roll(x, w - shift, 1) * sin_lo


def _in_proj_kernel(x_ref, xprev_ref, xnext_ref, gain_ref, wa_ref, wb_ref, wz_ref, wab_ref, wcq_ref, wck_ref,
                    wcvt_ref, wd_ref, cqg_ref, ckg_ref, conv_ref, alog_ref, dtb_ref, rot_a_ref, rot_c_ref,
                    a1_ref, a4_ref, a16_ref, b_ref, z_ref, gate_ref, cq_ref, ck_ref, cvt_ref, d_ref,
                    a_scr, xe_ref, *, nblk):
    h = _rms_rows(x_ref[...], gain_ref[...]).astype(BF16)
    dot = lambda w_ref: jnp.dot(h, w_ref[...], preferred_element_type=F32)
    scale = HEAD_DIM ** -0.5
    tm = x_ref.shape[0]

    halo = jnp.concatenate([xprev_ref[...], xnext_ref[...]], axis=0)
    ub_halo = jnp.dot(_rms_rows(halo, gain_ref[...]).astype(BF16), wb_ref[...], preferred_element_type=F32)
    pos = pl.program_id(0) % nblk
    xe_ref[0:HALO, :] = ub_halo[0:HALO] * jnp.where(pos > 0, 1.0, 0.0)
    xe_ref[HALO:HALO + tm, :] = dot(wb_ref)
    xe_ref[HALO + tm:, :] = ub_halo[HALO:] * jnp.where(pos < nblk - 1, 1.0, 0.0)
    _delta_prep(xe_ref, dot(wab_ref)[:, :gate_ref.shape[-1]], conv_ref, alog_ref, dtb_ref, b_ref, gate_ref)

    ua = dot(wa_ref)
    cos, s_lo, s_hi = rot_a_ref[0], rot_a_ref[1], rot_a_ref[2]
    half = ROPE_DIMS // 2
    qkv_a = (_rotate(ua[:, 0:MIX_W], cos, s_lo, s_hi, half) * scale,
             _rotate(ua[:, MIX_W:2 * MIX_W], cos, s_lo, s_hi, half), ua[:, 2 * MIX_W:])
    a1_ref[...] = jnp.concatenate(qkv_a, axis=1).astype(BF16)
    slabs = 3 * MIX_W // LANES
    for c in range(slabs):
        part, off = divmod(c * LANES, MIX_W)
        a_scr[c] = qkv_a[part][:, off:off + LANES]
    for (_, dil), a_ref in zip(A_PATTERNS[1:], (a4_ref, a16_ref)):
        for r in range(dil):
            for c in range(slabs):
                col = r * 3 * MIX_W + c * LANES
                a_ref[:, col:col + LANES] = a_scr[c, pl.ds(r, tm // dil, stride=dil), :].astype(BF16)

    z_ref[...] = dot(wz_ref).astype(BF16)

    cos, s_lo, s_hi = rot_c_ref[0], rot_c_ref[1], rot_c_ref[2]
    quarter = HEAD_DIM // 4
    uq = dot(wcq_ref)
    uq = uq * lax.rsqrt(_head_sums(uq * uq) * (1.0 / HEAD_DIM) + RMS_EPS) * cqg_ref[...]
    cq_ref[...] = (_rotate(uq, cos, s_lo, s_hi, quarter) * (scale * LOG2_E)).astype(BF16)
    kw = C_KV_HEADS * HEAD_DIM
    uk = dot(wck_ref)
    uk = uk * lax.rsqrt(_head_sums(uk * uk) * (1.0 / HEAD_DIM) + RMS_EPS) * ckg_ref[...]
    ck_ref[...] = _rotate(uk, cos[:, :kw], s_lo[:, :kw], s_hi[:, :kw], quarter).astype(BF16)
    cvt_ref[...] = lax.dot_general(wcvt_ref[...], h, (((1,), (1,)), ((), ())),
                                   preferred_element_type=F32).astype(BF16)

    ud = dot(wd_ref)
    d_ref[:, 0:MIX_W] = (ud[:, 0:MIX_W] * scale).astype(BF16)
    d_ref[:, MIX_W:] = ud[:, MIX_W:].astype(BF16)


def _rotary_tables(seq):
    pos = np.arange(seq, dtype=np.float64)

    def table(pos_of_lane, theta, half, lane_in_group, active):
        freq_idx = lane_in_group % half
        inv = theta ** (-(freq_idx.astype(np.float64)) / half)
        ang = pos_of_lane * inv[None, :]
        cos = np.where(active[None, :], np.cos(ang), 1.0)
        sin = np.where(active[None, :], np.sin(ang), 0.0)
        low = (lane_in_group < half)[None, :]
        return np.stack([cos, np.where(low, -sin, 0.0), np.where(low, 0.0, sin)]).astype(np.float32)

    lane = np.arange(MIX_W) % HEAD_DIM
    rot_a = table(np.broadcast_to(pos[:, None], (seq, MIX_W)), ROPE_THETA, ROPE_DIMS // 2, lane % ROPE_DIMS,
                  lane < ROPE_DIMS)
    rows, cols = pos // GRID_W, pos % GRID_W
    half = HEAD_DIM // 2
    pos_c = np.where((lane < half)[None, :], rows[:, None], cols[:, None])
    rot_c = table(pos_c, AXIAL_THETA, half // 2, lane % half, np.ones(MIX_W, bool))
    return jnp.asarray(rot_a), jnp.asarray(rot_c)


def _in_proj(x2d, seq, lw, tables, tm):
    t = x2d.shape[0]
    rot_a, rot_c = tables
    nblk = seq // tm
    tok = lambda w: pl.BlockSpec((tm, w), lambda i: (i, 0))
    rot = pl.BlockSpec((3, tm, MIX_W), lambda i: (0, i % nblk, 0))
    weights = (lw["norm_mix"], lw["wa"], lw["wb"], lw["wz"], lw["wab"], lw["wcq"], lw["wck"], lw["wcvt"], lw["wd"],
               lw["cq_gain"], lw["ck_gain"], lw["conv_w"], lw["a_log16"], lw["dt_bias16"])
    kw = C_KV_HEADS * HEAD_DIM
    per = tm // HALO
    halo_specs = [pl.BlockSpec((HALO, D_MODEL), lambda i: (jnp.maximum(i * per - 1, 0), 0)),
                  pl.BlockSpec((HALO, D_MODEL), lambda i: (jnp.minimum((i + 1) * per, t // HALO - 1), 0))]
    outs = ((3 * MIX_W, BF16), (MIX_W, BF16), (16, F32), (MIX_W, BF16), (kw, BF16), None, (3 * MIX_W, BF16))
    out_specs = [pl.BlockSpec((kw, tm), lambda i: (0, i)) if o is None else tok(o[0]) for o in outs]
    out_shape = [jax.ShapeDtypeStruct((kw, t), BF16) if o is None else jax.ShapeDtypeStruct((t, o[0]), o[1])
                 for o in outs]
    a_specs = [pl.BlockSpec((tm // dil, dil * 3 * MIX_W), lambda i: (i, 0)) for _, dil in A_PATTERNS]
    a_shape = [jax.ShapeDtypeStruct((t // dil, dil * 3 * MIX_W), BF16) for _, dil in A_PATTERNS]
    res = pl.pallas_call(
        functools.partial(_in_proj_kernel, nblk=nblk),
        grid=(t // tm,),
        in_specs=[tok(D_MODEL)] + halo_specs + [_const_spec(w.shape) for w in weights] + [rot, rot],
        out_specs=a_specs + out_specs,
        out_shape=a_shape + out_shape,
        scratch_shapes=[pltpu.VMEM((3 * MIX_W // LANES, tm, LANES), F32),
                        pltpu.VMEM((tm + 2 * HALO, 3 * MIX_W), F32)],
        compiler_params=_params("parallel"),
    )(x2d, x2d, x2d, *weights, rot_a, rot_c)
    return res[:len(A_PATTERNS)], res[len(A_PATTERNS):]


def _gqa_kernel(q_ref, k_ref, vt_ref, o_ref, acc_ref, s_ref, *, tk):
    tq = q_ref.shape[0]
    seq = k_ref.shape[0]
    kw = C_KV_HEADS * HEAD_DIM
    qt = q_ref[...].astype(F32).T
    row_g = lax.broadcasted_iota(jnp.int32, (kw, 1), 0) // HEAD_DIM
    qs = jnp.concatenate([jnp.where(row_g == g, qt[blk * kw:(blk + 1) * kw], 0.0)
                          for g in range(C_KV_HEADS) for blk in range(2)], axis=1).astype(BF16)
    acc_ref[...] = jnp.zeros(acc_ref.shape, F32)

    width = 2 * tq

    nchunk = seq // tk

    def scores(j, slot):
        start = pl.multiple_of(j * tk, tk)
        s_ref[slot] = jnp.dot(k_ref[pl.ds(start, tk), :], qs, preferred_element_type=F32)

    def kv_head(g, j, slot, m_old, l_old, out):
        cols = slice(g * width, (g + 1) * width)
        start = pl.multiple_of(j * tk, tk)
        s = s_ref[slot, :, cols]
        m_new = jnp.maximum(m_old, jnp.max(s, axis=0, keepdims=True))
        alpha = jnp.exp2(m_old - m_new)
        p = jnp.exp2(s - m_new)
        pv = jnp.dot(vt_ref[g * HEAD_DIM:(g + 1) * HEAD_DIM, pl.ds(start, tk)], p.astype(BF16),
                     preferred_element_type=F32)
        yield
        acc_ref[:, cols] = acc_ref[:, cols] * alpha + pv
        out[g] = (m_new, l_old * alpha + jnp.sum(p, axis=0, keepdims=True))

    def softmax_chunk(j, slot, carry):
        out = [None] * C_KV_HEADS
        _interleave(*[kv_head(g, j, slot, *carry[g], out) for g in range(C_KV_HEADS)])
        return tuple(out)

    def chunk_pair(i, carry, last):
        scores(2 * i + 1, 1)
        carry = softmax_chunk(2 * i, 0, carry)
        if not last:
            scores(2 * i + 2, 0)
        return softmax_chunk(2 * i + 1, 1, carry)

    scores(0, 0)
    carry = ((jnp.full((1, width), NEG_INF, F32), jnp.zeros((1, width), F32)),) * C_KV_HEADS
    if nchunk == 1:
        carry = softmax_chunk(0, 0, carry)
    else:
        assert nchunk % 2 == 0
        carry = lax.fori_loop(0, nchunk // 2 - 1, lambda i, c: chunk_pair(i, c, False), carry)
        carry = chunk_pair(nchunk // 2 - 1, carry, True)
    o = acc_ref[...] / jnp.concatenate([l for _, l in carry], axis=1)
    for blk in range(2):
        pair = jnp.concatenate([o[:, (2 * g + blk) * tq:(2 * g + blk + 1) * tq] for g in range(C_KV_HEADS)], axis=0)
        o_ref[:, blk * kw:(blk + 1) * kw] = pair.T.astype(o_ref.dtype)


def _gqa(cq, ck, cvt, batch, seq, tq, tk):
    nq = seq // tq
    kw = C_KV_HEADS * HEAD_DIM
    return pl.pallas_call(
        functools.partial(_gqa_kernel, tk=tk),
        grid=(batch, nq),
        in_specs=[pl.BlockSpec((tq, MIX_W), lambda b, i: (b * nq + i, 0)),
                  pl.BlockSpec((seq, kw), lambda b, i: (b, 0)),
                  pl.BlockSpec((kw, seq), lambda b, i: (0, b))],
        out_specs=pl.BlockSpec((tq, MIX_W), lambda b, i: (b * nq + i, 0)),
        out_shape=jax.ShapeDtypeStruct((batch * seq, MIX_W), BF16),
        scratch_shapes=[pltpu.VMEM((HEAD_DIM, 4 * tq), F32), pltpu.VMEM((2, tk, 4 * tq), F32)],
        compiler_params=_params("parallel", "arbitrary"),
    )(cq, ck, cvt)


def _stack_heads(q):
    ids = _head_ids(MIX_W)
    zero = jnp.zeros((), q.dtype)
    return jnp.concatenate([jnp.where(ids == h, q, zero) for h in range(N_HEADS)], axis=0)


def _unstack_heads(o, n):
    ids = _head_ids(MIX_W)
    out = o[0:n]
    for h in range(1, N_HEADS):
        out = jnp.where(ids == h, o[h * n:(h + 1) * n], out)
    return out


NBR_UNROLL = 4


def _nbr_kernel(q_ref, k_ref, v_ref, bias_ref, o_ref, *, rows_per_step, grid_rows, kr):
    j = pl.program_id(1)
    win = kr * GRID_W

    def query_row(i):
        r = j * rows_per_step + i
        rs = jnp.clip(r - kr // 2, 0, grid_rows - kr)
        kstart = pl.multiple_of(rs * GRID_W, GRID_W)
        qstart = pl.multiple_of(i * GRID_W, GRID_W)
        qs = _stack_heads(q_ref[pl.ds(qstart, GRID_W), :])
        s = lax.dot_general(qs, k_ref[pl.ds(kstart, win), :], (((1,), (1,)), ((), ())),
                            preferred_element_type=F32)
        yield
        s = s + bias_ref[r - rs]
        m = jnp.max(s, axis=-1, keepdims=True)
        p = jnp.exp(s - m)
        l = jnp.sum(p, axis=-1, keepdims=True)
        o = jnp.dot(p.astype(BF16), v_ref[pl.ds(kstart, win), :], preferred_element_type=F32)
        yield
        o_ref[pl.ds(qstart, GRID_W), :] = _unstack_heads(o / l, GRID_W).astype(o_ref.dtype)

    def body(i, carry):
        _interleave(*[query_row(i * NBR_UNROLL + u) for u in range(NBR_UNROLL)])
        return carry

    lax.fori_loop(0, rows_per_step // NBR_UNROLL, body, 0)


def _nbr_bias(rpb, kr):
    c = np.arange(GRID_W)
    cs = np.clip(c - NA_COLS // 2, 0, GRID_W - NA_COLS)
    col_valid = (c[None, :] >= cs[:, None]) & (c[None, :] < cs[:, None] + NA_COLS)
    dc = np.clip(c[None, :] - c[:, None], -(NA_COLS - 1), NA_COLS - 1) + NA_COLS - 1
    dr = np.arange(kr)[None, :] - np.arange(kr)[:, None] + NA_ROWS - 1
    pick_r = (dr[..., None] == np.arange(2 * NA_ROWS - 1)).astype(np.float32)
    pick_c = (dc[..., None] == np.arange(2 * NA_COLS - 1)).astype(np.float32)
    bias = jnp.einsum('hab,vja,qkb->vhqjk', rpb.astype(F32), pick_r, pick_c, precision=lax.Precision.HIGHEST)
    bias = jnp.where(col_valid[None, None, :, None, :], bias, NEG_INF)
    return bias.reshape(kr, N_HEADS * GRID_W, kr * GRID_W)


def _nbr(dqkv, bias, batch, seq, rows_per_step):
    grid_rows = seq // GRID_W
    kr = min(NA_ROWS, grid_rows)
    nblk = grid_rows // rows_per_step
    tq = rows_per_step * GRID_W
    return pl.pallas_call(
        functools.partial(_nbr_kernel, rows_per_step=rows_per_step, grid_rows=grid_rows, kr=kr),
        grid=(batch, nblk),
        in_specs=[pl.BlockSpec((tq, MIX_W), lambda b, i: (b * nblk + i, 0)),
                  pl.BlockSpec((seq, MIX_W), lambda b, i: (b, 1)),
                  pl.BlockSpec((seq, MIX_W), lambda b, i: (b, 2)),
                  _const_spec(bias.shape)],
        out_specs=pl.BlockSpec((tq, MIX_W), lambda b, i: (b * nblk + i, 0)),
        out_shape=jax.ShapeDtypeStruct((batch * seq, MIX_W), BF16),
        compiler_params=_params("parallel", "arbitrary"),
    )(dqkv, dqkv, dqkv, bias)


BAND_UNROLL = 2


def _band_kernel(qkv_ref, o_ref, lse_ref, *, half, kwin, tq):
    length = qkv_ref.shape[0]
    rows = o_ref.shape[0]
    nq = rows // tq
    row0 = pl.program_id(1) * rows

    def unit(u):
        r, qi = u // nq, u % nq
        col = pl.multiple_of(r * (3 * MIX_W), LANES)
        ocol = pl.multiple_of(r * MIX_W, LANES)
        q_local = pl.multiple_of(qi * tq, tq)
        i0 = row0 + q_local
        kstart = pl.multiple_of(jnp.clip(i0 - half, 0, length - kwin), BF16_SUBLANES)
        qs = _stack_heads(qkv_ref[pl.ds(pl.multiple_of(i0, tq), tq), pl.ds(col, MIX_W)])
        s = lax.dot_general(qs, qkv_ref[pl.ds(kstart, kwin), pl.ds(col + MIX_W, MIX_W)],
                            (((1,), (1,)), ((), ())), preferred_element_type=F32)
        yield
        qpos = i0 + lax.broadcasted_iota(jnp.int32, (N_HEADS * tq, kwin), 0) % tq
        kpos = kstart + lax.broadcasted_iota(jnp.int32, (N_HEADS * tq, kwin), 1)
        s = jnp.where(jnp.abs(qpos - kpos) <= half, s, NEG_INF)
        m = jnp.max(s, axis=-1, keepdims=True)
        p = jnp.exp(s - m)
        l = jnp.sum(p, axis=-1, keepdims=True)
        o = jnp.dot(p.astype(BF16), qkv_ref[pl.ds(kstart, kwin), pl.ds(col + 2 * MIX_W, MIX_W)],
                    preferred_element_type=F32)
        yield
        o_ref[pl.ds(q_local, tq), pl.ds(ocol, MIX_W)] = _unstack_heads(o / l, tq).astype(o_ref.dtype)
        lse = m + jnp.log(l)
        lane = lax.broadcasted_iota(jnp.int32, (1, lse_ref.shape[1]), 1)
        packed = lse_ref[pl.ds(q_local, tq), :]
        for h in range(N_HEADS):
            packed = jnp.where(lane == r * N_HEADS + h, lse[h * tq:(h + 1) * tq], packed)
        lse_ref[pl.ds(q_local, tq), :] = packed

    lse_ref[...] = jnp.zeros(lse_ref.shape, F32)
    units = (o_ref.shape[1] // MIX_W) * nq

    def body(i, carry):
        _interleave(*[unit(i * BAND_UNROLL + j) for j in range(BAND_UNROLL)])
        return carry

    lax.fori_loop(0, units // BAND_UNROLL, body, 0)


def _band(a, batch, seq, window, dil, tq, step_rows):
    half = window // (2 * dil)
    length = seq // dil
    tq = min(tq, length)
    kwin = min(tq + 2 * half, length)
    rows = min(step_rows, length) if dil == 1 else length
    view = a.reshape(batch, length, dil * 3 * MIX_W)
    o, lse = pl.pallas_call(
        functools.partial(_band_kernel, half=half, kwin=kwin, tq=tq),
        grid=(batch, length // rows),
        in_specs=[pl.BlockSpec((None, length, dil * 3 * MIX_W), lambda b, i: (b, 0, 0))],
        out_specs=[pl.BlockSpec((None, rows, dil * MIX_W), lambda b, i: (b, i, 0)),
                   pl.BlockSpec((None, rows, dil * N_HEADS), lambda b, i: (b, i, 0))],
        out_shape=[jax.ShapeDtypeStruct((batch, length, dil * MIX_W), BF16),
                   jax.ShapeDtypeStruct((batch, length, dil * N_HEADS), F32)],
        compiler_params=_params("parallel", "arbitrary"),
    )(view)
    return o.reshape(batch * length, dil * MIX_W), lse.reshape(batch * seq, N_HEADS)


CUM_ROWS = 256


def _chunk_tri(n, upper):
    r = lax.broadcasted_iota(jnp.int32, (n, n), 0)
    c = lax.broadcasted_iota(jnp.int32, (n, n), 1)
    same = (r // B_CHUNK) == (c // B_CHUNK)
    keep = (r <= c) if upper else (r >= c)
    return jnp.where(same & keep, 1.0, 0.0).astype(F32)


def _delta_prep(xe_ref, ab, conv_ref, alog_ref, dtb_ref, qkv_ref, gate_ref):
    pad = HALO
    rows = qkv_ref.shape[0]
    centre = B_CONV // 2
    y = jnp.zeros((rows, 3 * MIX_W), F32)
    for tap in range(B_CONV):
        y = y + xe_ref[pad + tap - centre:pad + tap - centre + rows, :] * conv_ref[tap:tap + 1, :]
    y = y * jax.nn.sigmoid(y)
    q, k = y[:, 0:MIX_W], y[:, MIX_W:2 * MIX_W]
    q = q * lax.rsqrt(_head_sums(q * q) + 1e-6) * (HEAD_DIM ** -0.5)
    k = k * lax.rsqrt(_head_sums(k * k) + 1e-6)
    qkv_ref[:, 0:MIX_W] = q.astype(BF16)
    qkv_ref[:, MIX_W:2 * MIX_W] = k.astype(BF16)
    qkv_ref[:, 2 * MIX_W:] = y[:, 2 * MIX_W:].astype(BF16)

    col = lax.broadcasted_iota(jnp.int32, (1, ab.shape[-1]), 1)
    xs = ab + dtb_ref[...]
    softplus = jnp.maximum(xs, 0.0) + jnp.log(1.0 + jnp.exp(-jnp.abs(xs)))
    g = -jnp.exp(alog_ref[...]) * softplus
    beta = jax.nn.sigmoid(ab)
    tri_f, tri_b = _chunk_tri(CUM_ROWS, False).astype(BF16), _chunk_tri(CUM_ROWS, True).astype(BF16)
    g_hi = g.astype(BF16)
    g_mid = (g - g_hi.astype(F32)).astype(BF16)
    g_lo = (g - g_hi.astype(F32) - g_mid.astype(F32)).astype(BF16)
    for r0 in range(0, rows, CUM_ROWS):
        pieces = [p[r0:r0 + CUM_ROWS] for p in (g_hi, g_mid, g_lo)]
        cum_f = sum(jnp.dot(tri_f, p, preferred_element_type=F32) for p in pieces)
        cum_b = sum(jnp.dot(tri_b, p, preferred_element_type=F32) for p in pieces)
        gate_ref[r0:r0 + CUM_ROWS, :] = jnp.where(col < 8, beta[r0:r0 + CUM_ROWS],
                                                   jnp.where(col < 12, cum_f, cum_b))


def _expand_heads(cols):
    ids = _head_ids(MIX_W)
    out = jnp.broadcast_to(cols[:, 0:1], (cols.shape[0], MIX_W))
    for h in range(1, N_HEADS):
        out = jnp.where(ids == h, jnp.broadcast_to(cols[:, h:h + 1], out.shape), out)
    return out


def _block_diag(packed, mask):
    return jnp.where(mask, jnp.concatenate([packed] * N_HEADS, axis=0), 0.0).astype(BF16)


def _mm(a, b):
    return jnp.dot(a.astype(BF16), b.astype(BF16), preferred_element_type=F32)


def _delta_local(s, backward, qkv_ref, gate_ref, grow_ref, u_ref, w_ref, qh_ref, kt_ref, in_ref, eg_ref, c):
    n = B_CHUNK
    d = 2 * s + (1 if backward else 0)
    r0 = pl.multiple_of(c * n, n)
    q = qkv_ref[s, pl.ds(r0, n), 0:MIX_W]
    k = qkv_ref[s, pl.ds(r0, n), MIX_W:2 * MIX_W]
    v = qkv_ref[s, pl.ds(r0, n), 2 * MIX_W:].astype(F32)
    gates = gate_ref[s, pl.ds(r0, n), :]
    d0 = 4 if backward else 0
    beta = _expand_heads(gates[:, d0:d0 + 4])
    gc = _expand_heads(gates[:, 8 + d0:12 + d0])
    gc_row = grow_ref[s, pl.ds(c, 1), d0 * n:(d0 + 4) * n]

    row = lax.broadcasted_iota(jnp.int32, (n, MIX_W), 0)
    col = lax.broadcasted_iota(jnp.int32, (n, MIX_W), 1) % n
    incl = (row <= col) if backward else (row >= col)
    strict = (row < col) if backward else (row > col)
    bd_mask = (lax.broadcasted_iota(jnp.int32, (MIX_W, MIX_W), 0) // n
               == lax.broadcasted_iota(jnp.int32, (MIX_W, MIX_W), 1) // n)
    decay = jnp.exp(jnp.where(incl, gc - gc_row, NEG_INF))

    kf = k.astype(F32)
    kq = lax.dot_general(jnp.concatenate([q, k], axis=0), _stack_heads(k), (((1,), (1,)), ((), ())),
                         preferred_element_type=F32)
    yield
    intra = kq[0:n] * decay
    low = jnp.where(strict, kq[n:] * beta * decay, 0.0)

    x = jnp.where(row == col, 1.0, 0.0) - low
    p = _mm(low, _block_diag(low, bd_mask))
    yield
    steps = int(math.log2(n)) - 1
    for step in range(steps):
        last = step == steps - 1
        lhs = x if last else jnp.concatenate([x, p], axis=0)
        r = _mm(lhs, _block_diag(p, bd_mask))
        yield
        x = x + r[0:n]
        if not last:
            p = r[n:]

    kbeta = kf * beta
    u_ref[d, pl.ds(r0, n), :] = _mm(x, _block_diag(v * beta, bd_mask))
    w_ref[d, pl.ds(r0, n), :] = _mm(x, _block_diag(kbeta * jnp.exp(gc), bd_mask)).astype(BF16)
    edge = gc[0:1] if backward else gc[n - 1:n]
    qh_ref[d, pl.ds(r0, n), :] = (q.astype(F32) * jnp.exp(gc)).astype(BF16)
    kt_ref[d, pl.ds(r0, n), :] = (kf * jnp.exp(edge - gc)).astype(BF16)
    in_ref[d, pl.ds(r0, n), :] = intra.astype(BF16)
    eg_ref[d, pl.ds(c, 1), :] = jnp.exp(edge)


def _delta_step(s, backward, u_ref, w_ref, qh_ref, kt_ref, in_ref, eg_ref, o_ref, s_ref, c):
    n = B_CHUNK
    d = 2 * s + (1 if backward else 0)
    r0 = pl.multiple_of(c * n, n)
    bd_mask = (lax.broadcasted_iota(jnp.int32, (MIX_W, MIX_W), 0) // n
               == lax.broadcasted_iota(jnp.int32, (MIX_W, MIX_W), 1) // n)
    state = s_ref[d]
    lhs = jnp.concatenate([w_ref[d, pl.ds(r0, n), :], qh_ref[d, pl.ds(r0, n), :]], axis=0)
    ws_qs = jnp.dot(lhs, state.astype(BF16), preferred_element_type=F32)
    yield
    v_new = u_ref[d, pl.ds(r0, n), :] - ws_qs[0:n]
    o_ref[s, pl.ds(r0, n), :] = ws_qs[n:] + jnp.dot(in_ref[d, pl.ds(r0, n), :], _block_diag(v_new, bd_mask),
                                                  preferred_element_type=F32)
    kv = lax.dot_general(kt_ref[d, pl.ds(r0, n), :], v_new.astype(BF16), (((0,), (0,)), ((), ())),
                         preferred_element_type=F32)
    s_ref[d] = state * eg_ref[d, pl.ds(c, 1), :] + jnp.where(bd_mask, kv, 0.0)


LOCAL_UNROLL = 8


def _delta_kernel(qkv_f, gate_f, grow_f, qkv_b, gate_b, grow_b, of_ref, ob_ref,
                  s_ref, u_ref, w_ref, qh_ref, kt_ref, in_ref, eg_ref):
    @pl.when(pl.program_id(1) == 0)
    def _():
        s_ref[...] = jnp.zeros(s_ref.shape, F32)

    nseq = qkv_f.shape[0]
    nchunk = qkv_f.shape[1] // B_CHUNK
    staged = (u_ref, w_ref, qh_ref, kt_ref, in_ref, eg_ref)

    for s in range(nseq):
        def local(i, carry, s=s):
            chunks = [i * LOCAL_UNROLL + g for g in range(LOCAL_UNROLL)]
            _interleave(*[_delta_local(s, False, qkv_f, gate_f, grow_f, *staged, c) for c in chunks],
                        *[_delta_local(s, True, qkv_b, gate_b, grow_b, *staged, c) for c in chunks])
            return carry

        lax.fori_loop(0, nchunk // LOCAL_UNROLL, local, 0)

    def step(i, carry):
        _interleave(*[_delta_step(s, False, *staged, of_ref, s_ref, i) for s in range(nseq)],
                    *[_delta_step(s, True, *staged, ob_ref, s_ref, nchunk - 1 - i) for s in range(nseq)])
        return carry

    lax.fori_loop(0, nchunk, step, 0)


def _delta(qkvn, gates, batch, seq, rows, nseq):
    t = batch * seq
    nblk = seq // rows
    nchunk = rows // B_CHUNK
    grow = gates[:, 8:16].reshape(t // B_CHUNK, B_CHUNK, 8).transpose(0, 2, 1).reshape(t // B_CHUNK, 8 * B_CHUNK)
    group = lambda a, n: a.reshape(batch // nseq, nseq, n, a.shape[-1])
    qkvn4, gates4, grow4 = group(qkvn, seq), group(gates, seq), group(grow, seq // B_CHUNK)
    fwd = lambda b, j: (b, 0, j, 0)
    bwd = lambda b, j: (b, 0, nblk - 1 - j, 0)
    blk = lambda n, w, m: pl.BlockSpec((None, nseq, n, w), m)
    specs = lambda m: [blk(rows, 3 * MIX_W, m), blk(rows, 16, m), blk(nchunk, 8 * B_CHUNK, m)]
    slots = 2 * nseq
    of, ob = pl.pallas_call(
        _delta_kernel,
        grid=(batch // nseq, nblk),
        in_specs=specs(fwd) + specs(bwd),
        out_specs=[blk(rows, MIX_W, fwd), blk(rows, MIX_W, bwd)],
        out_shape=[jax.ShapeDtypeStruct((batch // nseq, nseq, seq, MIX_W), F32)] * 2,
        scratch_shapes=([pltpu.VMEM((slots, MIX_W, MIX_W), F32), pltpu.VMEM((slots, rows, MIX_W), F32)]
                        + [pltpu.VMEM((slots, rows, MIX_W), BF16)] * 4 + [pltpu.VMEM((slots, nchunk, MIX_W), F32)]),
        compiler_params=_params("parallel", "arbitrary"),
    )(qkvn4, gates4, grow4, qkvn4, gates4, grow4)
    return of.reshape(t, MIX_W), ob.reshape(t, MIX_W)


def _merge_kernel(x_ref, gain_ref, oa0, oa1, oa2, la0, la1, la2, bf_ref, bb_ref, z_ref, bgain_ref,
                  oc_ref, od_ref, wg_ref, bg_ref, wbr_ref, wout_ref, out_ref, o_scr):
    x = x_ref[...]
    tm = x.shape[0]
    h = _rms_rows(x, gain_ref[...]).astype(BF16)

    slabs = MIX_W // LANES
    outs_a = [oa0[...].astype(F32)]
    for p, ((_, dil), oa) in enumerate(zip(A_PATTERNS[1:], (oa1, oa2))):
        for r in range(dil):
            for c in range(slabs):
                col = r * MIX_W + c * LANES
                o_scr[p, c, pl.ds(r, tm // dil, stride=dil), :] = oa[:, col:col + LANES].astype(F32)
        outs_a.append(jnp.concatenate([o_scr[p, c] for c in range(slabs)], axis=1))

    lses = [_expand_heads(r[...]) for r in (la0, la1, la2)]
    top = jnp.maximum(jnp.maximum(lses[0], lses[1]), lses[2])
    ws = [jnp.exp(l - top) for l in lses]
    o_a = sum(w * o for w, o in zip(ws, outs_a)) / (ws[0] + ws[1] + ws[2])

    o_b = bf_ref[...] + bb_ref[...]
    z = z_ref[...].astype(F32)
    o_b = (o_b * lax.rsqrt(_head_sums(o_b * o_b) * (1.0 / HEAD_DIM) + RMS_EPS) * bgain_ref[...]
           * (z * jax.nn.sigmoid(z)))

    branches = (o_a, o_b, oc_ref[...], od_ref[...])
    y = jnp.zeros(x.shape, F32)
    for i, o in enumerate(branches):
        gate = jax.nn.sigmoid(jnp.dot(h, wg_ref[i], preferred_element_type=F32) + bg_ref[i])
        y = y + gate * jnp.dot(o.astype(BF16), wbr_ref[i], preferred_element_type=F32)
    out_ref[...] = x + jnp.dot(y.astype(BF16), wout_ref[...], preferred_element_type=F32)


def _merge(x2d, a_outs, b_outs, z, oc, od, lw, tm):
    t = x2d.shape[0]
    tok = lambda w: pl.BlockSpec((tm, w), lambda i: (i, 0))
    (o0, l0), (o1, l1), (o2, l2) = a_outs
    acts = (o0, o1, o2, l0, l1, l2, b_outs[0], b_outs[1], z)
    weights = (lw["wg"], lw["bg"], lw["wbr"], lw["wout"])
    return pl.pallas_call(
        _merge_kernel,
        grid=(t // tm,),
        in_specs=([tok(D_MODEL), _const_spec(lw["norm_mix"].shape)]
                  + [pl.BlockSpec((tm * a.shape[0] // t, a.shape[1]), lambda i: (i, 0)) for a in acts]
                  + [_const_spec(lw["b_gain"].shape), tok(MIX_W), tok(MIX_W)]
                  + [_const_spec(w.shape) for w in weights]),
        out_specs=tok(D_MODEL),
        out_shape=jax.ShapeDtypeStruct((t, D_MODEL), F32),
        scratch_shapes=[pltpu.VMEM((len(A_PATTERNS) - 1, MIX_W // LANES, tm, LANES), F32)],
        compiler_params=_params("parallel"),
    )(x2d, lw["norm_mix"], *acts, lw["b_gain"], oc, od, *weights)


SUBLANES = 8
ROUTE_ROWS = 24
EXPERT_UNROLL = 2


def _moe_kernel(x_ref, gain_ref, wr_ref, br_ref, wgt_ref, wup_ref, wdn_ref, fgain_ref, out_ref,
                hs_ref, ws_ref, ys_ref, *, final, cap):
    x = x_ref[...]
    tm = x.shape[0]
    h = _rms_rows(x, gain_ref[...])
    hb = h.astype(BF16)
    h_lo = (h - hb.astype(F32)).astype(BF16)
    parts = jnp.dot(jnp.concatenate([hb, h_lo], axis=0), wr_ref[...], preferred_element_type=F32)
    logits = (parts[:tm, :LANES] + parts[:tm, LANES:]) + (parts[tm:, :LANES] + parts[tm:, LANES:]) + br_ref[...]
    lane = lax.broadcasted_iota(jnp.int32, (1, LANES), 1)

    lt = logits.T[0:ROUTE_ROWS]
    row = lax.broadcasted_iota(jnp.int32, (ROUTE_ROWS, 1), 0)

    def first_max(vals):
        top = jnp.max(vals, axis=0, keepdims=True)
        return top, jnp.min(jnp.where(vals == top, row, ROUTE_ROWS), axis=0, keepdims=True)

    gl = jnp.where(row < N_GROUPS, lt, NEG_INF)
    g_top, g_row = first_max(gl)
    g_prob = 1.0 / jnp.sum(jnp.exp(gl - g_top), axis=0, keepdims=True)
    in_group = (row >= N_GROUPS) & (row < N_GROUPS + N_EXPERTS) & ((row - N_GROUPS) // EXPERTS_PER_GROUP == g_row)
    el = jnp.where(in_group, lt, NEG_INF)
    top1, idx1 = first_max(el)
    top2, idx2 = first_max(jnp.where(row == idx1, NEG_INF, el))
    e2 = jnp.exp(top2 - top1)
    w1 = g_prob / (1.0 + e2)
    weights_t = jnp.where(row == idx1, w1, 0.0) + jnp.where(row == idx2, w1 * e2, 0.0)
    packed = jnp.concatenate([weights_t, jnp.broadcast_to(g_row.astype(F32), (SUBLANES, tm)),
                              jnp.zeros((LANES - ROUTE_ROWS - SUBLANES, tm), F32)], axis=0).T
    weights = jnp.where(lane < ROUTE_ROWS, packed, 0.0)
    g_idx = jnp.sum(jnp.where(lane == ROUTE_ROWS, packed, 0.0), axis=-1, keepdims=True).astype(jnp.int32)

    def expert(rows, row_weights, e, outs):
        w_e = jnp.sum(jnp.where(lane == e + N_GROUPS, row_weights, 0.0), axis=-1, keepdims=True)
        gate = jnp.dot(rows, wgt_ref[e], preferred_element_type=F32)
        up = jnp.dot(rows, wup_ref[e], preferred_element_type=F32)
        yield
        hid = gate * jax.nn.sigmoid(gate) * up * w_e
        outs.append(jnp.dot(hid.astype(BF16), wdn_ref[e], preferred_element_type=F32))

    def finish(moe_out):
        y = x + moe_out
        if final:
            y = _rms_rows(y, fgain_ref[...])
        out_ref[...] = y

    member = jnp.where(lane == g_idx, 1.0, 0.0)
    fits = jnp.max(jnp.sum(member, axis=0, keepdims=True)) <= cap

    @pl.when(fits)
    def _():
        before = lax.broadcasted_iota(jnp.int32, (tm, tm), 0) > lax.broadcasted_iota(jnp.int32, (tm, tm), 1)
        ahead = jnp.dot(jnp.where(before, 1.0, 0.0).astype(BF16), member.astype(BF16),
                        preferred_element_type=F32)
        pos = jnp.sum(member * ahead, axis=-1, keepdims=True).astype(jnp.int32)
        slot = g_idx * cap + pos
        perm_t = jnp.where(lax.broadcasted_iota(jnp.int32, (1, N_GROUPS * cap), 1) == slot, 1.0, 0.0).astype(BF16)
        slot_row = jnp.broadcast_to(slot.astype(F32), (tm, LANES)).T[0:1]
        perm = jnp.where(lax.broadcasted_iota(jnp.int32, (N_GROUPS * cap, 1), 0).astype(F32) == slot_row,
                         1.0, 0.0).astype(BF16)
        gather = lambda a: jnp.dot(perm, a, preferred_element_type=F32)
        hs_ref[...] = gather(hb).astype(BF16)
        w_hi = weights.astype(BF16)
        w_parts = gather(jnp.concatenate([w_hi, (weights - w_hi.astype(F32)).astype(BF16)], axis=1))
        ws_ref[...] = w_parts[:, :LANES] + w_parts[:, LANES:]

        def group(g, carry):
            r0 = pl.multiple_of(g * cap, cap)
            rows, row_weights = hs_ref[pl.ds(r0, cap), :], ws_ref[pl.ds(r0, cap), :]
            outs = []
            _interleave(*[expert(rows, row_weights, g * EXPERTS_PER_GROUP + j, outs)
                          for j in range(EXPERTS_PER_GROUP)])
            ys_ref[pl.ds(r0, cap), :] = sum(outs[1:], outs[0]).astype(BF16)
            return carry

        lax.fori_loop(0, N_GROUPS, group, 0)
        finish(jnp.dot(perm_t, ys_ref[...], preferred_element_type=F32))

    @pl.when(jnp.logical_not(fits))
    def _():
        def experts(i, acc):
            outs = []
            _interleave(*[expert(hb, weights, i * EXPERT_UNROLL + j, outs) for j in range(EXPERT_UNROLL)])
            return acc + sum(outs[1:], outs[0])

        finish(lax.fori_loop(0, N_EXPERTS // EXPERT_UNROLL, experts, jnp.zeros(x.shape, F32)))


def _moe(x2d, lw, final_gain, final, tm, cap):
    t = x2d.shape[0]
    tok = pl.BlockSpec((tm, D_MODEL), lambda i: (i, 0))
    once = lambda a: pl.BlockSpec(a.shape, lambda i: (0,) * a.ndim, pipeline_mode=pl.Buffered(1))
    weights = (lw["norm_ffn"], lw["w_route"], lw["b_route"], lw["w_ff_gate"], lw["w_ff_up"], lw["w_ff_down"],
               final_gain)
    return pl.pallas_call(
        functools.partial(_moe_kernel, final=final, cap=cap),
        grid=(t // tm,),
        in_specs=[tok] + [once(w) for w in weights],
        out_specs=tok,
        out_shape=jax.ShapeDtypeStruct((t, D_MODEL), F32),
        scratch_shapes=[pltpu.VMEM((N_GROUPS * cap, D_MODEL), BF16), pltpu.VMEM((N_GROUPS * cap, LANES), F32),
                        pltpu.VMEM((N_GROUPS * cap, D_MODEL), BF16)],
        compiler_params=_params("parallel"),
    )(x2d, *weights)


def _split_bf16(w):
    hi = w.astype(BF16)
    return jnp.concatenate([hi, (w - hi.astype(F32)).astype(BF16)], axis=1)


def _layer_weights(p, l):
    w_in = p["w_in"][l]
    offs = np.cumsum([0, 3 * MIX_W, 3 * MIX_W, MIX_W, 16, MIX_W, 2 * C_KV_HEADS * HEAD_DIM, 3 * MIX_W])
    sec = [w_in[:, offs[i]:offs[i + 1]] for i in range(7)]
    c_order = np.concatenate([np.arange(h * HEAD_DIM, (h + 1) * HEAD_DIM) for h in (0, 2, 1, 3)])
    row = lambda v: v.reshape(1, -1).astype(F32)
    gate16 = lambda v: jnp.concatenate([jnp.zeros((8,), F32), v.reshape(-1).astype(F32)]).reshape(1, 16)
    wbr = p["w_branch"][l]
    wbr = jnp.stack([wbr[0], wbr[1], wbr[2][c_order], wbr[3]])
    w_route = jnp.concatenate([p["w_route_group"][l], p["w_route_expert"][l].reshape(D_MODEL, N_EXPERTS)], axis=1)
    b_route = jnp.concatenate([p["b_route_group"][l], p["b_route_expert"][l].reshape(N_EXPERTS)])
    pad_lanes = lambda a: jnp.pad(a, [(0, 0)] * (a.ndim - 1) + [(0, LANES - a.shape[-1])])
    return dict(
        norm_mix=row(p["norm_mix"][l]),
        wa=sec[0].astype(BF16), wb=sec[1].astype(BF16), wz=sec[2].astype(BF16),
        wab=pad_lanes(sec[3]).astype(BF16), wcq=sec[4][:, c_order].astype(BF16), wck=sec[5][:, :C_KV_HEADS * HEAD_DIM].astype(BF16), wcvt=sec[5][:, C_KV_HEADS * HEAD_DIM:].T.astype(BF16),
        wd=sec[6].astype(BF16),
        cq_gain=row(jnp.tile(p["c_q_norm"][l], N_HEADS)), ck_gain=row(jnp.tile(p["c_k_norm"][l], C_KV_HEADS)),
        conv_w=p["conv_w"][l].astype(F32), a_log16=gate16(p["a_log"][l]), dt_bias16=gate16(p["dt_bias"][l]),
        b_gain=row(jnp.tile(p["b_norm"][l], N_HEADS)),
        wg=p["w_gate"][l].astype(BF16), bg=p["b_gate"][l].reshape(N_HEADS, 1, D_MODEL).astype(F32),
        wbr=wbr.astype(BF16), wout=p["w_out"][l].astype(BF16),
        norm_ffn=row(p["norm_ffn"][l]),
        w_route=_split_bf16(pad_lanes(w_route).astype(F32)), b_route=pad_lanes(b_route.reshape(1, -1)).astype(F32),
        w_ff_gate=p["w_ff_gate"][l].reshape(N_EXPERTS, D_MODEL, D_EXPERT).astype(BF16),
        w_ff_up=p["w_ff_up"][l].reshape(N_EXPERTS, D_MODEL, D_EXPERT).astype(BF16),
        w_ff_down=p["w_ff_down"][l].reshape(N_EXPERTS, D_EXPERT, D_MODEL).astype(BF16),
        d_rpb=p["d_rpb"][l],
    )


def _tiles(seq):
    tm = min(512, seq)
    share = tm // N_GROUPS
    moe_cap = (share + share // 4 + BF16_SUBLANES - 1) // BF16_SUBLANES * BF16_SUBLANES
    return dict(tm=tm, moe_cap=moe_cap, tq_c=min(256, seq), tk_c=min(512, seq), rows_d=min(8, seq // GRID_W),
                tq_a=128, rows_a=2048, rows_delta=min(1024, seq))


def _encoder(x, layers, final_gain):
    batch, seq, _ = x.shape
    ts = _tiles(seq)
    tables = _rotary_tables(seq)
    depth = len(layers)
    x2d = x.reshape(batch * seq, D_MODEL)
    for l, lw in enumerate(layers):
        a_views, (qkvn, z, gates, cq, ck, cvt, dqkv) = _in_proj(x2d, seq, lw, tables, ts["tm"])
        a_outs = [_band(a, batch, seq, window, dil, ts["tq_a"], ts["rows_a"])
                  for a, (window, dil) in zip(a_views, A_PATTERNS)]
        b_outs = _delta(qkvn, gates, batch, seq, ts["rows_delta"], 2 if batch % 2 == 0 else 1)
        oc = _gqa(cq, ck, cvt, batch, seq, ts["tq_c"], ts["tk_c"])
        od = _nbr(dqkv, _nbr_bias(lw["d_rpb"], min(NA_ROWS, seq // GRID_W)), batch, seq, ts["rows_d"])
        x2d = _merge(x2d, a_outs, b_outs, z, oc, od, lw, ts["tm"])
        x2d = _moe(x2d, lw, final_gain, l == depth - 1, ts["tm"], ts["moe_cap"])
    return x2d.reshape(batch, seq, D_MODEL)


def kernel(x_prompt, x_sample, norm_mix, w_in, conv_w, a_log, dt_bias, b_norm, c_q_norm, c_k_norm, d_rpb,
           w_gate, b_gate, w_branch, w_out, norm_ffn, w_route_group, b_route_group, w_route_expert,
           b_route_expert, w_ff_gate, w_ff_up, w_ff_down, norm_final):
    p = dict(norm_mix=norm_mix, w_in=w_in, conv_w=conv_w, a_log=a_log, dt_bias=dt_bias, b_norm=b_norm,
             c_q_norm=c_q_norm, c_k_norm=c_k_norm, d_rpb=d_rpb, w_gate=w_gate, b_gate=b_gate,
             w_branch=w_branch, w_out=w_out, norm_ffn=norm_ffn, w_route_group=w_route_group,
             b_route_group=b_route_group, w_route_expert=w_route_expert, b_route_expert=b_route_expert,
             w_ff_gate=w_ff_gate, w_ff_up=w_ff_up, w_ff_down=w_ff_down)
    layers = [_layer_weights(p, l) for l in range(norm_mix.shape[0])]
    final_gain = norm_final.reshape(1, -1).astype(F32)
    return _encoder(x_prompt, layers, final_gain), _encoder(x_sample, layers, final_gain)
```

```python
import functools
import math

import jax
import jax.numpy as jnp
import numpy as np
from jax import lax
from jax.experimental import pallas as pl
from jax.experimental.pallas import tpu as pltpu

D_MODEL = 1024
HEAD_DIM = 64
N_HEADS = 4
MIX_W = N_HEADS * HEAD_DIM
GRID_W = 64
RMS_EPS = 1e-6
NEG_INF = -1e30
LOG2_E = math.log2(math.e)

A_PATTERNS = ((128, 1), (512, 4), (2048, 16))
ROPE_THETA = 500000.0
ROPE_DIMS = HEAD_DIM // 4
AXIAL_THETA = 10000.0
C_KV_HEADS = 2
B_CONV = 5
B_CHUNK = 64
NA_ROWS = 8
NA_COLS = 16
N_GROUPS = 4
EXPERTS_PER_GROUP = 4
N_EXPERTS = N_GROUPS * EXPERTS_PER_GROUP
D_EXPERT = 256

LANES = 128
BF16_SUBLANES = 16
HALO = 8
VMEM_LIMIT = 56 * 1024 * 1024

F32 = jnp.float32
BF16 = jnp.bfloat16


def _params(*sem):
    return pltpu.CompilerParams(dimension_semantics=sem, vmem_limit_bytes=VMEM_LIMIT)


def _const_spec(shape):
    n = len(shape)
    return pl.BlockSpec(shape, lambda *_: (0,) * n)


def _rms_rows(x, gain):
    return x * lax.rsqrt(jnp.mean(x * x, axis=-1, keepdims=True) + RMS_EPS) * gain


def _head_ids(width):
    return lax.broadcasted_iota(jnp.int32, (1, width), 1) // HEAD_DIM


def _block_ones(width):
    r = lax.broadcasted_iota(jnp.int32, (width, width), 0) // HEAD_DIM
    c = lax.broadcasted_iota(jnp.int32, (width, width), 1) // HEAD_DIM
    return jnp.where(r == c, 1.0, 0.0).astype(BF16)


def _head_sums(x):
    return jnp.dot(x.astype(BF16), _block_ones(x.shape[-1]), preferred_element_type=F32)


def _interleave(*chains):
    live = list(chains)
    while live:
        live = [c for c in live if next(c, live) is not live]


def _rotate(x, cos, sin_lo, sin_hi, shift):
    w = x.shape[-1]
    return x * cos + pltpu.roll(x, shift, 1) * sin_hi + pltpu.roll(x, w - shift, 1) * sin_lo


def _in_proj_kernel(x_ref, xprev_ref, xnext_ref, gain_ref, wa_ref, wb_ref, wz_ref, wab_ref, wcq_ref, wck_ref,
                    wcvt_ref, wd_ref, cqg_ref, ckg_ref, conv_ref, alog_ref, dtb_ref, rot_a_ref, rot_c_ref,
                    a1_ref, a4_ref, a16_ref, b_ref, z_ref, gate_ref, cq_ref, ck_ref, cvt_ref, d_ref,
                    a_scr, xe_ref, *, nblk):
    h = _rms_rows(x_ref[...], gain_ref[...]).astype(BF16)
    dot = lambda w_ref: jnp.dot(h, w_ref[...], preferred_element_type=F32)
    scale = HEAD_DIM ** -0.5
    tm = x_ref.shape[0]

    halo = jnp.concatenate([xprev_ref[...], xnext_ref[...]], axis=0)
    ub_halo = jnp.dot(_rms_rows(halo, gain_ref[...]).astype(BF16), wb_ref[...], preferred_element_type=F32)
    pos = pl.program_id(0) % nblk
    xe_ref[0:HALO, :] = ub_halo[0:HALO] * jnp.where(pos > 0, 1.0, 0.0)
    xe_ref[HALO:HALO + tm, :] = dot(wb_ref)
    xe_ref[HALO + tm:, :] = ub_halo[HALO:] * jnp.where(pos < nblk - 1, 1.0, 0.0)
    _delta_prep(xe_ref, dot(wab_ref)[:, :gate_ref.shape[-1]], conv_ref, alog_ref, dtb_ref, b_ref, gate_ref)

    ua = dot(wa_ref)
    cos, s_lo, s_hi = rot_a_ref[0], rot_a_ref[1], rot_a_ref[2]
    half = ROPE_DIMS // 2
    qkv_a = (_rotate(ua[:, 0:MIX_W], cos, s_lo, s_hi, half) * scale,
             _rotate(ua[:, MIX_W:2 * MIX_W], cos, s_lo, s_hi, half), ua[:, 2 * MIX_W:])
    a1_ref[...] = jnp.concatenate(qkv_a, axis=1).astype(BF16)
    slabs = 3 * MIX_W // LANES
    for c in range(slabs):
        part, off = divmod(c * LANES, MIX_W)
        a_scr[c] = qkv_a[part][:, off:off + LANES]
    for (_, dil), a_ref in zip(A_PATTERNS[1:], (a4_ref, a16_ref)):
        for r in range(dil):
            for c in range(slabs):
                col = r * 3 * MIX_W + c * LANES
                a_ref[:, col:col + LANES] = a_scr[c, pl.ds(r, tm // dil, stride=dil), :].astype(BF16)

    z_ref[...] = dot(wz_ref).astype(BF16)

    cos, s_lo, s_hi = rot_c_ref[0], rot_c_ref[1], rot_c_ref[2]
    quarter = HEAD_DIM // 4
    uq = dot(wcq_ref)
    uq = uq * lax.rsqrt(_head_sums(uq * uq) * (1.0 / HEAD_DIM) + RMS_EPS) * cqg_ref[...]
    cq_ref[...] = (_rotate(uq, cos, s_lo, s_hi, quarter) * (scale * LOG2_E)).astype(BF16)
    kw = C_KV_HEADS * HEAD_DIM
    uk = dot(wck_ref)
    uk = uk * lax.rsqrt(_head_sums(uk * uk) * (1.0 / HEAD_DIM) + RMS_EPS) * ckg_ref[...]
    ck_ref[...] = _rotate(uk, cos[:, :kw], s_lo[:, :kw], s_hi[:, :kw], quarter).astype(BF16)
    cvt_ref[...] = lax.dot_general(wcvt_ref[...], h, (((1,), (1,)), ((), ())),
                                   preferred_element_type=F32).astype(BF16)

    ud = dot(wd_ref)
    d_ref[:, 0:MIX_W] = (ud[:, 0:MIX_W] * scale).astype(BF16)
    d_ref[:, MIX_W:] = ud[:, MIX_W:].astype(BF16)


def _rotary_tables(seq):
    pos = np.arange(seq, dtype=np.float64)

    def table(pos_of_lane, theta, half, lane_in_group, active):
        freq_idx = lane_in_group % half
        inv = theta ** (-(freq_idx.astype(np.float64)) / half)
        ang = pos_of_lane * inv[None, :]
        cos = np.where(active[None, :], np.cos(ang), 1.0)
        sin = np.where(active[None, :], np.sin(ang), 0.0)
        low = (lane_in_group < half)[None, :]
        return np.stack([cos, np.where(low, -sin, 0.0), np.where(low, 0.0, sin)]).astype(np.float32)

    lane = np.arange(MIX_W) % HEAD_DIM
    rot_a = table(np.broadcast_to(pos[:, None], (seq, MIX_W)), ROPE_THETA, ROPE_DIMS // 2, lane % ROPE_DIMS,
                  lane < ROPE_DIMS)
    rows, cols = pos // GRID_W, pos % GRID_W
    half = HEAD_DIM // 2
    pos_c = np.where((lane < half)[None, :], rows[:, None], cols[:, None])
    rot_c = table(pos_c, AXIAL_THETA, half // 2, lane % half, np.ones(MIX_W, bool))
    return jnp.asarray(rot_a), jnp.asarray(rot_c)


def _in_proj(x2d, seq, lw, tables, tm):
    t = x2d.shape[0]
    rot_a, rot_c = tables
    nblk = seq // tm
    tok = lambda w: pl.BlockSpec((tm, w), lambda i: (i, 0))
    rot = pl.BlockSpec((3, tm, MIX_W), lambda i: (0, i % nblk, 0))
    weights = (lw["norm_mix"], lw["wa"], lw["wb"], lw["wz"], lw["wab"], lw["wcq"], lw["wck"], lw["wcvt"], lw["wd"],
               lw["cq_gain"], lw["ck_gain"], lw["conv_w"], lw["a_log16"], lw["dt_bias16"])
    kw = C_KV_HEADS * HEAD_DIM
    per = tm // HALO
    halo_specs = [pl.BlockSpec((HALO, D_MODEL), lambda i: (jnp.maximum(i * per - 1, 0), 0)),
                  pl.BlockSpec((HALO, D_MODEL), lambda i: (jnp.minimum((i + 1) * per, t // HALO - 1), 0))]
    outs = ((3 * MIX_W, BF16), (MIX_W, BF16), (16, F32), (MIX_W, BF16), (kw, BF16), None, (3 * MIX_W, BF16))
    out_specs = [pl.BlockSpec((kw, tm), lambda i: (0, i)) if o is None else tok(o[0]) for o in outs]
    out_shape = [jax.ShapeDtypeStruct((kw, t), BF16) if o is None else jax.ShapeDtypeStruct((t, o[0]), o[1])
                 for o in outs]
    a_specs = [pl.BlockSpec((tm // dil, dil * 3 * MIX_W), lambda i: (i, 0)) for _, dil in A_PATTERNS]
    a_shape = [jax.ShapeDtypeStruct((t // dil, dil * 3 * MIX_W), BF16) for _, dil in A_PATTERNS]
    res = pl.pallas_call(
        functools.partial(_in_proj_kernel, nblk=nblk),
        grid=(t // tm,),
        in_specs=[tok(D_MODEL)] + halo_specs + [_const_spec(w.shape) for w in weights] + [rot, rot],
        out_specs=a_specs + out_specs,
        out_shape=a_shape + out_shape,
        scratch_shapes=[pltpu.VMEM((3 * MIX_W // LANES, tm, LANES), F32),
                        pltpu.VMEM((tm + 2 * HALO, 3 * MIX_W), F32)],
        compiler_params=_params("parallel"),
    )(x2d, x2d, x2d, *weights, rot_a, rot_c)
    return res[:len(A_PATTERNS)], res[len(A_PATTERNS):]


def _gqa_kernel(q_ref, k_ref, vt_ref, o_ref, acc_ref, s_ref, *, tk):
    tq = q_ref.shape[0]
    seq = k_ref.shape[0]
    kw = C_KV_HEADS * HEAD_DIM
    qt = q_ref[...].astype(F32).T
    row_g = lax.broadcasted_iota(jnp.int32, (kw, 1), 0) // HEAD_DIM
    qs = jnp.concatenate([jnp.where(row_g == g, qt[blk * kw:(blk + 1) * kw], 0.0)
                          for g in range(C_KV_HEADS) for blk in range(2)], axis=1).astype(BF16)
    acc_ref[...] = jnp.zeros(acc_ref.shape, F32)

    width = 2 * tq

    nchunk = seq // tk

    def scores(j, slot):
        start = pl.multiple_of(j * tk, tk)
        s_ref[slot] = jnp.dot(k_ref[pl.ds(start, tk), :], qs, preferred_element_type=F32)

    def kv_head(g, j, slot, m_old, l_old, out):
        cols = slice(g * width, (g + 1) * width)
        start = pl.multiple_of(j * tk, tk)
        s = s_ref[slot, :, cols]
        m_new = jnp.maximum(m_old, jnp.max(s, axis=0, keepdims=True))
        alpha = jnp.exp2(m_old - m_new)
        p = jnp.exp2(s - m_new)
        pv = jnp.dot(vt_ref[g * HEAD_DIM:(g + 1) * HEAD_DIM, pl.ds(start, tk)], p.astype(BF16),
                     preferred_element_type=F32)
        yield
        acc_ref[:, cols] = acc_ref[:, cols] * alpha + pv
        out[g] = (m_new, l_old * alpha + jnp.sum(p, axis=0, keepdims=True))

    def softmax_chunk(j, slot, carry):
        out = [None] * C_KV_HEADS
        _interleave(*[kv_head(g, j, slot, *carry[g], out) for g in range(C_KV_HEADS)])
        return tuple(out)

    def chunk_pair(i, carry, last):
        scores(2 * i + 1, 1)
        carry = softmax_chunk(2 * i, 0, carry)
        if not last:
            scores(2 * i + 2, 0)
        return softmax_chunk(2 * i + 1, 1, carry)

    scores(0, 0)
    carry = ((jnp.full((1, width), NEG_INF, F32), jnp.zeros((1, width), F32)),) * C_KV_HEADS
    if nchunk == 1:
        carry = softmax_chunk(0, 0, carry)
    else:
        assert nchunk % 2 == 0
        carry = lax.fori_loop(0, nchunk // 2 - 1, lambda i, c: chunk_pair(i, c, False), carry)
        carry = chunk_pair(nchunk // 2 - 1, carry, True)
    o = acc_ref[...] / jnp.concatenate([l for _, l in carry], axis=1)
    for blk in range(2):
        pair = jnp.concatenate([o[:, (2 * g + blk) * tq:(2 * g + blk + 1) * tq] for g in range(C_KV_HEADS)], axis=0)
        o_ref[:, blk * kw:(blk + 1) * kw] = pair.T.astype(o_ref.dtype)


def _gqa(cq, ck, cvt, batch, seq, tq, tk):
    nq = seq // tq
    kw = C_KV_HEADS * HEAD_DIM
    return pl.pallas_call(
        functools.partial(_gqa_kernel, tk=tk),
        grid=(batch, nq),
        in_specs=[pl.BlockSpec((tq, MIX_W), lambda b, i: (b * nq + i, 0)),
                  pl.BlockSpec((seq, kw), lambda b, i: (b, 0)),
                  pl.BlockSpec((kw, seq), lambda b, i: (0, b))],
        out_specs=pl.BlockSpec((tq, MIX_W), lambda b, i: (b * nq + i, 0)),
        out_shape=jax.ShapeDtypeStruct((batch * seq, MIX_W), BF16),
        scratch_shapes=[pltpu.VMEM((HEAD_DIM, 4 * tq), F32), pltpu.VMEM((2, tk, 4 * tq), F32)],
        compiler_params=_params("parallel", "arbitrary"),
    )(cq, ck, cvt)


def _stack_heads(q):
    ids = _head_ids(MIX_W)
    zero = jnp.zeros((), q.dtype)
    return jnp.concatenate([jnp.where(ids == h, q, zero) for h in range(N_HEADS)], axis=0)


def _unstack_heads(o, n):
    ids = _head_ids(MIX_W)
    out = o[0:n]
    for h in range(1, N_HEADS):
        out = jnp.where(ids == h, o[h * n:(h + 1) * n], out)
    return out


NBR_UNROLL = 8


def _nbr_kernel(q_ref, k_ref, v_ref, bias_ref, o_ref, *, rows_per_step, grid_rows, kr):
    j = pl.program_id(1)
    win = kr * GRID_W

    def query_row(i):
        r = j * rows_per_step + i
        rs = jnp.clip(r - kr // 2, 0, grid_rows - kr)
        kstart = pl.multiple_of(rs * GRID_W, GRID_W)
        qstart = pl.multiple_of(i * GRID_W, GRID_W)
        qs = _stack_heads(q_ref[pl.ds(qstart, GRID_W), :])
        s = lax.dot_general(qs, k_ref[pl.ds(kstart, win), :], (((1,), (1,)), ((), ())),
                            preferred_element_type=F32)
        yield
        s = s + bias_ref[r - rs]
        m = jnp.max(s, axis=-1, keepdims=True)
        p = jnp.exp(s - m)
        l = jnp.sum(p, axis=-1, keepdims=True)
        o = jnp.dot(p.astype(BF16), v_ref[pl.ds(kstart, win), :], preferred_element_type=F32)
        yield
        o_ref[pl.ds(qstart, GRID_W), :] = _unstack_heads(o / l, GRID_W).astype(o_ref.dtype)

    def body(i, carry):
        _interleave(*[query_row(i * NBR_UNROLL + u) for u in range(NBR_UNROLL)])
        return carry

    lax.fori_loop(0, rows_per_step // NBR_UNROLL, body, 0)


def _nbr_bias(rpb, kr):
    c = np.arange(GRID_W)
    cs = np.clip(c - NA_COLS // 2, 0, GRID_W - NA_COLS)
    col_valid = (c[None, :] >= cs[:, None]) & (c[None, :] < cs[:, None] + NA_COLS)
    dc = np.clip(c[None, :] - c[:, None], -(NA_COLS - 1), NA_COLS - 1) + NA_COLS - 1
    dr = np.arange(kr)[None, :] - np.arange(kr)[:, None] + NA_ROWS - 1
    pick_r = (dr[..., None] == np.arange(2 * NA_ROWS - 1)).astype(np.float32)
    pick_c = (dc[..., None] == np.arange(2 * NA_COLS - 1)).astype(np.float32)
    bias = jnp.einsum('hab,vja,qkb->vhqjk', rpb.astype(F32), pick_r, pick_c, precision=lax.Precision.HIGHEST)
    bias = jnp.where(col_valid[None, None, :, None, :], bias, NEG_INF)
    return bias.reshape(kr, N_HEADS * GRID_W, kr * GRID_W)


def _nbr(dqkv, bias, batch, seq, rows_per_step):
    grid_rows = seq // GRID_W
    kr = min(NA_ROWS, grid_rows)
    nblk = grid_rows // rows_per_step
    tq = rows_per_step * GRID_W
    return pl.pallas_call(
        functools.partial(_nbr_kernel, rows_per_step=rows_per_step, grid_rows=grid_rows, kr=kr),
        grid=(batch, nblk),
        in_specs=[pl.BlockSpec((tq, MIX_W), lambda b, i: (b * nblk + i, 0)),
                  pl.BlockSpec((seq, MIX_W), lambda b, i: (b, 1)),
                  pl.BlockSpec((seq, MIX_W), lambda b, i: (b, 2)),
                  _const_spec(bias.shape)],
        out_specs=pl.BlockSpec((tq, MIX_W), lambda b, i: (b * nblk + i, 0)),
        out_shape=jax.ShapeDtypeStruct((batch * seq, MIX_W), BF16),
        compiler_params=_params("parallel", "arbitrary"),
    )(dqkv, dqkv, dqkv, bias)


BAND_UNROLL = 4


def _band_kernel(qkv_ref, o_ref, lse_ref, *, half, kwin, tq):
    length = qkv_ref.shape[0]
    rows = o_ref.shape[0]
    nq = rows // tq
    row0 = pl.program_id(1) * rows

    def unit(u):
        r, qi = u // nq, u % nq
        col = pl.multiple_of(r * (3 * MIX_W), LANES)
        ocol = pl.multiple_of(r * MIX_W, LANES)
        q_local = pl.multiple_of(qi * tq, tq)
        i0 = row0 + q_local
        kstart = pl.multiple_of(jnp.clip(i0 - half, 0, length - kwin), BF16_SUBLANES)
        qs = _stack_heads(qkv_ref[pl.ds(pl.multiple_of(i0, tq), tq), pl.ds(col, MIX_W)])
        s = lax.dot_general(qs, qkv_ref[pl.ds(kstart, kwin), pl.ds(col + MIX_W, MIX_W)],
                            (((1,), (1,)), ((), ())), preferred_element_type=F32)
        yield
        qpos = i0 + lax.broadcasted_iota(jnp.int32, (N_HEADS * tq, kwin), 0) % tq
        kpos = kstart + lax.broadcasted_iota(jnp.int32, (N_HEADS * tq, kwin), 1)
        s = jnp.where(jnp.abs(qpos - kpos) <= half, s, NEG_INF)
        m = jnp.max(s, axis=-1, keepdims=True)
        p = jnp.exp(s - m)
        l = jnp.sum(p, axis=-1, keepdims=True)
        o = jnp.dot(p.astype(BF16), qkv_ref[pl.ds(kstart, kwin), pl.ds(col + 2 * MIX_W, MIX_W)],
                    preferred_element_type=F32)
        yield
        o_ref[pl.ds(q_local, tq), pl.ds(ocol, MIX_W)] = _unstack_heads(o / l, tq).astype(o_ref.dtype)
        lse = m + jnp.log(l)
        lane = lax.broadcasted_iota(jnp.int32, (1, lse_ref.shape[1]), 1)
        packed = lse_ref[pl.ds(q_local, tq), :]
        for h in range(N_HEADS):
            packed = jnp.where(lane == r * N_HEADS + h, lse[h * tq:(h + 1) * tq], packed)
        lse_ref[pl.ds(q_local, tq), :] = packed

    lse_ref[...] = jnp.zeros(lse_ref.shape, F32)
    units = (o_ref.shape[1] // MIX_W) * nq

    def body(i, carry):
        _interleave(*[unit(i * BAND_UNROLL + j) for j in range(BAND_UNROLL)])
        return carry

    lax.fori_loop(0, units // BAND_UNROLL, body, 0)


def _band(a, batch, seq, window, dil, tq, step_rows):
    half = window // (2 * dil)
    length = seq // dil
    tq = min(tq, length)
    kwin = min(tq + 2 * half, length)
    rows = min(step_rows, length) if dil == 1 else length
    view = a.reshape(batch, length, dil * 3 * MIX_W)
    o, lse = pl.pallas_call(
        functools.partial(_band_kernel, half=half, kwin=kwin, tq=tq),
        grid=(batch, length // rows),
        in_specs=[pl.BlockSpec((None, length, dil * 3 * MIX_W), lambda b, i: (b, 0, 0))],
        out_specs=[pl.BlockSpec((None, rows, dil * MIX_W), lambda b, i: (b, i, 0)),
                   pl.BlockSpec((None, rows, dil * N_HEADS), lambda b, i: (b, i, 0))],
        out_shape=[jax.ShapeDtypeStruct((batch, length, dil * MIX_W), BF16),
                   jax.ShapeDtypeStruct((batch, length, dil * N_HEADS), F32)],
        compiler_params=_params("parallel", "arbitrary"),
    )(view)
    return o.reshape(batch * length, dil * MIX_W), lse.reshape(batch * seq, N_HEADS)


CUM_ROWS = 256


def _chunk_tri(n, upper):
    r = lax.broadcasted_iota(jnp.int32, (n, n), 0)
    c = lax.broadcasted_iota(jnp.int32, (n, n), 1)
    same = (r // B_CHUNK) == (c // B_CHUNK)
    keep = (r <= c) if upper else (r >= c)
    return jnp.where(same & keep, 1.0, 0.0).astype(F32)


def _delta_prep(xe_ref, ab, conv_ref, alog_ref, dtb_ref, qkv_ref, gate_ref):
    pad = HALO
    rows = qkv_ref.shape[0]
    centre = B_CONV // 2
    y = jnp.zeros((rows, 3 * MIX_W), F32)
    for tap in range(B_CONV):
        y = y + xe_ref[pad + tap - centre:pad + tap - centre + rows, :] * conv_ref[tap:tap + 1, :]
    y = y * jax.nn.sigmoid(y)
    q, k = y[:, 0:MIX_W], y[:, MIX_W:2 * MIX_W]
    q = q * lax.rsqrt(_head_sums(q * q) + 1e-6) * (HEAD_DIM ** -0.5)
    k = k * lax.rsqrt(_head_sums(k * k) + 1e-6)
    qkv_ref[:, 0:MIX_W] = q.astype(BF16)
    qkv_ref[:, MIX_W:2 * MIX_W] = k.astype(BF16)
    qkv_ref[:, 2 * MIX_W:] = y[:, 2 * MIX_W:].astype(BF16)

    col = lax.broadcasted_iota(jnp.int32, (1, ab.shape[-1]), 1)
    xs = ab + dtb_ref[...]
    softplus = jnp.maximum(xs, 0.0) + jnp.log(1.0 + jnp.exp(-jnp.abs(xs)))
    g = -jnp.exp(alog_ref[...]) * softplus
    beta = jax.nn.sigmoid(ab)
    tri_f, tri_b = _chunk_tri(CUM_ROWS, False).astype(BF16), _chunk_tri(CUM_ROWS, True).astype(BF16)
    g_hi = g.astype(BF16)
    g_mid = (g - g_hi.astype(F32)).astype(BF16)
    g_lo = (g - g_hi.astype(F32) - g_mid.astype(F32)).astype(BF16)
    for r0 in range(0, rows, CUM_ROWS):
        pieces = [p[r0:r0 + CUM_ROWS] for p in (g_hi, g_mid, g_lo)]
        cum_f = sum(jnp.dot(tri_f, p, preferred_element_type=F32) for p in pieces)
        cum_b = sum(jnp.dot(tri_b, p, preferred_element_type=F32) for p in pieces)
        gate_ref[r0:r0 + CUM_ROWS, :] = jnp.where(col < 8, beta[r0:r0 + CUM_ROWS],
                                                   jnp.where(col < 12, cum_f, cum_b))


def _expand_heads(cols):
    ids = _head_ids(MIX_W)
    out = jnp.broadcast_to(cols[:, 0:1], (cols.shape[0], MIX_W))
    for h in range(1, N_HEADS):
        out = jnp.where(ids == h, jnp.broadcast_to(cols[:, h:h + 1], out.shape), out)
    return out


def _block_diag(packed, mask):
    return jnp.where(mask, jnp.concatenate([packed] * N_HEADS, axis=0), 0.0).astype(BF16)


def _mm(a, b):
    return jnp.dot(a.astype(BF16), b.astype(BF16), preferred_element_type=F32)


def _delta_local(s, backward, qkv_ref, gate_ref, grow_ref, u_ref, w_ref, qh_ref, kt_ref, in_ref, eg_ref, c):
    n = B_CHUNK
    d = 2 * s + (1 if backward else 0)
    r0 = pl.multiple_of(c * n, n)
    q = qkv_ref[s, pl.ds(r0, n), 0:MIX_W]
    k = qkv_ref[s, pl.ds(r0, n), MIX_W:2 * MIX_W]
    v = qkv_ref[s, pl.ds(r0, n), 2 * MIX_W:].astype(F32)
    gates = gate_ref[s, pl.ds(r0, n), :]
    d0 = 4 if backward else 0
    beta = _expand_heads(gates[:, d0:d0 + 4])
    gc = _expand_heads(gates[:, 8 + d0:12 + d0])
    gc_row = grow_ref[s, pl.ds(c, 1), d0 * n:(d0 + 4) * n]

    row = lax.broadcasted_iota(jnp.int32, (n, MIX_W), 0)
    col = lax.broadcasted_iota(jnp.int32, (n, MIX_W), 1) % n
    incl = (row <= col) if backward else (row >= col)
    strict = (row < col) if backward else (row > col)
    bd_mask = (lax.broadcasted_iota(jnp.int32, (MIX_W, MIX_W), 0) // n
               == lax.broadcasted_iota(jnp.int32, (MIX_W, MIX_W), 1) // n)
    decay = jnp.exp(jnp.where(incl, gc - gc_row, NEG_INF))

    kf = k.astype(F32)
    kq = lax.dot_general(jnp.concatenate([q, k], axis=0), _stack_heads(k), (((1,), (1,)), ((), ())),
                         preferred_element_type=F32)
    yield
    intra = kq[0:n] * decay
    low = jnp.where(strict, kq[n:] * beta * decay, 0.0)

    x = jnp.where(row == col, 1.0, 0.0) - low
    p = _mm(low, _block_diag(low, bd_mask))
    yield
    steps = int(math.log2(n)) - 1
    for step in range(steps):
        last = step == steps - 1
        lhs = x if last else jnp.concatenate([x, p], axis=0)
        r = _mm(lhs, _block_diag(p, bd_mask))
        yield
        x = x + r[0:n]
        if not last:
            p = r[n:]

    kbeta = kf * beta
    u_ref[d, pl.ds(r0, n), :] = _mm(x, _block_diag(v * beta, bd_mask))
    w_ref[d, pl.ds(r0, n), :] = _mm(x, _block_diag(kbeta * jnp.exp(gc), bd_mask)).astype(BF16)
    edge = gc[0:1] if backward else gc[n - 1:n]
    qh_ref[d, pl.ds(r0, n), :] = (q.astype(F32) * jnp.exp(gc)).astype(BF16)
    kt_ref[d, pl.ds(r0, n), :] = (kf * jnp.exp(edge - gc)).astype(BF16)
    in_ref[d, pl.ds(r0, n), :] = intra.astype(BF16)
    eg_ref[d, pl.ds(c, 1), :] = jnp.exp(edge)


def _delta_step(s, backward, u_ref, w_ref, qh_ref, kt_ref, in_ref, eg_ref, o_ref, s_ref, c):
    n = B_CHUNK
    d = 2 * s + (1 if backward else 0)
    r0 = pl.multiple_of(c * n, n)
    bd_mask = (lax.broadcasted_iota(jnp.int32, (MIX_W, MIX_W), 0) // n
               == lax.broadcasted_iota(jnp.int32, (MIX_W, MIX_W), 1) // n)
    state = s_ref[d]
    lhs = jnp.concatenate([w_ref[d, pl.ds(r0, n), :], qh_ref[d, pl.ds(r0, n), :]], axis=0)
    ws_qs = jnp.dot(lhs, state.astype(BF16), preferred_element_type=F32)
    yield
    v_new = u_ref[d, pl.ds(r0, n), :] - ws_qs[0:n]
    o_ref[s, pl.ds(r0, n), :] = ws_qs[n:] + jnp.dot(in_ref[d, pl.ds(r0, n), :], _block_diag(v_new, bd_mask),
                                                  preferred_element_type=F32)
    kv = lax.dot_general(kt_ref[d, pl.ds(r0, n), :], v_new.astype(BF16), (((0,), (0,)), ((), ())),
                         preferred_element_type=F32)
    s_ref[d] = state * eg_ref[d, pl.ds(c, 1), :] + jnp.where(bd_mask, kv, 0.0)


LOCAL_UNROLL = 8


def _delta_kernel(qkv_f, gate_f, grow_f, qkv_b, gate_b, grow_b, of_ref, ob_ref,
                  s_ref, u_ref, w_ref, qh_ref, kt_ref, in_ref, eg_ref):
    @pl.when(pl.program_id(1) == 0)
    def _():
        s_ref[...] = jnp.zeros(s_ref.shape, F32)

    nseq = qkv_f.shape[0]
    nchunk = qkv_f.shape[1] // B_CHUNK
    staged = (u_ref, w_ref, qh_ref, kt_ref, in_ref, eg_ref)

    for s in range(nseq):
        def local(i, carry, s=s):
            chunks = [i * LOCAL_UNROLL + g for g in range(LOCAL_UNROLL)]
            _interleave(*[_delta_local(s, False, qkv_f, gate_f, grow_f, *staged, c) for c in chunks],
                        *[_delta_local(s, True, qkv_b, gate_b, grow_b, *staged, c) for c in chunks])
            return carry

        lax.fori_loop(0, nchunk // LOCAL_UNROLL, local, 0)

    def step(i, carry):
        _interleave(*[_delta_step(s, False, *staged, of_ref, s_ref, i) for s in range(nseq)],
                    *[_delta_step(s, True, *staged, ob_ref, s_ref, nchunk - 1 - i) for s in range(nseq)])
        return carry

    lax.fori_loop(0, nchunk, step, 0)


def _delta(qkvn, gates, batch, seq, rows, nseq):
    t = batch * seq
    nblk = seq // rows
    nchunk = rows // B_CHUNK
    grow = gates[:, 8:16].reshape(t // B_CHUNK, B_CHUNK, 8).transpose(0, 2, 1).reshape(t // B_CHUNK, 8 * B_CHUNK)
    group = lambda a, n: a.reshape(batch // nseq, nseq, n, a.shape[-1])
    qkvn4, gates4, grow4 = group(qkvn, seq), group(gates, seq), group(grow, seq // B_CHUNK)
    fwd = lambda b, j: (b, 0, j, 0)
    bwd = lambda b, j: (b, 0, nblk - 1 - j, 0)
    blk = lambda n, w, m: pl.BlockSpec((None, nseq, n, w), m)
    specs = lambda m: [blk(rows, 3 * MIX_W, m), blk(rows, 16, m), blk(nchunk, 8 * B_CHUNK, m)]
    slots = 2 * nseq
    of, ob = pl.pallas_call(
        _delta_kernel,
        grid=(batch // nseq, nblk),
        in_specs=specs(fwd) + specs(bwd),
        out_specs=[blk(rows, MIX_W, fwd), blk(rows, MIX_W, bwd)],
        out_shape=[jax.ShapeDtypeStruct((batch // nseq, nseq, seq, MIX_W), F32)] * 2,
        scratch_shapes=([pltpu.VMEM((slots, MIX_W, MIX_W), F32), pltpu.VMEM((slots, rows, MIX_W), F32)]
                        + [pltpu.VMEM((slots, rows, MIX_W), BF16)] * 4 + [pltpu.VMEM((slots, nchunk, MIX_W), F32)]),
        compiler_params=_params("parallel", "arbitrary"),
    )(qkvn4, gates4, grow4, qkvn4, gates4, grow4)
    return of.reshape(t, MIX_W), ob.reshape(t, MIX_W)


def _merge_kernel(x_ref, gain_ref, oa0, oa1, oa2, la0, la1, la2, bf_ref, bb_ref, z_ref, bgain_ref,
                  oc_ref, od_ref, wg_ref, bg_ref, wbr_ref, wout_ref, out_ref, o_scr):
    x = x_ref[...]
    tm = x.shape[0]
    h = _rms_rows(x, gain_ref[...]).astype(BF16)

    slabs = MIX_W // LANES
    outs_a = [oa0[...].astype(F32)]
    for p, ((_, dil), oa) in enumerate(zip(A_PATTERNS[1:], (oa1, oa2))):
        for r in range(dil):
            for c in range(slabs):
                col = r * MIX_W + c * LANES
                o_scr[p, c, pl.ds(r, tm // dil, stride=dil), :] = oa[:, col:col + LANES].astype(F32)
        outs_a.append(jnp.concatenate([o_scr[p, c] for c in range(slabs)], axis=1))

    lses = [_expand_heads(r[...]) for r in (la0, la1, la2)]
    top = jnp.maximum(jnp.maximum(lses[0], lses[1]), lses[2])
    ws = [jnp.exp(l - top) for l in lses]
    o_a = sum(w * o for w, o in zip(ws, outs_a)) / (ws[0] + ws[1] + ws[2])

    o_b = bf_ref[...] + bb_ref[...]
    z = z_ref[...].astype(F32)
    o_b = (o_b * lax.rsqrt(_head_sums(o_b * o_b) * (1.0 / HEAD_DIM) + RMS_EPS) * bgain_ref[...]
           * (z * jax.nn.sigmoid(z)))

    branches = (o_a, o_b, oc_ref[...], od_ref[...])
    y = jnp.zeros(x.shape, F32)
    for i, o in enumerate(branches):
        gate = jax.nn.sigmoid(jnp.dot(h, wg_ref[i], preferred_element_type=F32) + bg_ref[i])
        y = y + gate * jnp.dot(o.astype(BF16), wbr_ref[i], preferred_element_type=F32)
    out_ref[...] = x + jnp.dot(y.astype(BF16), wout_ref[...], preferred_element_type=F32)


def _merge(x2d, a_outs, b_outs, z, oc, od, lw, tm):
    t = x2d.shape[0]
    tok = lambda w: pl.BlockSpec((tm, w), lambda i: (i, 0))
    (o0, l0), (o1, l1), (o2, l2) = a_outs
    acts = (o0, o1, o2, l0, l1, l2, b_outs[0], b_outs[1], z)
    weights = (lw["wg"], lw["bg"], lw["wbr"], lw["wout"])
    return pl.pallas_call(
        _merge_kernel,
        grid=(t // tm,),
        in_specs=([tok(D_MODEL), _const_spec(lw["norm_mix"].shape)]
                  + [pl.BlockSpec((tm * a.shape[0] // t, a.shape[1]), lambda i: (i, 0)) for a in acts]
                  + [_const_spec(lw["b_gain"].shape), tok(MIX_W), tok(MIX_W)]
                  + [_const_spec(w.shape) for w in weights]),
        out_specs=tok(D_MODEL),
        out_shape=jax.ShapeDtypeStruct((t, D_MODEL), F32),
        scratch_shapes=[pltpu.VMEM((len(A_PATTERNS) - 1, MIX_W // LANES, tm, LANES), F32)],
        compiler_params=_params("parallel"),
    )(x2d, lw["norm_mix"], *acts, lw["b_gain"], oc, od, *weights)


SUBLANES = 8
ROUTE_ROWS = 24
EXPERT_UNROLL = 2


def _moe_kernel(x_ref, gain_ref, wr_ref, br_ref, wgt_ref, wup_ref, wdn_ref, fgain_ref, out_ref,
                hs_ref, ws_ref, ys_ref, *, final, cap):
    x = x_ref[...]
    tm = x.shape[0]
    h = _rms_rows(x, gain_ref[...])
    hb = h.astype(BF16)
    h_lo = (h - hb.astype(F32)).astype(BF16)
    parts = jnp.dot(jnp.concatenate([hb, h_lo], axis=0), wr_ref[...], preferred_element_type=F32)
    logits = (parts[:tm, :LANES] + parts[:tm, LANES:]) + (parts[tm:, :LANES] + parts[tm:, LANES:]) + br_ref[...]
    lane = lax.broadcasted_iota(jnp.int32, (1, LANES), 1)

    lt = logits.T[0:ROUTE_ROWS]
    row = lax.broadcasted_iota(jnp.int32, (ROUTE_ROWS, 1), 0)

    def first_max(vals):
        top = jnp.max(vals, axis=0, keepdims=True)
        return top, jnp.min(jnp.where(vals == top, row, ROUTE_ROWS), axis=0, keepdims=True)

    gl = jnp.where(row < N_GROUPS, lt, NEG_INF)
    g_top, g_row = first_max(gl)
    g_prob = 1.0 / jnp.sum(jnp.exp(gl - g_top), axis=0, keepdims=True)
    in_group = (row >= N_GROUPS) & (row < N_GROUPS + N_EXPERTS) & ((row - N_GROUPS) // EXPERTS_PER_GROUP == g_row)
    el = jnp.where(in_group, lt, NEG_INF)
    top1, idx1 = first_max(el)
    top2, idx2 = first_max(jnp.where(row == idx1, NEG_INF, el))
    e2 = jnp.exp(top2 - top1)
    w1 = g_prob / (1.0 + e2)
    weights_t = jnp.where(row == idx1, w1, 0.0) + jnp.where(row == idx2, w1 * e2, 0.0)
    packed = jnp.concatenate([weights_t, jnp.broadcast_to(g_row.astype(F32), (SUBLANES, tm)),
                              jnp.zeros((LANES - ROUTE_ROWS - SUBLANES, tm), F32)], axis=0).T
    weights = jnp.where(lane < ROUTE_ROWS, packed, 0.0)
    g_idx = jnp.sum(jnp.where(lane == ROUTE_ROWS, packed, 0.0), axis=-1, keepdims=True).astype(jnp.int32)

    def expert(rows, row_weights, e, outs):
        w_e = jnp.sum(jnp.where(lane == e + N_GROUPS, row_weights, 0.0), axis=-1, keepdims=True)
        gate = jnp.dot(rows, wgt_ref[e], preferred_element_type=F32)
        up = jnp.dot(rows, wup_ref[e], preferred_element_type=F32)
        yield
        hid = gate * jax.nn.sigmoid(gate) * up * w_e
        outs.append(jnp.dot(hid.astype(BF16), wdn_ref[e], preferred_element_type=F32))

    def finish(moe_out):
        y = x + moe_out
        if final:
            y = _rms_rows(y, fgain_ref[...])
        out_ref[...] = y

    member = jnp.where(lane == g_idx, 1.0, 0.0)
    fits = jnp.max(jnp.sum(member, axis=0, keepdims=True)) <= cap

    @pl.when(fits)
    def _():
        before = lax.broadcasted_iota(jnp.int32, (tm, tm), 0) > lax.broadcasted_iota(jnp.int32, (tm, tm), 1)
        ahead = jnp.dot(jnp.where(before, 1.0, 0.0).astype(BF16), member.astype(BF16),
                        preferred_element_type=F32)
        pos = jnp.sum(member * ahead, axis=-1, keepdims=True).astype(jnp.int32)
        slot = g_idx * cap + pos
        perm_t = jnp.where(lax.broadcasted_iota(jnp.int32, (1, N_GROUPS * cap), 1) == slot, 1.0, 0.0).astype(BF16)
        slot_row = jnp.broadcast_to(slot.astype(F32), (tm, LANES)).T[0:1]
        perm = jnp.where(lax.broadcasted_iota(jnp.int32, (N_GROUPS * cap, 1), 0).astype(F32) == slot_row,
                         1.0, 0.0).astype(BF16)
        gather = lambda a: jnp.dot(perm, a, preferred_element_type=F32)
        hs_ref[...] = gather(hb).astype(BF16)
        w_hi = weights.astype(BF16)
        w_parts = gather(jnp.concatenate([w_hi, (weights - w_hi.astype(F32)).astype(BF16)], axis=1))
        ws_ref[...] = w_parts[:, :LANES] + w_parts[:, LANES:]

        def group(g, carry):
            r0 = pl.multiple_of(g * cap, cap)
            rows, row_weights = hs_ref[pl.ds(r0, cap), :], ws_ref[pl.ds(r0, cap), :]
            outs = []
            _interleave(*[expert(rows, row_weights, g * EXPERTS_PER_GROUP + j, outs)
                          for j in range(EXPERTS_PER_GROUP)])
            ys_ref[pl.ds(r0, cap), :] = sum(outs[1:], outs[0]).astype(BF16)
            return carry

        lax.fori_loop(0, N_GROUPS, group, 0)
        finish(jnp.dot(perm_t, ys_ref[...], preferred_element_type=F32))

    @pl.when(jnp.logical_not(fits))
    def _():
        def experts(i, acc):
            outs = []
            _interleave(*[expert(hb, weights, i * EXPERT_UNROLL + j, outs) for j in range(EXPERT_UNROLL)])
            return acc + sum(outs[1:], outs[0])

        finish(lax.fori_loop(0, N_EXPERTS // EXPERT_UNROLL, experts, jnp.zeros(x.shape, F32)))


def _moe(x2d, lw, final_gain, final, tm, cap):
    t = x2d.shape[0]
    tok = pl.BlockSpec((tm, D_MODEL), lambda i: (i, 0))
    once = lambda a: pl.BlockSpec(a.shape, lambda i: (0,) * a.ndim, pipeline_mode=pl.Buffered(1))
    weights = (lw["norm_ffn"], lw["w_route"], lw["b_route"], lw["w_ff_gate"], lw["w_ff_up"], lw["w_ff_down"],
               final_gain)
    return pl.pallas_call(
        functools.partial(_moe_kernel, final=final, cap=cap),
        grid=(t // tm,),
        in_specs=[tok] + [once(w) for w in weights],
        out_specs=tok,
        out_shape=jax.ShapeDtypeStruct((t, D_MODEL), F32),
        scratch_shapes=[pltpu.VMEM((N_GROUPS * cap, D_MODEL), BF16), pltpu.VMEM((N_GROUPS * cap, LANES), F32),
                        pltpu.VMEM((N_GROUPS * cap, D_MODEL), BF16)],
        compiler_params=_params("parallel"),
    )(x2d, *weights)


def _split_bf16(w):
    hi = w.astype(BF16)
    return jnp.concatenate([hi, (w - hi.astype(F32)).astype(BF16)], axis=1)


def _layer_weights(p, l):
    w_in = p["w_in"][l]
    offs = np.cumsum([0, 3 * MIX_W, 3 * MIX_W, MIX_W, 16, MIX_W, 2 * C_KV_HEADS * HEAD_DIM, 3 * MIX_W])
    sec = [w_in[:, offs[i]:offs[i + 1]] for i in range(7)]
    c_order = np.concatenate([np.arange(h * HEAD_DIM, (h + 1) * HEAD_DIM) for h in (0, 2, 1, 3)])
    row = lambda v: v.reshape(1, -1).astype(F32)
    gate16 = lambda v: jnp.concatenate([jnp.zeros((8,), F32), v.reshape(-1).astype(F32)]).reshape(1, 16)
    wbr = p["w_branch"][l]
    wbr = jnp.stack([wbr[0], wbr[1], wbr[2][c_order], wbr[3]])
    w_route = jnp.concatenate([p["w_route_group"][l], p["w_route_expert"][l].reshape(D_MODEL, N_EXPERTS)], axis=1)
    b_route = jnp.concatenate([p["b_route_group"][l], p["b_route_expert"][l].reshape(N_EXPERTS)])
    pad_lanes = lambda a: jnp.pad(a, [(0, 0)] * (a.ndim - 1) + [(0, LANES - a.shape[-1])])
    return dict(
        norm_mix=row(p["norm_mix"][l]),
        wa=sec[0].astype(BF16), wb=sec[1].astype(BF16), wz=sec[2].astype(BF16),
        wab=pad_lanes(sec[3]).astype(BF16), wcq=sec[4][:, c_order].astype(BF16), wck=sec[5][:, :C_KV_HEADS * HEAD_DIM].astype(BF16), wcvt=sec[5][:, C_KV_HEADS * HEAD_DIM:].T.astype(BF16),
        wd=sec[6].astype(BF16),
        cq_gain=row(jnp.tile(p["c_q_norm"][l], N_HEADS)), ck_gain=row(jnp.tile(p["c_k_norm"][l], C_KV_HEADS)),
        conv_w=p["conv_w"][l].astype(F32), a_log16=gate16(p["a_log"][l]), dt_bias16=gate16(p["dt_bias"][l]),
        b_gain=row(jnp.tile(p["b_norm"][l], N_HEADS)),
        wg=p["w_gate"][l].astype(BF16), bg=p["b_gate"][l].reshape(N_HEADS, 1, D_MODEL).astype(F32),
        wbr=wbr.astype(BF16), wout=p["w_out"][l].astype(BF16),
        norm_ffn=row(p["norm_ffn"][l]),
        w_route=_split_bf16(pad_lanes(w_route).astype(F32)), b_route=pad_lanes(b_route.reshape(1, -1)).astype(F32),
        w_ff_gate=p["w_ff_gate"][l].reshape(N_EXPERTS, D_MODEL, D_EXPERT).astype(BF16),
        w_ff_up=p["w_ff_up"][l].reshape(N_EXPERTS, D_MODEL, D_EXPERT).astype(BF16),
        w_ff_down=p["w_ff_down"][l].reshape(N_EXPERTS, D_EXPERT, D_MODEL).astype(BF16),
        d_rpb=p["d_rpb"][l],
    )


def _tiles(seq):
    tm = min(512, seq)
    share = tm // N_GROUPS
    moe_cap = (share + share // 4 + BF16_SUBLANES - 1) // BF16_SUBLANES * BF16_SUBLANES
    return dict(tm=tm, moe_cap=moe_cap, tq_c=min(256, seq), tk_c=min(512, seq), rows_d=min(8, seq // GRID_W),
                tq_a=128, rows_a=2048, rows_delta=min(1024, seq))


def _encoder(x, layers, final_gain):
    batch, seq, _ = x.shape
    ts = _tiles(seq)
    tables = _rotary_tables(seq)
    depth = len(layers)
    x2d = x.reshape(batch * seq, D_MODEL)
    for l, lw in enumerate(layers):
        a_views, (qkvn, z, gates, cq, ck, cvt, dqkv) = _in_proj(x2d, seq, lw, tables, ts["tm"])
        a_outs = [_band(a, batch, seq, window, dil, ts["tq_a"], ts["rows_a"])
                  for a, (window, dil) in zip(a_views, A_PATTERNS)]
        b_outs = _delta(qkvn, gates, batch, seq, ts["rows_delta"], 2 if batch % 2 == 0 else 1)
        oc = _gqa(cq, ck, cvt, batch, seq, ts["tq_c"], ts["tk_c"])
        od = _nbr(dqkv, _nbr_bias(lw["d_rpb"], min(NA_ROWS, seq // GRID_W)), batch, seq, ts["rows_d"])
        x2d = _merge(x2d, a_outs, b_outs, z, oc, od, lw, ts["tm"])
        x2d = _moe(x2d, lw, final_gain, l == depth - 1, ts["tm"], ts["moe_cap"])
    return x2d.reshape(batch, seq, D_MODEL)


def kernel(x_prompt, x_sample, norm_mix, w_in, conv_w, a_log, dt_bias, b_norm, c_q_norm, c_k_norm, d_rpb,
           w_gate, b_gate, w_branch, w_out, norm_ffn, w_route_group, b_route_group, w_route_expert,
           b_route_expert, w_ff_gate, w_ff_up, w_ff_down, norm_final):
    p = dict(norm_mix=norm_mix, w_in=w_in, conv_w=conv_w, a_log=a_log, dt_bias=dt_bias, b_norm=b_norm,
             c_q_norm=c_q_norm, c_k_norm=c_k_norm, d_rpb=d_rpb, w_gate=w_gate, b_gate=b_gate,
             w_branch=w_branch, w_out=w_out, norm_ffn=norm_ffn, w_route_group=w_route_group,
             b_route_group=b_route_group, w_route_expert=w_route_expert, b_route_expert=b_route_expert,
             w_ff_gate=w_ff_gate, w_ff_up=w_ff_up, w_ff_down=w_ff_down)
    layers = [_layer_weights(p, l) for l in range(norm_mix.shape[0])]
    final_gain = norm_final.reshape(1, -1).astype(F32)
    return _encoder(x_prompt, layers, final_gain), _encoder(x_sample, layers, final_gain)
```
